```python
import jax
import jax.numpy as jnp
from jax import lax
import numpy as np

D_MODEL = 1024
BATCH = 4
SEQ = 4096
DEPTH = 2
DEC_BATCH = 128
DEC_SEQ = 8
PAST_LEN = 8192
PAGE_SIZE = 128

HEAD_DIM = 64
POOL_WINDOWS = (2, 4, 8, 16)
POOL_CH = D_MODEL // 2
POOL_GROUP = POOL_CH // len(POOL_WINDOWS)
POOL_STATE = max(POOL_WINDOWS) - 1
NSA_HEADS = (D_MODEL - POOL_CH) // HEAD_DIM
NSA_KV_HEADS = 2
CMP_BLOCK = 32
CMP_STRIDE = 16
CMP_HIDDEN = 2 * HEAD_DIM
SLC_BLOCK = 64
SLC_TOPN = 16
NSA_WINDOW = 512
NSA_KV_COLS = NSA_KV_HEADS * 2 * HEAD_DIM
IN0_COLS = POOL_CH + NSA_HEADS * HEAD_DIM + 3 * NSA_KV_COLS + 3 * NSA_HEADS
SWA_HEADS = D_MODEL // HEAD_DIM
SWA_KV_HEADS = 2
SWA_WINDOW = 128
IN1_COLS = SWA_HEADS * HEAD_DIM + SWA_KV_HEADS * 2 * HEAD_DIM
FFN_DIM = 2816
N_EXPERTS = 8
TOP_K = 2
EXPERT_DIM = 3584
MOE_BLOCK = 256
Q_BLOCK = 128
RMS_EPS = 1e-6
NEG_INF = -1e30
FORCE_SCORE = 1e9
ATTN_SCALE = HEAD_DIM ** -0.5

kernel_name = 'hybrid_pool_nsa_swa_moe_step'


def rms_norm(x, g):
    xf = x.astype(jnp.float32)
    y = xf * lax.rsqrt(jnp.mean(xf * xf, axis=-1, keepdims=True) + RMS_EPS)
    return (y * g.astype(jnp.float32)).astype(x.dtype)


def masked_softmax(s, mask, sink=None):
    s = jnp.where(mask, s, NEG_INF)
    m = jnp.max(s, axis=-1, keepdims=True)
    if sink is not None:
        m = jnp.maximum(m, sink)
    p = jnp.where(mask, jnp.exp(s - m), 0.0)
    den = jnp.sum(p, axis=-1, keepdims=True)
    if sink is not None:
        den = den + jnp.exp(sink - m)
    return p / jnp.maximum(den, 1e-30)


def swiglu(h, w_gate, w_up, w_down):
    return (jax.nn.silu(h @ w_gate) * (h @ w_up)) @ w_down


def moe_swiglu(x2d, router_w, router_b, w_gate, w_up, w_down):
    n_tok, d = x2d.shape
    logits = x2d.astype(jnp.float32) @ router_w.astype(jnp.float32) + router_b.astype(jnp.float32)
    top_logit, top_e = lax.top_k(logits, TOP_K)
    gate = jax.nn.softmax(top_logit, axis=-1)
    n_asg = n_tok * TOP_K
    flat_e = top_e.reshape(n_asg)
    order = jnp.argsort(flat_e)
    sorted_e = flat_e[order]
    counts = jnp.bincount(flat_e, length=N_EXPERTS)
    padded = (counts + MOE_BLOCK - 1) // MOE_BLOCK * MOE_BLOCK
    start = jnp.cumsum(counts) - counts
    pend = jnp.cumsum(padded)
    pstart = pend - padded
    dest_sorted = pstart[sorted_e] + jnp.arange(n_asg) - start[sorted_e]
    dest = jnp.zeros((n_asg,), jnp.int32).at[order].set(dest_sorted.astype(jnp.int32))
    n_blk = -(-n_asg // MOE_BLOCK) + N_EXPERTS
    cap = n_blk * MOE_BLOCK
    row_tok = jnp.full((cap,), n_tok, jnp.int32).at[dest].set(jnp.arange(n_asg, dtype=jnp.int32) // TOP_K)
    blk_e = jnp.minimum(jnp.sum(jnp.arange(n_blk)[:, None] * MOE_BLOCK >= pend[None, :], axis=1), N_EXPERTS - 1)
    xs = jnp.concatenate([x2d, jnp.zeros((1, d), x2d.dtype)], axis=0)[row_tok].reshape(n_blk, MOE_BLOCK, d)

    def expert_block(args):
        xb, e = args
        return swiglu(xb, w_gate[e], w_up[e], w_down[e])

    ys = lax.map(expert_block, (xs, blk_e)).reshape(cap, d)
    return jnp.einsum('nk,nkd->nd', gate.astype(x2d.dtype), ys[dest].reshape(n_tok, TOP_K, d))


def pool_mix(u_ext, n_out, pool_w, pool_scale):
    n, length, c = u_ext.shape
    uf = u_ext.astype(jnp.float32)
    csum = jnp.concatenate([jnp.zeros((n, 1, c), jnp.float32), jnp.cumsum(uf, axis=1)], axis=1)
    l = jnp.arange(length - n_out, length)
    diffs = []
    for gi, w in enumerate(POOL_WINDOWS):
        sl = slice(gi * POOL_GROUP, (gi + 1) * POOL_GROUP)
        lo = jnp.maximum(l + 1 - w, 0)
        cnt = jnp.minimum(l + 1, w).astype(jnp.float32)
        mean = (csum[:, l + 1, sl] - csum[:, lo, sl]) / cnt[None, :, None]
        diffs.append(mean - uf[:, l, sl])
    dg = jnp.stack(diffs, axis=2).astype(pool_w.dtype)
    y = jnp.einsum('ntgc,gce->ntge', dg, pool_w).reshape(n, n_out, c)
    return y * pool_scale


def compress(rows, w1, w2, pe):
    n, length, g, d = rows.shape
    r_parts = CMP_BLOCK // CMP_STRIDE
    n_seg = length // CMP_STRIDE
    n_cmp = n_seg - r_parts + 1
    seg = rows.reshape(n, n_seg, CMP_STRIDE, g, d).transpose(0, 1, 3, 2, 4).reshape(n, n_seg, g, CMP_STRIDE * d)
    w1r = w1.reshape(r_parts, CMP_STRIDE * d, CMP_HIDDEN)
    h = pe.reshape(-1) @ w1
    for r in range(r_parts):
        h = h + jnp.einsum('nsgc,ch->nsgh', seg[:, r:r + n_cmp], w1r[r])
    return jax.nn.gelu(h) @ w2


def compress_kv(rows, k_gain, cmp):
    kw1, kw2, kpe, vw1, vw2, vpe = cmp
    kc = rms_norm(compress(rows[..., 0, :], kw1, kw2, kpe), k_gain)
    vc = compress(rows[..., 1, :], vw1, vw2, vpe)
    return kc, vc


def cmp_attention(q, kc, vc, qpos):
    n, t, h, d = q.shape
    g = kc.shape[2]
    qg = q.reshape(n, t, g, h // g, d)
    s = jnp.einsum('ntghd,nmgd->nghtm', qg, kc, preferred_element_type=jnp.float32) * ATTN_SCALE
    blk_end = jnp.arange(kc.shape[1]) * CMP_STRIDE + CMP_BLOCK - 1
    mask = blk_end[None, :] <= qpos[:, None]
    p = masked_softmax(s, mask)
    o = jnp.einsum('nghtm,nmgd->ntghd', p.astype(vc.dtype), vc)
    return o.reshape(n, t, h, d), p


def select_blocks(p_cmp, qpos, n_blocks):
    n_cmp = p_cmp.shape[-1]
    c_start = jnp.arange(n_cmp) * CMP_STRIDE
    c_end = c_start + CMP_BLOCK - 1
    b_start = jnp.arange(n_blocks) * SLC_BLOCK
    b_end = b_start + SLC_BLOCK - 1
    cover = ((c_start[:, None] <= b_end[None, :]) & (c_end[:, None] >= b_start[None, :])).astype(p_cmp.dtype)
    score = jnp.einsum('nghtm,mj->ngtj', p_cmp, cover)
    cur = (qpos // SLC_BLOCK)[:, None]
    j = jnp.arange(n_blocks)[None, :]
    forced = (j == 0) | (j == cur) | (j == cur - 1)
    valid = b_start[None, :] <= qpos[:, None]
    score = jnp.where(valid, jnp.where(forced, FORCE_SCORE, score), NEG_INF)
    _, sel = lax.top_k(score, min(SLC_TOPN, n_blocks))
    return sel


def gathered_attention(q, k, v, kpos, qpos):
    s = jnp.einsum('nghqd,ngqkd->nghqk', q, k, preferred_element_type=jnp.float32) * ATTN_SCALE
    mask = (kpos <= qpos[None, None, :, None])[:, :, None]
    p = masked_softmax(s, mask)
    return jnp.einsum('nghqk,ngqkd->nghqd', p.astype(v.dtype), v)


def prompt_selected(q, rows, sel):
    n, t, h, d = q.shape
    g = rows.shape[2]
    nb = t // SLC_BLOCK
    nqb = t // Q_BLOCK
    kb = rows.reshape(n, nb, SLC_BLOCK, g, 2, d).transpose(0, 3, 1, 2, 4, 5)
    b_idx = jnp.arange(n)[:, None, None, None]
    g_idx = jnp.arange(g)[None, :, None, None]
    offs = jnp.arange(SLC_BLOCK)
    q_b = q.reshape(n, nqb, Q_BLOCK, g, h // g, d).transpose(1, 0, 3, 4, 2, 5)
    sel_b = sel.reshape(n, g, nqb, Q_BLOCK, -1).transpose(2, 0, 1, 3, 4)
    pos_b = jnp.arange(t).reshape(nqb, Q_BLOCK)

    def one(args):
        qb, sb, pb = args
        blk = kb[b_idx, g_idx, sb]
        k = blk[..., 0, :].reshape(n, g, Q_BLOCK, -1, d)
        v = blk[..., 1, :].reshape(n, g, Q_BLOCK, -1, d)
        kpos = (sb[..., None] * SLC_BLOCK + offs).reshape(n, g, Q_BLOCK, -1)
        return gathered_attention(qb, k, v, kpos, pb)

    o = lax.map(one, (q_b, sel_b, pos_b))
    return o.transpose(1, 0, 4, 2, 3, 5).reshape(n, t, h, d)


def sample_selected(q, sel, cache_slc, page_table, new_rows):
    n, s_len, h, d = q.shape
    g = new_rows.shape[2]
    past = page_table.shape[1] * PAGE_SIZE
    n_past_blk = past // SLC_BLOCK
    n_new_blk = -(-s_len // SLC_BLOCK)
    blk_per_page = PAGE_SIZE // SLC_BLOCK
    new_b = jnp.pad(new_rows, ((0, 0), (0, n_new_blk * SLC_BLOCK - s_len), (0, 0), (0, 0), (0, 0)))
    new_b = new_b.reshape(n, n_new_blk, SLC_BLOCK, g, 2, d).transpose(0, 1, 3, 2, 4, 5)
    b_idx = jnp.arange(n)[:, None, None]
    g_idx = jnp.arange(g)[None, :, None]
    offs = jnp.arange(SLC_BLOCK)
    q_s = q.reshape(n, s_len, g, h // g, d).transpose(1, 0, 2, 3, 4)
    sel_s = sel.transpose(2, 0, 1, 3)
    pos = past + jnp.arange(s_len)

    def one(args):
        qt, st, pt = args
        pj = jnp.clip(st, 0, n_past_blk - 1)
        page = page_table[b_idx, pj // blk_per_page]
        row = (pj % blk_per_page) * SLC_BLOCK
        from_past = cache_slc[page[..., None], row[..., None] + offs, g_idx[..., None]]
        nj = jnp.clip(st - n_past_blk, 0, n_new_blk - 1)
        from_new = new_b[b_idx, nj, g_idx]
        blk = jnp.where((st >= n_past_blk)[..., None, None, None], from_new, from_past.astype(new_b.dtype))
        k = blk[..., 0, :].reshape(n, g, 1, -1, d)
        v = blk[..., 1, :].reshape(n, g, 1, -1, d)
        kpos = (st[..., None] * SLC_BLOCK + offs).reshape(n, g, 1, -1)
        return gathered_attention(qt[:, :, :, None], k, v, kpos, pt[None])[:, :, :, 0]

    o = lax.map(one, (q_s, sel_s, pos))
    return o.transpose(1, 0, 2, 3, 4).reshape(n, s_len, h, d)


def band_attention(q, k, v, window, sinks=None):
    n, t, h, d = q.shape
    g = k.shape[2]
    nqb = t // Q_BLOCK
    span = Q_BLOCK + window
    pad = ((0, 0), (window, 0), (0, 0), (0, 0))
    idx = jnp.arange(nqb)[:, None] * Q_BLOCK + jnp.arange(span)[None, :]
    kb = jnp.pad(k, pad)[:, idx]
    vb = jnp.pad(v, pad)[:, idx]
    qb = q.reshape(n, nqb, Q_BLOCK, g, h // g, d)
    s = jnp.einsum('nbqghd,nbkgd->nbghqk', qb, kb, preferred_element_type=jnp.float32) * ATTN_SCALE
    qpos = jnp.arange(nqb)[:, None] * Q_BLOCK + jnp.arange(Q_BLOCK)[None, :]
    kpos = idx - window
    diff = qpos[:, :, None] - kpos[:, None, :]
    mask = ((kpos[:, None, :] >= 0) & (diff >= 0) & (diff <= window))[None, :, None, None]
    sink = None if sinks is None else sinks.astype(jnp.float32).reshape(g, h // g, 1, 1)
    p = masked_softmax(s, mask, sink)
    o = jnp.einsum('nbghqk,nbkgd->nbqghd', p.astype(v.dtype), vb)
    return o.reshape(n, t, h, d)


def window_attention_cached(q, buf, rows, pos0, window, sinks=None):
    n, s_len, h, d = q.shape
    wb = buf.shape[1]
    kv = jnp.concatenate([buf.astype(rows.dtype), rows], axis=1)
    k, v = kv[..., 0, :], kv[..., 1, :]
    g = k.shape[2]
    kpos = pos0 - wb + jnp.arange(wb + s_len)
    qpos = pos0 + jnp.arange(s_len)
    qg = q.reshape(n, s_len, g, h // g, d)
    s = jnp.einsum('nqghd,nkgd->nghqk', qg, k, preferred_element_type=jnp.float32) * ATTN_SCALE
    diff = qpos[:, None] - kpos[None, :]
    mask = (diff >= 0) & (diff <= window)
    sink = None if sinks is None else sinks.astype(jnp.float32).reshape(g, h // g, 1, 1)
    p = masked_softmax(s, mask, sink)
    o = jnp.einsum('nghqk,nkgd->nqghd', p.astype(v.dtype), v).reshape(n, s_len, h, d)
    return o, kv[:, -wb:]


def project_even(x, norm_g, w_in, q_gain, k_gain):
    n, t, _ = x.shape
    z = rms_norm(x, norm_g) @ w_in
    c0 = POOL_CH
    c1 = c0 + NSA_HEADS * HEAD_DIM
    c2 = c1 + 3 * NSA_KV_COLS
    u = z[..., :c0]
    q = rms_norm(z[..., c0:c1].reshape(n, t, NSA_HEADS, HEAD_DIM), q_gain)
    kv = z[..., c1:c2].reshape(n, t, 3, NSA_KV_HEADS, 2, HEAD_DIM)
    gates = jax.nn.sigmoid(z[..., c2:].astype(jnp.float32)).reshape(n, t, 3, NSA_HEADS, 1).astype(x.dtype)

    def normed(b):
        return jnp.stack([rms_norm(kv[:, :, b, :, 0], k_gain[b]), kv[:, :, b, :, 1]], axis=-2)

    return u, q, kv[:, :, 0], normed(1), normed(2), gates


def even_layer_out(x, pool_out, o_cmp, o_slc, o_win, gates, w_out, norm_ffn, fg, fu, fd):
    n, t, _ = x.shape
    o_nsa = gates[:, :, 0] * o_cmp + gates[:, :, 1] * o_slc + gates[:, :, 2] * o_win
    mix = jnp.concatenate([pool_out.astype(x.dtype), o_nsa.reshape(n, t, -1)], axis=-1) @ w_out
    x = x + mix
    return x + swiglu(rms_norm(x, norm_ffn), fg, fu, fd)


def even_layer_prompt(x, p):
    norm_mix, w_in, pool_w, pool_scale, q_gain, k_gain, cmp, w_out, norm_ffn, fg, fu, fd = p
    n, t, _ = x.shape
    u, q, cmp_rows, slc_rows, win_rows, gates = project_even(x, norm_mix, w_in, q_gain, k_gain)
    pool_out = pool_mix(u, t, pool_w, pool_scale)
    qpos = jnp.arange(t)
    kc, vc = compress_kv(cmp_rows, k_gain[0], cmp)
    o_cmp, p_cmp = cmp_attention(q, kc, vc, qpos)
    sel = select_blocks(p_cmp, qpos, t // SLC_BLOCK)
    o_slc = prompt_selected(q, slc_rows, sel)
    o_win = band_attention(q, win_rows[..., 0, :], win_rows[..., 1, :], NSA_WINDOW)
    y = even_layer_out(x, pool_out, o_cmp, o_slc, o_win, gates, w_out, norm_ffn, fg, fu, fd)
    keep = min(NSA_WINDOW, t)
    return y, (u[:, t - POOL_STATE:], cmp_rows, slc_rows, win_rows[:, t - keep:])


def even_layer_sample(x, state_pool, cache_cmp, cache_slc, cache_win, page_table, p):
    norm_mix, w_in, pool_w, pool_scale, q_gain, k_gain, cmp, w_out, norm_ffn, fg, fu, fd = p
    n, s_len, _ = x.shape
    past = page_table.shape[1] * PAGE_SIZE
    total = past + s_len
    u, q, cmp_rows, slc_rows, win_rows, gates = project_even(x, norm_mix, w_in, q_gain, k_gain)
    u_ext = jnp.concatenate([state_pool.astype(u.dtype), u], axis=1)
    pool_out = pool_mix(u_ext, s_len, pool_w, pool_scale)
    qpos = past + jnp.arange(s_len)
    past_cmp = cache_cmp[page_table].reshape(n, past, NSA_KV_HEADS, 2, HEAD_DIM).astype(cmp_rows.dtype)
    padded = -(-total // CMP_STRIDE) * CMP_STRIDE
    new_cmp = jnp.pad(cmp_rows, ((0, 0), (0, padded - total), (0, 0), (0, 0), (0, 0)))
    kc, vc = compress_kv(jnp.concatenate([past_cmp, new_cmp], axis=1), k_gain[0], cmp)
    o_cmp, p_cmp = cmp_attention(q, kc, vc, qpos)
    sel = select_blocks(p_cmp, qpos, -(-total // SLC_BLOCK))
    o_slc = sample_selected(q, sel, cache_slc, page_table, slc_rows)
    o_win, new_win = window_attention_cached(q, cache_win, win_rows, past, NSA_WINDOW)
    y = even_layer_out(x, pool_out, o_cmp, o_slc, o_win, gates, w_out, norm_ffn, fg, fu, fd)
    return y, (u_ext[:, -POOL_STATE:], cmp_rows, slc_rows, new_win)


def project_odd(x, norm_g, w_in, q_gain, k_gain):
    n, t, _ = x.shape
    z = rms_norm(x, norm_g) @ w_in
    cq = SWA_HEADS * HEAD_DIM
    q = rms_norm(z[..., :cq].reshape(n, t, SWA_HEADS, HEAD_DIM), q_gain)
    kv = z[..., cq:].reshape(n, t, SWA_KV_HEADS, 2, HEAD_DIM)
    rows = jnp.stack([rms_norm(kv[..., 0, :], k_gain), kv[..., 1, :]], axis=-2)
    return q, rows


def odd_layer_out(x, o, w_out, norm_ffn, router_w, router_b, eg, eu, ed):
    n, t, _ = x.shape
    x = x + o.reshape(n, t, -1) @ w_out
    h = rms_norm(x, norm_ffn).reshape(n * t, -1)
    return x + moe_swiglu(h, router_w, router_b, eg, eu, ed).reshape(n, t, -1)


def odd_layer_prompt(x, p):
    norm_mix, w_in, q_gain, k_gain, sinks, w_out, norm_ffn, router_w, router_b, eg, eu, ed = p
    t = x.shape[1]
    q, rows = project_odd(x, norm_mix, w_in, q_gain, k_gain)
    o = band_attention(q, rows[..., 0, :], rows[..., 1, :], SWA_WINDOW, sinks)
    y = odd_layer_out(x, o, w_out, norm_ffn, router_w, router_b, eg, eu, ed)
    keep = min(SWA_WINDOW, t)
    return y, rows[:, t - keep:]


def odd_layer_sample(x, cache_swa, past, p):
    norm_mix, w_in, q_gain, k_gain, sinks, w_out, norm_ffn, router_w, router_b, eg, eu, ed = p
    q, rows = project_odd(x, norm_mix, w_in, q_gain, k_gain)
    o, new_buf = window_attention_cached(q, cache_swa, rows, past, SWA_WINDOW, sinks)
    y = odd_layer_out(x, o, w_out, norm_ffn, router_w, router_b, eg, eu, ed)
    return y, new_buf


def setup_inputs(seed: int = 0) -> dict:
    key = jax.random.key(seed)
    keys = jax.random.split(key, 40)
    counter = [0]

    def nrm(shape, scale):
        k = keys[counter[0]]
        counter[0] += 1
        return jax.random.normal(k, shape, jnp.float32) * scale

    def gain(shape):
        return 1.0 + nrm(shape, 0.02)

    d = D_MODEL
    n_pages = PAST_LEN // PAGE_SIZE
    n_used = DEC_BATCH * n_pages
    n_phys = n_used + n_used // 4
    page_table = jax.random.permutation(keys[39], n_phys)[:n_used].reshape(DEC_BATCH, n_pages).astype(jnp.int32)
    cmp_in = CMP_BLOCK * HEAD_DIM
    return {
        'x_prompt': nrm((BATCH, SEQ, d), 1.0),
        'x_sample': nrm((DEC_BATCH, DEC_SEQ, d), 1.0),
        'state_pool': nrm((DEC_BATCH, POOL_STATE, POOL_CH), 1.0),
        'cache_nsa_cmp': nrm((n_phys, PAGE_SIZE, NSA_KV_HEADS, 2, HEAD_DIM), 1.0),
        'cache_nsa_slc': nrm((n_phys, PAGE_SIZE, NSA_KV_HEADS, 2, HEAD_DIM), 1.0),
        'cache_nsa_win': nrm((DEC_BATCH, min(NSA_WINDOW, PAST_LEN), NSA_KV_HEADS, 2, HEAD_DIM), 1.0),
        'cache_swa': nrm((DEC_BATCH, min(SWA_WINDOW, PAST_LEN), SWA_KV_HEADS, 2, HEAD_DIM), 1.0),
        'page_table': page_table,
        'norm0_mix': gain((d,)),
        'w_in0': nrm((d, IN0_COLS), d ** -0.5),
        'pool_w': nrm((len(POOL_WINDOWS), POOL_GROUP, POOL_GROUP), POOL_GROUP ** -0.5),
        'pool_scale': 1.0 + nrm((POOL_CH,), 0.1),
        'nsa_q_gain': gain((HEAD_DIM,)),
        'nsa_k_gain': gain((3, HEAD_DIM)),
        'cmp_k_w1': nrm((cmp_in, CMP_HIDDEN), cmp_in ** -0.5),
        'cmp_k_w2': nrm((CMP_HIDDEN, HEAD_DIM), CMP_HIDDEN ** -0.5),
        'cmp_k_pe': nrm((CMP_BLOCK, HEAD_DIM), 0.1),
        'cmp_v_w1': nrm((cmp_in, CMP_HIDDEN), cmp_in ** -0.5),
        'cmp_v_w2': nrm((CMP_HIDDEN, HEAD_DIM), CMP_HIDDEN ** -0.5),
        'cmp_v_pe': nrm((CMP_BLOCK, HEAD_DIM), 0.1),
        'w_out0': nrm((d, d), d ** -0.5),
        'norm0_ffn': gain((d,)),
        'ffn_w_gate': nrm((d, FFN_DIM), d ** -0.5),
        'ffn_w_up': nrm((d, FFN_DIM), d ** -0.5),
        'ffn_w_down': nrm((FFN_DIM, d), FFN_DIM ** -0.5),
        'norm1_mix': gain((d,)),
        'w_in1': nrm((d, IN1_COLS), d ** -0.5),
        'swa_q_gain': gain((HEAD_DIM,)),
        'swa_k_gain': gain((HEAD_DIM,)),
        'swa_sinks': nrm((SWA_HEADS,), 0.5),
        'w_out1': nrm((d, d), d ** -0.5),
        'norm1_ffn': gain((d,)),
        'router_w': nrm((d, N_EXPERTS), d ** -0.5),
        'router_b': nrm((N_EXPERTS,), 0.01),
        'moe_w_gate': nrm((N_EXPERTS, d, EXPERT_DIM), d ** -0.5),
        'moe_w_up': nrm((N_EXPERTS, d, EXPERT_DIM), d ** -0.5),
        'moe_w_down': nrm((N_EXPERTS, EXPERT_DIM, d), EXPERT_DIM ** -0.5),
    }


def reference(x_prompt, x_sample, state_pool, cache_nsa_cmp, cache_nsa_slc, cache_nsa_win, cache_swa, page_table,
              norm0_mix, w_in0, pool_w, pool_scale, nsa_q_gain, nsa_k_gain,
              cmp_k_w1, cmp_k_w2, cmp_k_pe, cmp_v_w1, cmp_v_w2, cmp_v_pe,
              w_out0, norm0_ffn, ffn_w_gate, ffn_w_up, ffn_w_down,
              norm1_mix, w_in1, swa_q_gain, swa_k_gain, swa_sinks, w_out1, norm1_ffn,
              router_w, router_b, moe_w_gate, moe_w_up, moe_w_down):
    cmp = (cmp_k_w1, cmp_k_w2, cmp_k_pe, cmp_v_w1, cmp_v_w2, cmp_v_pe)
    even_p = (norm0_mix, w_in0, pool_w, pool_scale, nsa_q_gain, nsa_k_gain, cmp,
              w_out0, norm0_ffn, ffn_w_gate, ffn_w_up, ffn_w_down)
    odd_p = (norm1_mix, w_in1, swa_q_gain, swa_k_gain, swa_sinks, w_out1, norm1_ffn,
             router_w, router_b, moe_w_gate, moe_w_up, moe_w_down)
    past = page_table.shape[1] * PAGE_SIZE
    y_p, y_s = x_prompt, x_sample
    for layer in range(DEPTH):
        if layer % 2 == 0:
            y_p, (pool_p, cmp_p, slc_p, win_p) = even_layer_prompt(y_p, even_p)
            y_s, (pool_s, cmp_s, slc_s, win_s) = even_layer_sample(
                y_s, state_pool, cache_nsa_cmp, cache_nsa_slc, cache_nsa_win, page_table, even_p)
        else:
            y_p, swa_p = odd_layer_prompt(y_p, odd_p)
            y_s, swa_s = odd_layer_sample(y_s, cache_swa, past, odd_p)
    return (y_p, y_s, pool_p, cmp_p, slc_p, win_p, swa_p, pool_s, cmp_s, slc_s, win_s, swa_s)
```

```python
import functools

import jax
import jax.numpy as jnp
from jax import lax
from jax.experimental import pallas as pl
from jax.experimental.pallas import tpu as pltpu

F32 = jnp.float32
BF16 = jnp.bfloat16

LANES = 128
MXU_COLS = 256
HEAD_DIM = 64
D_MODEL = 1024
POOL_WINDOWS = (2, 4, 8, 16)
POOL_CH = 512
POOL_GROUP = 128
POOL_STATE = 15
POOL_HALO = 16
NSA_HEADS = 8
SWA_HEADS = 16
KV_GROUPS = 2
KV_COLS = KV_GROUPS * 2 * HEAD_DIM
CMP_BLOCK = 32
CMP_STRIDE = 16
CMP_HIDDEN = 128
SLC_BLOCK = 64
SLC_SHIFT = SLC_BLOCK.bit_length() - 1
SLC_TOPN = 16
NSA_WINDOW = 512
SWA_WINDOW = 128
PAGE_SIZE = 128
N_EXPERTS = 8
RMS_EPS = 1e-6
NEG_INF = -1e30
FORCE_SCORE = 1e9
PAD_SCORE = -3e38
ATTN_SCALE = HEAD_DIM ** -0.5
VMEM_LIMIT = 56 * 1024 * 1024


def _cparams(sem):
    return pltpu.CompilerParams(dimension_semantics=sem, vmem_limit_bytes=VMEM_LIMIT)


def _split_bf16(x):
    hi = x.astype(BF16)
    lo = (x - hi.astype(F32)).astype(BF16)
    return hi, lo


def _dot(a, b):
    return jnp.dot(a, b, preferred_element_type=F32)


def _dot_nt(a, b):
    return lax.dot_general(a, b, (((1,), (1,)), ((), ())), preferred_element_type=F32)


def _proj_kernel(x_ref, g_ref, w_ref, aux_ref, *out_refs, segs):
    x = x_ref[...]
    ms = jnp.mean(x * x, axis=-1, keepdims=True)
    h = (x * lax.rsqrt(ms + RMS_EPS) * g_ref[...]).astype(BF16)
    lane = lax.broadcasted_iota(jnp.int32, (1, LANES), 1)
    is_key = lane < HEAD_DIM
    oi = 0
    for kind, c0, width in segs:
        if kind == "qnorm":
            for pair in range(width // MXU_COLS):
                cp = c0 + pair * MXU_COLS
                z = _dot(h, w_ref[:, cp:cp + MXU_COLS])
                for hh in range(MXU_COLS // LANES):
                    zh = z[:, hh * LANES:(hh + 1) * LANES]
                    msq = jnp.sum(zh * zh, axis=-1, keepdims=True) * (1.0 / HEAD_DIM)
                    lo = cp - c0 + hh * LANES
                    gain = aux_ref[:, cp + hh * LANES:cp + (hh + 1) * LANES]
                    out_refs[oi][:, lo:lo + LANES] = zh * lax.rsqrt(msq + RMS_EPS) * gain
            oi += 1
            continue
        z = _dot(h, w_ref[:, c0:c0 + width])
        aux = aux_ref[:, c0:c0 + width]
        if kind == "raw":
            out_refs[oi][...] = z
            oi += 1
        elif kind == "sigmoid":
            out_refs[oi][...] = jax.nn.sigmoid(z)
            oi += 1
        elif kind == "kvnorm":
            for gg in range(width // LANES):
                sl = slice(gg * LANES, (gg + 1) * LANES)
                zb = z[:, sl]
                msq = jnp.sum(jnp.where(is_key, zb * zb, 0.0), axis=-1, keepdims=True) * (1.0 / HEAD_DIM)
                y = jnp.where(is_key, zb * lax.rsqrt(msq + RMS_EPS) * aux[:, sl], zb)
                out_refs[oi][:, sl] = y
                out_refs[oi + 1][:, sl] = y.astype(BF16)
            oi += 2
        else:
            raise ValueError(kind)


def _proj(x, g, w, aux, segs, out_defs, tm, name):
    m, d = x.shape
    c = w.shape[1]
    out_shape = [jax.ShapeDtypeStruct((m, wd), dt) for wd, dt in out_defs]
    out_specs = [pl.BlockSpec((tm, wd), lambda i: (i, 0)) for wd, _ in out_defs]
    return pl.pallas_call(
        functools.partial(_proj_kernel, segs=segs),
        out_shape=out_shape,
        grid=(m // tm,),
        in_specs=[
            pl.BlockSpec((tm, d), lambda i: (i, 0)),
            pl.BlockSpec((1, d), lambda i: (0, 0)),
            pl.BlockSpec((d, c), lambda i: (0, 0)),
            pl.BlockSpec((1, c), lambda i: (0, 0)),
        ],
        out_specs=out_specs,
        compiler_params=_cparams(("parallel",)),
        name=name,
    )(x, g, w, aux)


def _pool_prompt_kernel(u_ref, halo_ref, pw_ref, ps_ref, o_ref, e_ref, *, tp):
    i = pl.program_id(1)
    e_ref[0:POOL_HALO, :] = jnp.where(i > 0, halo_ref[0], 0.0)
    e_ref[POOL_HALO:, :] = u_ref[0]
    t = i * tp + lax.broadcasted_iota(jnp.int32, (tp, 1), 0)
    for gi, w in enumerate(POOL_WINDOWS):
        sl = slice(gi * POOL_GROUP, (gi + 1) * POOL_GROUP)
        tok = e_ref[POOL_HALO:POOL_HALO + tp, sl]
        s = tok
        for k in range(1, w):
            s = s + e_ref[POOL_HALO - k:POOL_HALO - k + tp, sl]
        cnt = jnp.minimum(t + 1, w).astype(F32)
        dg = s / cnt - tok
        y = _dot(dg.astype(BF16), pw_ref[gi]) * ps_ref[:, sl]
        o_ref[0, :, sl] = y.astype(o_ref.dtype)


def _pool_prompt(u, pool_w, pool_scale, tp=512):
    n, t, c = u.shape
    hb = tp // POOL_HALO
    return pl.pallas_call(
        functools.partial(_pool_prompt_kernel, tp=tp),
        out_shape=jax.ShapeDtypeStruct((n, t, c), BF16),
        grid=(n, t // tp),
        in_specs=[
            pl.BlockSpec((1, tp, c), lambda b, i: (b, i, 0)),
            pl.BlockSpec((1, POOL_HALO, c), lambda b, i: (b, jnp.maximum(i * hb - 1, 0), 0)),
            pl.BlockSpec((len(POOL_WINDOWS), POOL_GROUP, POOL_GROUP), lambda b, i: (0, 0, 0)),
            pl.BlockSpec((1, c), lambda b, i: (0, 0)),
        ],
        out_specs=pl.BlockSpec((1, tp, c), lambda b, i: (b, i, 0)),
        scratch_shapes=[pltpu.VMEM((tp + POOL_HALO, c), F32)],
        compiler_params=_cparams(("parallel", "parallel")),
        name="pool_prompt",
    )(u, u, pool_w, pool_scale)


def _pool_sample_kernel(x_ref, pw_ref, ps_ref, o_ref, *, bn, s_len):
    base = 1 + POOL_STATE
    for gi, w in enumerate(POOL_WINDOWS):
        sl = slice(gi * POOL_GROUP, (gi + 1) * POOL_GROUP)
        tok = x_ref[:, base:base + s_len, sl]
        s = tok
        for k in range(1, w):
            s = s + x_ref[:, base - k:base - k + s_len, sl]
        dg = (s / float(w) - tok).reshape(bn * s_len, POOL_GROUP)
        y = _dot(dg.astype(BF16), pw_ref[gi]) * ps_ref[:, sl]
        o_ref[:, :, sl] = y.reshape(bn, s_len, POOL_GROUP).astype(o_ref.dtype)


def _pool_sample(x_ext, pool_w, pool_scale, s_len, bn=16):
    n, l, c = x_ext.shape
    return pl.pallas_call(
        functools.partial(_pool_sample_kernel, bn=bn, s_len=s_len),
        out_shape=jax.ShapeDtypeStruct((n, s_len, c), F32),
        grid=(n // bn,),
        in_specs=[
            pl.BlockSpec((bn, l, c), lambda b: (b, 0, 0)),
            pl.BlockSpec((len(POOL_WINDOWS), POOL_GROUP, POOL_GROUP), lambda b: (0, 0, 0)),
            pl.BlockSpec((1, c), lambda b: (0, 0)),
        ],
        out_specs=pl.BlockSpec((bn, s_len, c), lambda b: (b, 0, 0)),
        compiler_params=_cparams(("parallel",)),
        name="pool_sample",
    )(x_ext, pool_w, pool_scale)


def _compress1_kernel(x_ref, w_ref, pa_ref, pb_ref):
    acc = None
    for j in range(CMP_STRIDE):
        xj = x_ref[:, j * KV_COLS:(j + 1) * KV_COLS].astype(BF16)
        d = _dot(xj, w_ref[j])
        acc = d if acc is None else acc + d
    half = 4 * CMP_HIDDEN
    pa_ref[...] = acc[:, :half]
    pb_ref[...] = acc[:, half:]


def _compress1(x, w_bd, ts):
    r, c = x.shape
    ts = min(ts, r)
    half = 4 * CMP_HIDDEN
    return pl.pallas_call(
        _compress1_kernel,
        out_shape=[jax.ShapeDtypeStruct((r, half), F32), jax.ShapeDtypeStruct((r, half), F32)],
        grid=(r // ts,),
        in_specs=[
            pl.BlockSpec((ts, c), lambda i: (i, 0)),
            pl.BlockSpec(w_bd.shape, lambda i: (0, 0, 0)),
        ],
        out_specs=[pl.BlockSpec((ts, half), lambda i: (i, 0)), pl.BlockSpec((ts, half), lambda i: (i, 0))],
        compiler_params=_cparams(("parallel",)),
        name="compress1",
    )(x, w_bd)


def _gelu_tanh(x):
    return 0.5 * x * (1.0 + jnp.tanh(0.7978845608028654 * (x + 0.044715 * (x * x * x))))


def _compress2_kernel(pa_ref, pb_ref, pbn_ref, pek_ref, pev_ref, w1k_ref, w1v_ref, w2_ref, kg_ref, o_ref):
    pa = pa_ref[0]
    pb = pb_ref[0]
    s = pa.shape[0]
    row = lax.broadcasted_iota(jnp.int32, (s, 1), 0)
    pb_next = jnp.where(row == s - 1, pbn_ref[0][0:1, :], pltpu.roll(pb, s - 1, 0))
    hk = _dot(pek_ref[...], w1k_ref[...])[0:1, :]
    hv = _dot(pev_ref[...], w1v_ref[...])[0:1, :]
    pe_h = jnp.concatenate([hk, hv, hk, hv], axis=1)
    a = _gelu_tanh(pa + pb_next + pe_h)
    o = _dot(a.astype(BF16), w2_ref[...])
    lane = lax.broadcasted_iota(jnp.int32, (1, LANES), 1)
    is_key = lane < HEAD_DIM
    for gg in range(KV_GROUPS):
        sl = slice(gg * LANES, (gg + 1) * LANES)
        ob = o[:, sl]
        msq = jnp.sum(jnp.where(is_key, ob * ob, 0.0), axis=-1, keepdims=True) * (1.0 / HEAD_DIM)
        y = jnp.where(is_key, ob * lax.rsqrt(msq + RMS_EPS) * kg_ref[...], ob)
        o_ref[0, :, sl] = y.astype(o_ref.dtype)


def _compress2(pa, pb, pb_new, pek, pev, w1k, w1v, w2_bd, kg):
    n, s, c = pa.shape
    full2 = lambda b: (0, 0)
    return pl.pallas_call(
        _compress2_kernel,
        out_shape=jax.ShapeDtypeStruct((n, s, KV_COLS), BF16),
        grid=(n,),
        in_specs=[
            pl.BlockSpec((1, s, c), lambda b: (b, 0, 0)),
            pl.BlockSpec((1, s, c), lambda b: (b, 0, 0)),
            pl.BlockSpec((1, 8, c), lambda b: (b, 0, 0)),
            pl.BlockSpec(pek.shape, full2),
            pl.BlockSpec(pev.shape, full2),
            pl.BlockSpec(w1k.shape, full2),
            pl.BlockSpec(w1v.shape, full2),
            pl.BlockSpec(w2_bd.shape, full2),
            pl.BlockSpec(kg.shape, full2),
        ],
        out_specs=pl.BlockSpec((1, s, KV_COLS), lambda b: (b, 0, 0)),
        compiler_params=_cparams(("parallel",)),
        name="compress2",
    )(pa, pb, pb_new, pek, pev, w1k, w1v, w2_bd, kg)


def _stack_heads(q, hg):
    return jnp.concatenate([q[:, h * LANES:(h + 1) * LANES] for h in range(hg)], axis=0).astype(BF16)


def _row_qpos(hg, tq, q0):
    r = lax.broadcasted_iota(jnp.int32, (hg * tq, 1), 0)
    return q0 + (r & (tq - 1))


def _store_heads(o_ref, lead, o, hg, tq):
    lane = lax.broadcasted_iota(jnp.int32, (1, LANES), 1)
    for h in range(hg):
        blk = jnp.where(lane >= HEAD_DIM, o[h * tq:(h + 1) * tq], 0.0)
        o_ref[lead + (slice(None), slice(h * LANES, (h + 1) * LANES))] = blk.astype(o_ref.dtype)


def _cmp_attn_kernel(q_ref, kc_ref, cov_ref, o_ref, sel_ref, *, bn, hg, tq, q_off, n_blocks):
    i = pl.program_id(2)
    s_len = kc_ref.shape[1]
    q0 = q_off + i * tq
    qpos = _row_qpos(hg, tq, q0)
    blk_end = lax.broadcasted_iota(jnp.int32, (1, s_len), 1) * CMP_STRIDE + (CMP_BLOCK - 1)
    cond = blk_end <= qpos
    psums = []
    for b in range(bn):
        qs = _stack_heads(q_ref[b], hg)
        kc = kc_ref[b]
        s = jnp.where(cond, _dot_nt(qs, kc), NEG_INF)
        m = jnp.max(s, axis=-1, keepdims=True)
        p = jnp.where(cond, jnp.exp(s - m), 0.0)
        den = jnp.sum(p, axis=-1, keepdims=True)
        p = p / jnp.maximum(den, 1e-30)
        o = _dot(p.astype(BF16), kc)
        _store_heads(o_ref, (b,), o, hg, tq)
        ps = p[0:tq]
        for h in range(1, hg):
            ps = ps + p[h * tq:(h + 1) * tq]
        psums.append(ps)
    ps = psums[0] if bn == 1 else jnp.concatenate(psums, axis=0)
    hi, lo = _split_bf16(ps)
    cov = cov_ref[...]
    score = _dot_nt(cov, hi) + _dot_nt(cov, lo)
    nb_pad, nl = score.shape
    j = lax.broadcasted_iota(jnp.int32, (nb_pad, nl), 0)
    t = q0 + (lax.broadcasted_iota(jnp.int32, (1, nl), 1) & (tq - 1))
    cur = t >> SLC_SHIFT
    forced = (j == 0) | (j == cur) | (j == cur - 1)
    valid = j * SLC_BLOCK <= t
    score = jnp.where(valid, jnp.where(forced, FORCE_SCORE, score), NEG_INF)
    score = jnp.where(j < n_blocks, score, PAD_SCORE)
    sel = jnp.zeros((nb_pad, nl), F32)
    for _ in range(min(SLC_TOPN, n_blocks)):
        mx = jnp.max(score, axis=0, keepdims=True)
        jm = jnp.min(jnp.where(score == mx, j, nb_pad), axis=0, keepdims=True)
        pick = j == jm
        sel = jnp.where(pick, 1.0, sel)
        score = jnp.where(pick, -jnp.inf, score)
    sel_t = sel.T
    for b in range(bn):
        sel_ref[b, 0] = sel_t[b * tq:(b + 1) * tq]


def _cmp_attn(q, kc, cov_t, *, hg, tq, bn, q_off, n_blocks, out_dtype):
    n, t, hc = q.shape
    s_len = kc.shape[1]
    nb_pad = cov_t.shape[0]
    g = KV_GROUPS
    return pl.pallas_call(
        functools.partial(_cmp_attn_kernel, bn=bn, hg=hg, tq=tq, q_off=q_off, n_blocks=n_blocks),
        out_shape=[jax.ShapeDtypeStruct((n, t, hc), out_dtype), jax.ShapeDtypeStruct((n, g, t, nb_pad), F32)],
        grid=(n // bn, g, t // tq),
        in_specs=[
            pl.BlockSpec((bn, tq, hg * LANES), lambda b, gg, i: (b, i, gg)),
            pl.BlockSpec((bn, s_len, LANES), lambda b, gg, i: (b, 0, gg)),
            pl.BlockSpec((nb_pad, s_len), lambda b, gg, i: (0, 0)),
        ],
        out_specs=[
            pl.BlockSpec((bn, tq, hg * LANES), lambda b, gg, i: (b, i, gg)),
            pl.BlockSpec((bn, 1, tq, nb_pad), lambda b, gg, i: (b, gg, i, 0)),
        ],
        compiler_params=_cparams(("parallel", "parallel", "parallel")),
        name="cmp_attn",
    )(q, kc, cov_t)


def _attn_kernel(*refs, hg, tq, tk, window, q_off, k_off, use_sel, use_e, use_sink, loop):
    refs = list(refs)
    sink_ref = refs.pop(0) if use_sink else None
    q_ref = refs.pop(0)
    kv_ref = refs.pop(0)
    sel_ref = refs.pop(0) if use_sel else None
    e_ref = refs.pop(0) if use_e else None
    o_ref, m_sc, l_sc, acc_sc = refs
    g = pl.program_id(1)
    i = pl.program_id(2)
    rows = hg * tq
    q0 = q_off + i * tq
    qs = _stack_heads(q_ref[0], hg)
    qpos = _row_qpos(hg, tq, q0)
    m_sc[...] = jnp.full((rows, 1), NEG_INF, F32)
    l_sc[...] = jnp.zeros((rows, 1), F32)
    acc_sc[...] = jnp.zeros((rows, LANES), F32)
    sel_per_head = tq % 16 != 0
    if use_sel:
        sel = sel_ref[0, 0]
        nb_pad = sel.shape[1]
        if sel_per_head:
            sel = jnp.concatenate([sel] * hg, axis=0)
        sel = sel.astype(BF16)

    def tile(jt):
        k0 = pl.multiple_of(jt * tk, tk) if loop else 0
        kv = kv_ref[0, pl.ds(k0, tk), :] if loop else kv_ref[0]
        s = _dot_nt(qs, kv)
        kpos = k_off + k0 + lax.broadcasted_iota(jnp.int32, (1, tk), 1)
        d = qpos - kpos
        if window is None:
            s = jnp.where(d >= 0, s, NEG_INF)
        else:
            s = jnp.where(lax.bitcast_convert_type(d, jnp.uint32) <= jnp.uint32(window), s, NEG_INF)
        if use_sel:
            if use_e:
                expand = e_ref[...]
            else:
                bj = lax.broadcasted_iota(jnp.int32, (nb_pad, tk), 0)
                cj = lax.broadcasted_iota(jnp.int32, (nb_pad, tk), 1)
                expand = jnp.where(bj == ((k0 + cj) >> SLC_SHIFT), 1.0, 0.0).astype(BF16)
            km = _dot(sel, expand)
            if not sel_per_head:
                km = jnp.concatenate([km] * hg, axis=0)
            s = jnp.where(km > 0.5, s, NEG_INF)
        m_prev = m_sc[...]
        m_new = jnp.maximum(m_prev, jnp.max(s, axis=-1, keepdims=True))
        alpha = jnp.exp(m_prev - m_new)
        p = jnp.exp(s - m_new)
        l_sc[...] = alpha * l_sc[...] + jnp.sum(p, axis=-1, keepdims=True)
        acc_sc[...] = alpha * acc_sc[...] + _dot(p.astype(BF16), kv)
        m_sc[...] = m_new

    if loop:
        j_hi = i + 1
        j_lo = 0 if window is None else jnp.maximum(i - window // tk, 0)

        def body(jt, c):
            tile(jt)
            return c

        lax.fori_loop(j_lo, j_hi, body, 0)
    else:
        tile(0)

    m = m_sc[...]
    l = l_sc[...]
    acc = acc_sc[...]
    if use_sink:
        sink = jnp.concatenate(
            [jnp.full((tq, 1), sink_ref[g * hg + h], F32) for h in range(hg)], axis=0)
        m_fin = jnp.maximum(m, sink)
        w = jnp.exp(m - m_fin)
        den = l * w + jnp.exp(sink - m_fin)
        o = acc * w / jnp.maximum(den, 1e-30)
    else:
        o = jnp.where(m > 0.5 * NEG_INF, acc / jnp.maximum(l, 1e-30), 0.0)
    _store_heads(o_ref, (0,), o, hg, tq)


def _attn(q, kv, *, hg, tq, tk, window, q_off, k_off, loop, out_dtype, sel=None, expand=None, sinks=None):
    n, t, hc = q.shape
    tkv = kv.shape[1]
    g = KV_GROUPS
    use_sel = sel is not None
    use_e = expand is not None
    use_sink = sinks is not None
    rows = hg * tq
    if not loop:
        assert tk == tkv and t == tq
    else:
        assert tq == tk and tkv == t
    in_specs = []
    args = []
    if use_sink:
        in_specs.append(pl.BlockSpec(memory_space=pltpu.SMEM))
        args.append(sinks)
    in_specs += [
        pl.BlockSpec((1, tq, hg * LANES), lambda b, gg, i: (b, i, gg)),
        pl.BlockSpec((1, tkv, LANES), lambda b, gg, i: (b, 0, gg)),
    ]
    args += [q, kv]
    if use_sel:
        nb_pad = sel.shape[-1]
        in_specs.append(pl.BlockSpec((1, 1, tq, nb_pad), lambda b, gg, i: (b, gg, i, 0)))
        args.append(sel)
    if use_e:
        in_specs.append(pl.BlockSpec(expand.shape, lambda b, gg, i: (0, 0)))
        args.append(expand)
    return pl.pallas_call(
        functools.partial(_attn_kernel, hg=hg, tq=tq, tk=tk, window=window, q_off=q_off, k_off=k_off,
                          use_sel=use_sel, use_e=use_e, use_sink=use_sink, loop=loop),
        out_shape=jax.ShapeDtypeStruct((n, t, hc), out_dtype),
        grid=(n, g, t // tq),
        in_specs=in_specs,
        out_specs=pl.BlockSpec((1, tq, hg * LANES), lambda b, gg, i: (b, i, gg)),
        scratch_shapes=[pltpu.VMEM((rows, 1), F32), pltpu.VMEM((rows, 1), F32), pltpu.VMEM((rows, LANES), F32)],
        compiler_params=_cparams(("parallel", "parallel", "parallel")),
        name="attn",
    )(*args)


def _out0_kernel(x_ref, pool_ref, oc_ref, os_ref, ow_ref, gt_ref, eg_ref, wa_ref, wb_ref, o_ref):
    ghi, glo = _split_bf16(gt_ref[...])
    o = None
    for b, br in enumerate((oc_ref, os_ref, ow_ref)):
        gate = _dot(ghi, eg_ref[b]) + _dot(glo, eg_ref[b])
        term = gate * br[...].astype(F32)
        o = term if o is None else o + term
    mix = _dot(pool_ref[...], wa_ref[...]) + _dot(o.astype(BF16), wb_ref[...])
    o_ref[...] = x_ref[...] + mix


def _out0(x, pool_o, o_cmp, o_slc, o_win, gates, eg, wa, wb, tm=512):
    m, d = x.shape
    hc = o_cmp.shape[1]
    row = lambda w: pl.BlockSpec((tm, w), lambda i: (i, 0))
    return pl.pallas_call(
        _out0_kernel,
        out_shape=jax.ShapeDtypeStruct((m, d), F32),
        grid=(m // tm,),
        in_specs=[row(d), row(POOL_CH), row(hc), row(hc), row(hc), row(LANES),
                  pl.BlockSpec(eg.shape, lambda i: (0, 0, 0)),
                  pl.BlockSpec(wa.shape, lambda i: (0, 0)),
                  pl.BlockSpec(wb.shape, lambda i: (0, 0))],
        out_specs=row(d),
        compiler_params=_cparams(("parallel",)),
        name="out0",
    )(x, pool_o, o_cmp, o_slc, o_win, gates, eg, wa, wb)


def _ffn_kernel(x_ref, g_ref, wg_ref, wu_ref, wd_ref, o_ref, h_sc, acc_sc):
    f = pl.program_id(1)

    @pl.when(f == 0)
    def _():
        x = x_ref[...]
        ms = jnp.mean(x * x, axis=-1, keepdims=True)
        h_sc[...] = (x * lax.rsqrt(ms + RMS_EPS) * g_ref[...]).astype(BF16)
        acc_sc[...] = x

    h = h_sc[...]
    a = _dot(h, wg_ref[...])
    u = _dot(h, wu_ref[...])
    act = (a * jax.nn.sigmoid(a) * u).astype(BF16)
    acc_sc[...] += _dot(act, wd_ref[...])

    @pl.when(f == pl.num_programs(1) - 1)
    def _():
        o_ref[...] = acc_sc[...]


def _ffn(x, g, wg, wu, wd, tm=512, nf=2):
    m, d = x.shape
    fdim = wg.shape[1]
    tf = fdim // nf
    return pl.pallas_call(
        _ffn_kernel,
        out_shape=jax.ShapeDtypeStruct((m, d), F32),
        grid=(m // tm, nf),
        in_specs=[
            pl.BlockSpec((tm, d), lambda i, f: (i, 0)),
            pl.BlockSpec((1, d), lambda i, f: (0, 0)),
            pl.BlockSpec((d, tf), lambda i, f: (0, f)),
            pl.BlockSpec((d, tf), lambda i, f: (0, f)),
            pl.BlockSpec((tf, d), lambda i, f: (f, 0)),
        ],
        out_specs=pl.BlockSpec((tm, d), lambda i, f: (i, 0)),
        scratch_shapes=[pltpu.VMEM((tm, d), BF16), pltpu.VMEM((tm, d), F32)],
        compiler_params=_cparams(("parallel", "arbitrary")),
        name="ffn",
    )(x, g, wg, wu, wd)


def _out1_kernel(x_ref, o_ref_in, w_ref, g_ref, rwh_ref, rwl_ref, rb_ref, xo_ref, h_ref, r_ref):
    x = x_ref[...] + _dot(o_ref_in[...], w_ref[...])
    xo_ref[...] = x
    ms = jnp.mean(x * x, axis=-1, keepdims=True)
    h = x * lax.rsqrt(ms + RMS_EPS) * g_ref[...]
    h_ref[...] = h.astype(BF16)
    hi, lo = _split_bf16(h)
    logits = _dot(hi, rwh_ref[...]) + _dot(lo, rwh_ref[...]) + _dot(hi, rwl_ref[...]) + rb_ref[...]
    lane = lax.broadcasted_iota(jnp.int32, logits.shape, 1)
    logits = jnp.where(lane < N_EXPERTS, logits, -jnp.inf)
    m1 = jnp.max(logits, axis=-1, keepdims=True)
    i1 = jnp.min(jnp.where(logits == m1, lane, LANES), axis=-1, keepdims=True)
    rest = jnp.where(lane == i1, -jnp.inf, logits)
    m2 = jnp.max(rest, axis=-1, keepdims=True)
    i2 = jnp.min(jnp.where(rest == m2, lane, LANES), axis=-1, keepdims=True)
    e2 = jnp.exp(m2 - m1)
    g1 = 1.0 / (1.0 + e2)
    g2 = e2 / (1.0 + e2)
    r = jnp.where(lane == 0, i1.astype(F32), 0.0)
    r = jnp.where(lane == 1, i2.astype(F32), r)
    r = jnp.where(lane == 2, g1, r)
    r = jnp.where(lane == 3, g2, r)
    r_ref[...] = r


def _out1(x, o, w, g, rwh, rwl, rb, tm=512):
    m, d = x.shape
    hc = o.shape[1]
    row = lambda wd: pl.BlockSpec((tm, wd), lambda i: (i, 0))
    full = lambda a: pl.BlockSpec(a.shape, lambda i: (0, 0))
    return pl.pallas_call(
        _out1_kernel,
        out_shape=[jax.ShapeDtypeStruct((m, d), F32), jax.ShapeDtypeStruct((m, d), BF16),
                   jax.ShapeDtypeStruct((m, LANES), F32)],
        grid=(m // tm,),
        in_specs=[row(d), row(hc), full(w), full(g), full(rwh), full(rwl), full(rb)],
        out_specs=[row(d), row(d), row(LANES)],
        compiler_params=_cparams(("parallel",)),
        name="out1",
    )(x, o, w, g, rwh, rwl, rb)


def _moe_kernel(be_ref, nb_ref, x_ref, wg_ref, wu_ref, wd_ref, o_ref, acc_sc):
    b = pl.program_id(0)
    f = pl.program_id(1)
    used = b < nb_ref[0]

    @pl.when(f == 0)
    def _():
        acc_sc[...] = jnp.zeros_like(acc_sc)

    @pl.when(used)
    def _():
        x = x_ref[...]
        a = _dot(x, wg_ref[0])
        u = _dot(x, wu_ref[0])
        act = (a * jax.nn.sigmoid(a) * u).astype(BF16)
        acc_sc[...] += _dot(act, wd_ref[0])

    @pl.when(f == pl.num_programs(1) - 1)
    def _():
        o_ref[...] = acc_sc[...]


def _moe(xs, blk_e, n_used, wg, wu, wd, tm, tf):
    cap, d = xs.shape
    fdim = wg.shape[2]
    grid_spec = pltpu.PrefetchScalarGridSpec(
        num_scalar_prefetch=2,
        grid=(cap // tm, fdim // tf),
        in_specs=[
            pl.BlockSpec((tm, d), lambda b, f, be, nb: (b, 0)),
            pl.BlockSpec((1, d, tf), lambda b, f, be, nb: (be[b], 0, f)),
            pl.BlockSpec((1, d, tf), lambda b, f, be, nb: (be[b], 0, f)),
            pl.BlockSpec((1, tf, d), lambda b, f, be, nb: (be[b], f, 0)),
        ],
        out_specs=pl.BlockSpec((tm, d), lambda b, f, be, nb: (b, 0)),
        scratch_shapes=[pltpu.VMEM((tm, d), F32)],
    )
    return pl.pallas_call(
        _moe_kernel,
        out_shape=jax.ShapeDtypeStruct((cap, d), F32),
        grid_spec=grid_spec,
        compiler_params=_cparams(("parallel", "arbitrary")),
        name="moe",
    )(blk_e, n_used, xs, wg, wu, wd)


def _pad_heads_cols(w, heads):
    d = w.shape[0]
    w = w.reshape(d, heads, HEAD_DIM)
    return jnp.pad(w, ((0, 0), (0, 0), (0, LANES - HEAD_DIM))).reshape(d, heads * LANES)


def _pad_heads_rows(w, heads):
    d = w.shape[1]
    w = w.reshape(heads, HEAD_DIM, d)
    return jnp.pad(w, ((0, 0), (LANES - HEAD_DIM, 0), (0, 0))).reshape(heads * LANES, d)


def _head_gain(gain, heads, scale):
    g = jnp.pad(gain.astype(F32) * scale, (0, LANES - HEAD_DIM))
    return jnp.tile(g, heads)


def _key_gain(gain):
    return jnp.concatenate([gain.astype(F32), jnp.ones((HEAD_DIM,), F32)])


def _cover_t(n_cmp_pad, n_cmp, n_blocks, nb_pad):
    m = jnp.arange(n_cmp_pad)
    c_start = m * CMP_STRIDE
    c_end = c_start + CMP_BLOCK - 1
    jb = jnp.arange(nb_pad)
    b_start = jb * SLC_BLOCK
    b_end = b_start + SLC_BLOCK - 1
    cov = (c_start[None, :] <= b_end[:, None]) & (c_end[None, :] >= b_start[:, None])
    cov = cov & (m[None, :] < n_cmp) & (jb[:, None] < n_blocks)
    return cov.astype(BF16)


def _route(r, n_tok, tm):
    top_e = r[:, 0:2].astype(jnp.int32)
    gate = r[:, 2:4]
    n_asg = n_tok * 2
    flat_e = top_e.reshape(n_asg)
    order = jnp.argsort(flat_e)
    sorted_e = flat_e[order]
    counts = jnp.sum(flat_e[:, None] == jnp.arange(N_EXPERTS)[None, :], axis=0).astype(jnp.int32)
    padded = (counts + tm - 1) // tm * tm
    start = jnp.cumsum(counts) - counts
    pend = jnp.cumsum(padded)
    pstart = pend - padded
    dest_sorted = pstart[sorted_e] + jnp.arange(n_asg, dtype=jnp.int32) - start[sorted_e]
    dest = jnp.zeros((n_asg,), jnp.int32).at[order].set(dest_sorted.astype(jnp.int32))
    n_blk = n_asg // tm + N_EXPERTS
    cap = n_blk * tm
    row_tok = jnp.full((cap,), n_tok, jnp.int32).at[dest].set(jnp.arange(n_asg, dtype=jnp.int32) // 2)
    blk_e = jnp.minimum(jnp.sum(jnp.arange(n_blk)[:, None] * tm >= pend[None, :], axis=1), N_EXPERTS - 1)
    n_used = (pend[-1] // tm).astype(jnp.int32).reshape(1)
    return gate, dest, row_tok, blk_e.astype(jnp.int32), n_used


def kernel(x_prompt, x_sample, state_pool, cache_nsa_cmp, cache_nsa_slc, cache_nsa_win, cache_swa, page_table, norm0_mix, w_in0, pool_w, pool_scale, nsa_q_gain, nsa_k_gain, cmp_k_w1, cmp_k_w2, cmp_k_pe, cmp_v_w1, cmp_v_w2, cmp_v_pe, w_out0, norm0_ffn, ffn_w_gate, ffn_w_up, ffn_w_down, norm1_mix, w_in1, swa_q_gain, swa_k_gain, swa_sinks, w_out1, norm1_ffn, router_w, router_b, moe_w_gate, moe_w_up, moe_w_down):
    nb, t, d = x_prompt.shape
    ns, s_len, _ = x_sample.shape
    mp = nb * t
    msz = ns * s_len
    past = page_table.shape[1] * PAGE_SIZE
    n_phys = cache_nsa_cmp.shape[0]
    total = past + s_len
    g = KV_GROUPS

    x_all = jnp.concatenate([x_prompt.reshape(mp, d), x_sample.reshape(msz, d)], axis=0)

    c_q = POOL_CH
    c_kv = c_q + NSA_HEADS * HEAD_DIM
    c_gate = c_kv + 3 * KV_COLS
    n_gate = 3 * NSA_HEADS
    w0 = jnp.concatenate([
        w_in0[:, :c_q],
        _pad_heads_cols(w_in0[:, c_q:c_kv], NSA_HEADS),
        w_in0[:, c_kv:c_gate],
        jnp.pad(w_in0[:, c_gate:], ((0, 0), (0, LANES - n_gate))),
    ], axis=1).astype(BF16)
    qw = NSA_HEADS * LANES
    aux0 = jnp.concatenate([
        jnp.ones((POOL_CH,), F32),
        _head_gain(nsa_q_gain, NSA_HEADS, ATTN_SCALE),
        jnp.ones((KV_COLS,), F32),
        jnp.tile(_key_gain(nsa_k_gain[1]), g),
        jnp.tile(_key_gain(nsa_k_gain[2]), g),
        jnp.ones((LANES,), F32),
    ]).reshape(1, -1)
    o0 = POOL_CH
    o1 = o0 + qw
    o2 = o1 + KV_COLS
    o3 = o2 + KV_COLS
    o4 = o3 + KV_COLS
    segs0 = (("raw", 0, POOL_CH), ("qnorm", o0, qw), ("raw", o1, KV_COLS), ("kvnorm", o2, KV_COLS),
             ("kvnorm", o3, KV_COLS), ("sigmoid", o4, LANES))
    outs0 = ((POOL_CH, F32), (qw, F32), (KV_COLS, F32), (KV_COLS, F32), (KV_COLS, BF16),
             (KV_COLS, F32), (KV_COLS, BF16), (LANES, F32))
    u_all, q_all, cmp_all, slc_all, slc_bf, win_all, win_bf, gates_all = _proj(
        x_all, norm0_mix.reshape(1, d), w0, aux0, segs0, outs0, 512, "proj0")

    pool_w_bf = pool_w.astype(BF16)
    pool_scale2 = pool_scale.reshape(1, POOL_CH).astype(F32)

    u_p = u_all[:mp].reshape(nb, t, POOL_CH)
    u_s = u_all[mp:].reshape(ns, s_len, POOL_CH)
    pool_o_p = _pool_prompt(u_p, pool_w_bf, pool_scale2)
    u_ext = jnp.concatenate([state_pool.astype(F32), u_s], axis=1)
    x_ext = jnp.pad(u_ext, ((0, 0), (1, 0), (0, 0)))
    pool_o_s = _pool_sample(x_ext, pool_w_bf, pool_scale2, s_len)
    pool_o = jnp.concatenate([pool_o_p.reshape(mp, POOL_CH), pool_o_s.reshape(msz, POOL_CH).astype(BF16)], axis=0)

    def w1_parts(w1):
        return w1.reshape(2, CMP_STRIDE, HEAD_DIM, CMP_HIDDEN)

    w_sel = jnp.stack([w1_parts(cmp_k_w1), w1_parts(cmp_v_w1)] * g, axis=0)
    w1_bd = jnp.einsum("cpjdh,ce->jcdpeh", w_sel, jnp.eye(2 * g, dtype=F32))
    w1_bd = w1_bd.reshape(CMP_STRIDE, KV_COLS, 2 * 2 * g * CMP_HIDDEN).astype(BF16)
    w2_sel = jnp.stack([cmp_k_w2, cmp_v_w2] * g, axis=0)
    w2_bd = jnp.einsum("chd,ce->ched", w2_sel, jnp.eye(2 * g, dtype=F32))
    w2_bd = w2_bd.reshape(2 * g * CMP_HIDDEN, KV_COLS).astype(BF16)
    pek = jnp.pad(cmp_k_pe.reshape(1, -1), ((0, 7), (0, 0))).astype(BF16)
    pev = jnp.pad(cmp_v_pe.reshape(1, -1), ((0, 7), (0, 0))).astype(BF16)
    w1k = cmp_k_w1.astype(BF16)
    w1v = cmp_v_w1.astype(BF16)
    kg0 = _key_gain(nsa_k_gain[0]).reshape(1, LANES)
    seg_cols = CMP_STRIDE * KV_COLS
    half = 2 * g * CMP_HIDDEN

    n_seg_p = t // CMP_STRIDE
    pa_p, pb_p = _compress1(cmp_all[:mp].reshape(nb * n_seg_p, seg_cols), w1_bd, 256)
    kcv_p = _compress2(pa_p.reshape(nb, n_seg_p, half), pb_p.reshape(nb, n_seg_p, half),
                       jnp.zeros((nb, 8, half), F32), pek, pev, w1k, w1v, w2_bd, kg0)
    n_seg_s = past // CMP_STRIDE
    past_cmp = cache_nsa_cmp.reshape(n_phys, PAGE_SIZE * KV_COLS)[page_table]
    pa_s, pb_s = _compress1(past_cmp.reshape(ns * n_seg_s, seg_cols), w1_bd, 256)
    pad_new = -(-total // CMP_STRIDE) * CMP_STRIDE - past
    new_seg = jnp.pad(cmp_all[mp:].reshape(ns, s_len * KV_COLS), ((0, 0), (0, (pad_new - s_len) * KV_COLS)))
    _, pb_new = _compress1(new_seg, w1_bd, ns)
    pb_new = jnp.pad(pb_new.reshape(ns, 1, half), ((0, 0), (0, 7), (0, 0)))
    kcv_s = _compress2(pa_s.reshape(ns, n_seg_s, half), pb_s.reshape(ns, n_seg_s, half),
                       pb_new, pek, pev, w1k, w1v, w2_bd, kg0)

    hg0 = NSA_HEADS // g
    q_p = q_all[:mp].reshape(nb, t, qw)
    q_s = q_all[mp:].reshape(ns, s_len, qw)
    nblk_p = t // SLC_BLOCK
    nbp_p = -(-nblk_p // LANES) * LANES
    cov_p = _cover_t(n_seg_p, n_seg_p - 1, nblk_p, nbp_p)
    o_cmp_p, sel_p = _cmp_attn(q_p, kcv_p, cov_p, hg=hg0, tq=256, bn=1, q_off=0, n_blocks=nblk_p,
                               out_dtype=BF16)
    nblk_s = -(-total // SLC_BLOCK)
    nbp_s = -(-nblk_s // LANES) * LANES
    n_cmp_s = -(-total // CMP_STRIDE) - 1
    cov_s = _cover_t(n_seg_s, n_cmp_s, nblk_s, nbp_s)
    o_cmp_s, sel_s = _cmp_attn(q_s, kcv_s, cov_s, hg=hg0, tq=s_len, bn=LANES // s_len, q_off=past,
                               n_blocks=nblk_s, out_dtype=F32)

    slc_p = slc_bf[:mp].reshape(nb, t, KV_COLS)
    o_slc_p = _attn(q_p, slc_p, hg=hg0, tq=256, tk=256, window=None, q_off=0, k_off=0, loop=True, sel=sel_p,
                    out_dtype=BF16)
    tk_s = -(-total // 512) * 512
    past_slc = cache_nsa_slc.reshape(n_phys, PAGE_SIZE, KV_COLS)[page_table].astype(BF16)
    kv_slc_s = jnp.concatenate([
        past_slc.reshape(ns, past, KV_COLS), slc_bf[mp:].reshape(ns, s_len, KV_COLS),
        jnp.zeros((ns, tk_s - total, KV_COLS), BF16)], axis=1)
    expand_s = (jnp.arange(tk_s)[None, :] // SLC_BLOCK == jnp.arange(nbp_s)[:, None]).astype(BF16)
    o_slc_s = _attn(q_s, kv_slc_s, hg=hg0, tq=s_len, tk=tk_s, window=None, q_off=past, k_off=0, loop=False,
                    sel=sel_s, expand=expand_s, out_dtype=F32)

    win_p_rows = win_bf[:mp].reshape(nb, t, KV_COLS)
    o_win_p = _attn(q_p, win_p_rows, hg=hg0, tq=256, tk=256, window=NSA_WINDOW, q_off=0, k_off=0, loop=True,
                    out_dtype=BF16)
    wb_len = cache_nsa_win.shape[1]
    tk_w = -(-(wb_len + s_len) // 16) * 16
    kv_win_s = jnp.concatenate([
        cache_nsa_win.astype(BF16).reshape(ns, wb_len, KV_COLS), win_bf[mp:].reshape(ns, s_len, KV_COLS),
        jnp.zeros((ns, tk_w - wb_len - s_len, KV_COLS), BF16)], axis=1)
    o_win_s = _attn(q_s, kv_win_s, hg=hg0, tq=s_len, tk=tk_w, window=NSA_WINDOW, q_off=past,
                    k_off=past - wb_len, loop=False, out_dtype=F32)

    cat = lambda a, b: jnp.concatenate([a.reshape(mp, -1), b.reshape(msz, -1).astype(BF16)], axis=0)
    eg = (jnp.arange(LANES)[None, :, None] ==
          (jnp.arange(3)[:, None, None] * NSA_HEADS + jnp.arange(qw)[None, None, :] // LANES)).astype(BF16)
    wa0 = w_out0[:POOL_CH].astype(BF16)
    wb0 = _pad_heads_rows(w_out0[POOL_CH:], NSA_HEADS).astype(BF16)
    x1 = _out0(x_all, pool_o, cat(o_cmp_p, o_cmp_s), cat(o_slc_p, o_slc_s), cat(o_win_p, o_win_s),
               gates_all, eg, wa0, wb0)
    x2 = _ffn(x1, norm0_ffn.reshape(1, d), ffn_w_gate.astype(BF16), ffn_w_up.astype(BF16),
              ffn_w_down.astype(BF16))

    c_q1 = SWA_HEADS * HEAD_DIM
    qw1 = SWA_HEADS * LANES
    w1p = jnp.concatenate([_pad_heads_cols(w_in1[:, :c_q1], SWA_HEADS), w_in1[:, c_q1:]], axis=1).astype(BF16)
    aux1 = jnp.concatenate([_head_gain(swa_q_gain, SWA_HEADS, ATTN_SCALE),
                            jnp.tile(_key_gain(swa_k_gain), g)]).reshape(1, -1)
    segs1 = (("qnorm", 0, qw1), ("kvnorm", qw1, KV_COLS))
    outs1 = ((qw1, F32), (KV_COLS, F32), (KV_COLS, BF16))
    q1_all, swa_all, swa_bf = _proj(x2, norm1_mix.reshape(1, d), w1p, aux1, segs1, outs1, 512, "proj1")
    hg1 = SWA_HEADS // g
    sinks = swa_sinks.astype(F32)
    q1_p = q1_all[:mp].reshape(nb, t, qw1)
    q1_s = q1_all[mp:].reshape(ns, s_len, qw1)
    o1_p = _attn(q1_p, swa_bf[:mp].reshape(nb, t, KV_COLS), hg=hg1, tq=128, tk=128, window=SWA_WINDOW,
                 q_off=0, k_off=0, loop=True, sinks=sinks, out_dtype=BF16)
    sb_len = cache_swa.shape[1]
    tk_1 = -(-(sb_len + s_len) // 16) * 16
    kv_swa_s = jnp.concatenate([
        cache_swa.astype(BF16).reshape(ns, sb_len, KV_COLS), swa_bf[mp:].reshape(ns, s_len, KV_COLS),
        jnp.zeros((ns, tk_1 - sb_len - s_len, KV_COLS), BF16)], axis=1)
    o1_s = _attn(q1_s, kv_swa_s, hg=hg1, tq=s_len, tk=tk_1, window=SWA_WINDOW, q_off=past,
                 k_off=past - sb_len, loop=False, sinks=sinks, out_dtype=F32)

    rw = jnp.pad(router_w.astype(F32), ((0, 0), (0, LANES - N_EXPERTS)))
    rwh, rwl = _split_bf16(rw)
    rb = jnp.pad(router_b.astype(F32), (0, LANES - N_EXPERTS)).reshape(1, LANES)
    x3, h3, r = _out1(x2, cat(o1_p, o1_s), _pad_heads_rows(w_out1, SWA_HEADS).astype(BF16),
                      norm1_ffn.reshape(1, d), rwh, rwl, rb)
    m_all = mp + msz
    tm_moe = 512
    gate, dest, row_tok, blk_e, n_used = _route(r, m_all, tm_moe)
    xs = jnp.concatenate([h3, jnp.zeros((1, d), BF16)], axis=0)[row_tok]
    ys = _moe(xs, blk_e, n_used, moe_w_gate.astype(BF16), moe_w_up.astype(BF16), moe_w_down.astype(BF16),
              tm_moe, 512)
    y_moe = jnp.einsum("nk,nkd->nd", gate, ys[dest].reshape(m_all, 2, d))
    x4 = x3 + y_moe

    y_p = x4[:mp].reshape(nb, t, d)
    y_s = x4[mp:].reshape(ns, s_len, d)
    rows5 = lambda a, n, l: a.reshape(n, l, g, 2, HEAD_DIM)
    keep_w = min(NSA_WINDOW, t)
    keep_s = min(SWA_WINDOW, t)
    pool_p = u_p[:, t - POOL_STATE:]
    cmp_p = rows5(cmp_all[:mp], nb, t)
    slc_p_out = rows5(slc_all[:mp], nb, t)
    win_p_out = rows5(win_all[:mp], nb, t)[:, t - keep_w:]
    swa_p_out = rows5(swa_all[:mp], nb, t)[:, t - keep_s:]
    pool_s = u_ext[:, -POOL_STATE:]
    cmp_s = rows5(cmp_all[mp:], ns, s_len)
    slc_s_out = rows5(slc_all[mp:], ns, s_len)
    win_s_out = jnp.concatenate([cache_nsa_win.astype(F32), rows5(win_all[mp:], ns, s_len)], axis=1)[:, -wb_len:]
    swa_s_out = jnp.concatenate([cache_swa.astype(F32), rows5(swa_all[mp:], ns, s_len)], axis=1)[:, -sb_len:]
    return (y_p, y_s, pool_p, cmp_p, slc_p_out, win_p_out, swa_p_out, pool_s, cmp_s, slc_s_out, win_s_out,
            swa_s_out)
```

```python
import functools

import jax
import jax.numpy as jnp
from jax import lax
from jax.experimental import pallas as pl
from jax.experimental.pallas import tpu as pltpu

F32 = jnp.float32
BF16 = jnp.bfloat16

LANES = 128
MXU_COLS = 256
HEAD_DIM = 64
D_MODEL = 1024
POOL_WINDOWS = (2, 4, 8, 16)
POOL_CH = 512
POOL_GROUP = 128
POOL_STATE = 15
POOL_HALO = 16
NSA_HEADS = 8
SWA_HEADS = 16
KV_GROUPS = 2
KV_COLS = KV_GROUPS * 2 * HEAD_DIM
CMP_BLOCK = 32
CMP_STRIDE = 16
CMP_HIDDEN = 128
SLC_BLOCK = 64
SLC_SHIFT = SLC_BLOCK.bit_length() - 1
SLC_TOPN = 16
NSA_WINDOW = 512
SWA_WINDOW = 128
PAGE_SIZE = 128
N_EXPERTS = 8
RMS_EPS = 1e-6
NEG_INF = -1e30
FORCE_SCORE = 1e9
PAD_SCORE = -3e38
ATTN_SCALE = HEAD_DIM ** -0.5
VMEM_LIMIT = 56 * 1024 * 1024


def _cparams(sem):
    return pltpu.CompilerParams(dimension_semantics=sem, vmem_limit_bytes=VMEM_LIMIT)


def _split_bf16(x):
    hi = x.astype(BF16)
    lo = (x - hi.astype(F32)).astype(BF16)
    return hi, lo


def _dot(a, b):
    return jnp.dot(a, b, preferred_element_type=F32)


def _dot_nt(a, b):
    return lax.dot_general(a, b, (((1,), (1,)), ((), ())), preferred_element_type=F32)


def _proj_kernel(x_ref, g_ref, w_ref, aux_ref, *out_refs, segs):
    x = x_ref[...]
    ms = jnp.mean(x * x, axis=-1, keepdims=True)
    h = (x * lax.rsqrt(ms + RMS_EPS) * g_ref[...]).astype(BF16)
    lane = lax.broadcasted_iota(jnp.int32, (1, LANES), 1)
    is_key = lane < HEAD_DIM
    oi = 0
    for kind, c0, width in segs:
        if kind == "qnorm":
            for pair in range(width // MXU_COLS):
                cp = c0 + pair * MXU_COLS
                z = _dot(h, w_ref[:, cp:cp + MXU_COLS])
                for hh in range(MXU_COLS // LANES):
                    zh = z[:, hh * LANES:(hh + 1) * LANES]
                    msq = jnp.sum(zh * zh, axis=-1, keepdims=True) * (1.0 / HEAD_DIM)
                    lo = cp - c0 + hh * LANES
                    gain = aux_ref[:, cp + hh * LANES:cp + (hh + 1) * LANES]
                    out_refs[oi][:, lo:lo + LANES] = zh * lax.rsqrt(msq + RMS_EPS) * gain
            oi += 1
            continue
        z = _dot(h, w_ref[:, c0:c0 + width])
        aux = aux_ref[:, c0:c0 + width]
        if kind == "raw":
            out_refs[oi][...] = z
            oi += 1
        elif kind == "sigmoid":
            out_refs[oi][...] = jax.nn.sigmoid(z)
            oi += 1
        elif kind == "kvnorm":
            for gg in range(width // LANES):
                sl = slice(gg * LANES, (gg + 1) * LANES)
                zb = z[:, sl]
                msq = jnp.sum(jnp.where(is_key, zb * zb, 0.0), axis=-1, keepdims=True) * (1.0 / HEAD_DIM)
                y = jnp.where(is_key, zb * lax.rsqrt(msq + RMS_EPS) * aux[:, sl], zb)
                out_refs[oi][:, sl] = y
                out_refs[oi + 1][:, sl] = y.astype(BF16)
            oi += 2
        else:
            raise ValueError(kind)


def _proj(x, g, w, aux, segs, out_defs, tm, name):
    m, d = x.shape
    c = w.shape[1]
    out_shape = [jax.ShapeDtypeStruct((m, wd), dt) for wd, dt in out_defs]
    out_specs = [pl.BlockSpec((tm, wd), lambda i: (i, 0)) for wd, _ in out_defs]
    return pl.pallas_call(
        functools.partial(_proj_kernel, segs=segs),
        out_shape=out_shape,
        grid=(m // tm,),
        in_specs=[
            pl.BlockSpec((tm, d), lambda i: (i, 0)),
            pl.BlockSpec((1, d), lambda i: (0, 0)),
            pl.BlockSpec((d, c), lambda i: (0, 0)),
            pl.BlockSpec((1, c), lambda i: (0, 0)),
        ],
        out_specs=out_specs,
        compiler_params=_cparams(("parallel",)),
        name=name,
    )(x, g, w, aux)


def _pool_prompt_kernel(u_ref, halo_ref, pw_ref, ps_ref, o_ref, e_ref, *, tp):
    i = pl.program_id(1)
    e_ref[0:POOL_HALO, :] = jnp.where(i > 0, halo_ref[0], 0.0)
    e_ref[POOL_HALO:, :] = u_ref[0]
    t = i * tp + lax.broadcasted_iota(jnp.int32, (tp, 1), 0)
    for gi, w in enumerate(POOL_WINDOWS):
        sl = slice(gi * POOL_GROUP, (gi + 1) * POOL_GROUP)
        tok = e_ref[POOL_HALO:POOL_HALO + tp, sl]
        s = tok
        for k in range(1, w):
            s = s + e_ref[POOL_HALO - k:POOL_HALO - k + tp, sl]
        cnt = jnp.minimum(t + 1, w).astype(F32)
        dg = s / cnt - tok
        y = _dot(dg.astype(BF16), pw_ref[gi]) * ps_ref[:, sl]
        o_ref[0, :, sl] = y.astype(o_ref.dtype)


def _pool_prompt(u, pool_w, pool_scale, tp=512):
    n, t, c = u.shape
    hb = tp // POOL_HALO
    return pl.pallas_call(
        functools.partial(_pool_prompt_kernel, tp=tp),
        out_shape=jax.ShapeDtypeStruct((n, t, c), BF16),
        grid=(n, t // tp),
        in_specs=[
            pl.BlockSpec((1, tp, c), lambda b, i: (b, i, 0)),
            pl.BlockSpec((1, POOL_HALO, c), lambda b, i: (b, jnp.maximum(i * hb - 1, 0), 0)),
            pl.BlockSpec((len(POOL_WINDOWS), POOL_GROUP, POOL_GROUP), lambda b, i: (0, 0, 0)),
            pl.BlockSpec((1, c), lambda b, i: (0, 0)),
        ],
        out_specs=pl.BlockSpec((1, tp, c), lambda b, i: (b, i, 0)),
        scratch_shapes=[pltpu.VMEM((tp + POOL_HALO, c), F32)],
        compiler_params=_cparams(("parallel", "parallel")),
        name="pool_prompt",
    )(u, u, pool_w, pool_scale)


def _pool_sample_kernel(x_ref, pw_ref, ps_ref, o_ref, *, bn, s_len):
    base = 1 + POOL_STATE
    for gi, w in enumerate(POOL_WINDOWS):
        sl = slice(gi * POOL_GROUP, (gi + 1) * POOL_GROUP)
        tok = x_ref[:, base:base + s_len, sl]
        s = tok
        for k in range(1, w):
            s = s + x_ref[:, base - k:base - k + s_len, sl]
        dg = (s / float(w) - tok).reshape(bn * s_len, POOL_GROUP)
        y = _dot(dg.astype(BF16), pw_ref[gi]) * ps_ref[:, sl]
        o_ref[:, :, sl] = y.reshape(bn, s_len, POOL_GROUP).astype(o_ref.dtype)


def _pool_sample(x_ext, pool_w, pool_scale, s_len, bn=16):
    n, l, c = x_ext.shape
    return pl.pallas_call(
        functools.partial(_pool_sample_kernel, bn=bn, s_len=s_len),
        out_shape=jax.ShapeDtypeStruct((n, s_len, c), F32),
        grid=(n // bn,),
        in_specs=[
            pl.BlockSpec((bn, l, c), lambda b: (b, 0, 0)),
            pl.BlockSpec((len(POOL_WINDOWS), POOL_GROUP, POOL_GROUP), lambda b: (0, 0, 0)),
            pl.BlockSpec((1, c), lambda b: (0, 0)),
        ],
        out_specs=pl.BlockSpec((bn, s_len, c), lambda b: (b, 0, 0)),
        compiler_params=_cparams(("parallel",)),
        name="pool_sample",
    )(x_ext, pool_w, pool_scale)


def _compress1_kernel(x_ref, w_ref, pa_ref, pb_ref):
    acc = None
    for j in range(CMP_STRIDE):
        xj = x_ref[:, j * KV_COLS:(j + 1) * KV_COLS].astype(BF16)
        d = _dot(xj, w_ref[j])
        acc = d if acc is None else acc + d
    half = 4 * CMP_HIDDEN
    pa_ref[...] = acc[:, :half]
    pb_ref[...] = acc[:, half:]


def _compress1(x, w_bd, ts):
    r, c = x.shape
    ts = min(ts, r)
    half = 4 * CMP_HIDDEN
    return pl.pallas_call(
        _compress1_kernel,
        out_shape=[jax.ShapeDtypeStruct((r, half), F32), jax.ShapeDtypeStruct((r, half), F32)],
        grid=(r // ts,),
        in_specs=[
            pl.BlockSpec((ts, c), lambda i: (i, 0)),
            pl.BlockSpec(w_bd.shape, lambda i: (0, 0, 0)),
        ],
        out_specs=[pl.BlockSpec((ts, half), lambda i: (i, 0)), pl.BlockSpec((ts, half), lambda i: (i, 0))],
        compiler_params=_cparams(("parallel",)),
        name="compress1",
    )(x, w_bd)


def _gelu_tanh(x):
    return 0.5 * x * (1.0 + jnp.tanh(0.7978845608028654 * (x + 0.044715 * (x * x * x))))


def _compress2_kernel(pa_ref, pb_ref, pbn_ref, pek_ref, pev_ref, w1k_ref, w1v_ref, w2_ref, kg_ref, o_ref):
    pa = pa_ref[0]
    pb = pb_ref[0]
    s = pa.shape[0]
    row = lax.broadcasted_iota(jnp.int32, (s, 1), 0)
    pb_next = jnp.where(row == s - 1, pbn_ref[0][0:1, :], pltpu.roll(pb, s - 1, 0))
    hk = _dot(pek_ref[...], w1k_ref[...])[0:1, :]
    hv = _dot(pev_ref[...], w1v_ref[...])[0:1, :]
    pe_h = jnp.concatenate([hk, hv, hk, hv], axis=1)
    a = _gelu_tanh(pa + pb_next + pe_h)
    o = _dot(a.astype(BF16), w2_ref[...])
    lane = lax.broadcasted_iota(jnp.int32, (1, LANES), 1)
    is_key = lane < HEAD_DIM
    for gg in range(KV_GROUPS):
        sl = slice(gg * LANES, (gg + 1) * LANES)
        ob = o[:, sl]
        msq = jnp.sum(jnp.where(is_key, ob * ob, 0.0), axis=-1, keepdims=True) * (1.0 / HEAD_DIM)
        y = jnp.where(is_key, ob * lax.rsqrt(msq + RMS_EPS) * kg_ref[...], ob)
        o_ref[0, :, sl] = y.astype(o_ref.dtype)


def _compress2(pa, pb, pb_new, pek, pev, w1k, w1v, w2_bd, kg):
    n, s, c = pa.shape
    full2 = lambda b: (0, 0)
    return pl.pallas_call(
        _compress2_kernel,
        out_shape=jax.ShapeDtypeStruct((n, s, KV_COLS), BF16),
        grid=(n,),
        in_specs=[
            pl.BlockSpec((1, s, c), lambda b: (b, 0, 0)),
            pl.BlockSpec((1, s, c), lambda b: (b, 0, 0)),
            pl.BlockSpec((1, 8, c), lambda b: (b, 0, 0)),
            pl.BlockSpec(pek.shape, full2),
            pl.BlockSpec(pev.shape, full2),
            pl.BlockSpec(w1k.shape, full2),
            pl.BlockSpec(w1v.shape, full2),
            pl.BlockSpec(w2_bd.shape, full2),
            pl.BlockSpec(kg.shape, full2),
        ],
        out_specs=pl.BlockSpec((1, s, KV_COLS), lambda b: (b, 0, 0)),
        compiler_params=_cparams(("parallel",)),
        name="compress2",
    )(pa, pb, pb_new, pek, pev, w1k, w1v, w2_bd, kg)


def _stack_heads(q, hg):
    return jnp.concatenate([q[:, h * LANES:(h + 1) * LANES] for h in range(hg)], axis=0).astype(BF16)


def _row_qpos(hg, tq, q0):
    r = lax.broadcasted_iota(jnp.int32, (hg * tq, 1), 0)
    return q0 + (r & (tq - 1))


def _store_heads(o_ref, lead, o, hg, tq):
    lane = lax.broadcasted_iota(jnp.int32, (1, LANES), 1)
    for h in range(hg):
        blk = jnp.where(lane >= HEAD_DIM, o[h * tq:(h + 1) * tq], 0.0)
        o_ref[lead + (slice(None), slice(h * LANES, (h + 1) * LANES))] = blk.astype(o_ref.dtype)


def _cmp_attn_kernel(q_ref, kc_ref, cov_ref, o_ref, sel_ref, *, bn, hg, tq, q_off, n_blocks, row_off, as_bias):
    i = pl.program_id(2)
    s_len = kc_ref.shape[1]
    q0 = q_off + i * tq
    qpos = _row_qpos(hg, tq, q0)
    blk_end = lax.broadcasted_iota(jnp.int32, (1, s_len), 1) * CMP_STRIDE + (CMP_BLOCK - 1)
    cond = blk_end <= qpos
    psums = []
    for b in range(bn):
        qs = _stack_heads(q_ref[b], hg)
        kc = kc_ref[b]
        s = jnp.where(cond, _dot_nt(qs, kc), NEG_INF)
        m = jnp.max(s, axis=-1, keepdims=True)
        p = jnp.where(cond, jnp.exp(s - m), 0.0)
        den = jnp.sum(p, axis=-1, keepdims=True)
        p = p / jnp.maximum(den, 1e-30)
        o = _dot(p.astype(BF16), kc)
        _store_heads(o_ref, (b,), o, hg, tq)
        ps = p[0:tq]
        for h in range(1, hg):
            ps = ps + p[h * tq:(h + 1) * tq]
        psums.append(ps)
    ps = psums[0] if bn == 1 else jnp.concatenate(psums, axis=0)
    hi, lo = _split_bf16(ps)
    cov = cov_ref[...]
    score = _dot_nt(cov, hi) + _dot_nt(cov, lo)
    nb_pad, nl = score.shape
    j = lax.broadcasted_iota(jnp.int32, (nb_pad, nl), 0) - row_off
    t = q0 + (lax.broadcasted_iota(jnp.int32, (1, nl), 1) & (tq - 1))
    cur = t >> SLC_SHIFT
    forced = (j == 0) | (j == cur) | (j == cur - 1)
    valid = j * SLC_BLOCK <= t
    score = jnp.where(valid, jnp.where(forced, FORCE_SCORE, score), NEG_INF)
    is_block = lax.bitcast_convert_type(j, jnp.uint32) < jnp.uint32(n_blocks)
    score = jnp.where(is_block, score, PAD_SCORE)
    sel = jnp.zeros((nb_pad, nl), F32)
    for _ in range(min(SLC_TOPN, n_blocks)):
        mx = jnp.max(score, axis=0, keepdims=True)
        jm = jnp.min(jnp.where(score == mx, j, nb_pad), axis=0, keepdims=True)
        pick = j == jm
        sel = jnp.where(pick, 1.0, sel)
        score = jnp.where(pick, -jnp.inf, score)
    if as_bias:
        sel = jnp.where(is_block, (sel - 1.0) * (-NEG_INF), 0.0)
    sel_t = sel.T
    for b in range(bn):
        sel_ref[b, 0] = sel_t[b * tq:(b + 1) * tq]


def _cmp_attn(q, kc, cov_t, *, hg, tq, bn, q_off, n_blocks, out_dtype, row_off, as_bias):
    n, t, hc = q.shape
    s_len = kc.shape[1]
    nb_pad = cov_t.shape[0]
    g = KV_GROUPS
    return pl.pallas_call(
        functools.partial(_cmp_attn_kernel, bn=bn, hg=hg, tq=tq, q_off=q_off, n_blocks=n_blocks,
                          row_off=row_off, as_bias=as_bias),
        out_shape=[jax.ShapeDtypeStruct((n, t, hc), out_dtype), jax.ShapeDtypeStruct((n, g, t, nb_pad), F32)],
        grid=(n // bn, g, t // tq),
        in_specs=[
            pl.BlockSpec((bn, tq, hg * LANES), lambda b, gg, i: (b, i, gg)),
            pl.BlockSpec((bn, s_len, LANES), lambda b, gg, i: (b, 0, gg)),
            pl.BlockSpec((nb_pad, s_len), lambda b, gg, i: (0, 0)),
        ],
        out_specs=[
            pl.BlockSpec((bn, tq, hg * LANES), lambda b, gg, i: (b, i, gg)),
            pl.BlockSpec((bn, 1, tq, nb_pad), lambda b, gg, i: (b, gg, i, 0)),
        ],
        compiler_params=_cparams(("parallel", "parallel", "parallel")),
        name="cmp_attn",
    )(q, kc, cov_t)


def _flash_kernel(*refs, hg, tq, tk, window, use_sel, use_sink, chunk):
    refs = list(refs)
    sink_ref = refs.pop(0) if use_sink else None
    q_ref = refs.pop(0)
    kv_ref = refs.pop(0)
    selb_ref = refs.pop(0) if use_sel else None
    o_ref, qs_sc, m_sc, acc_sc = refs
    g = pl.program_id(1)
    i = pl.program_id(2)
    rows = hg * tq
    q0 = i * tq
    for h in range(hg):
        qh = q_ref[0, :, h * LANES:(h + 1) * LANES]
        if use_sel:
            qh = qh + selb_ref[0, 0]
        qs_sc[h * tq:(h + 1) * tq, :] = qh.astype(BF16)
    m_sc[...] = jnp.full((rows, LANES), NEG_INF, F32)
    acc_sc[...] = jnp.zeros((rows, LANES), F32)

    def tile(k0, masked):
        kv = kv_ref[0, pl.ds(k0, tk), :]
        klane = lax.broadcasted_iota(jnp.int32, (tk, LANES), 1)
        if use_sel:
            krow = lax.broadcasted_iota(jnp.int32, (tk, LANES), 0)
            onehot = jnp.where(klane - HEAD_DIM == ((k0 + krow) >> SLC_SHIFT), 1.0, 0.0).astype(BF16)
            kaug = jnp.where(klane < HEAD_DIM, kv, onehot)
        else:
            kaug = kv
        vaug = jnp.where(klane < HEAD_DIM, jnp.ones_like(kv), kv)
        for c in range(rows // chunk):
            r0 = c * chunk
            s = _dot_nt(qs_sc[r0:r0 + chunk, :], kaug)
            if masked:
                qpos = q0 + ((r0 + lax.broadcasted_iota(jnp.int32, (chunk, 1), 0)) & (tq - 1))
                d = qpos - (k0 + lax.broadcasted_iota(jnp.int32, (1, tk), 1))
                if window is None:
                    s = jnp.where(d >= 0, s, NEG_INF)
                else:
                    s = jnp.where(lax.bitcast_convert_type(d, jnp.uint32) <= jnp.uint32(window), s, NEG_INF)
            m_prev = m_sc[r0:r0 + chunk, :]
            m_new = jnp.maximum(m_prev, jnp.max(s, axis=-1, keepdims=True))
            alpha = jnp.exp(m_prev - m_new)
            p = jnp.exp(s - jnp.concatenate([m_new] * (tk // LANES), axis=1))
            acc_sc[r0:r0 + chunk, :] = alpha * acc_sc[r0:r0 + chunk, :] + _dot(p.astype(BF16), vaug)
            m_sc[r0:r0 + chunk, :] = m_new

    if window is None:
        def body(jt, c):
            tile(pl.multiple_of(jt * tk, tk), False)
            return c

        lax.fori_loop(0, i, body, 0)
        tile(pl.multiple_of(i * tk, tk), True)
    else:
        tile(pl.multiple_of(jnp.maximum(q0 - window, 0), LANES), True)

    acc = acc_sc[...]
    den = pltpu.roll(acc, HEAD_DIM, 1)
    m = m_sc[...]
    if use_sink:
        sink = jnp.concatenate([jnp.full((tq, LANES), sink_ref[g * hg + h], F32) for h in range(hg)], axis=0)
        m_fin = jnp.maximum(m, sink)
        w = jnp.exp(m - m_fin)
        o = acc * w / jnp.maximum(den * w + jnp.exp(sink - m_fin), 1e-30)
    else:
        o = jnp.where(m > 0.5 * NEG_INF, acc / jnp.maximum(den, 1e-30), 0.0)
    lane = lax.broadcasted_iota(jnp.int32, (1, LANES), 1)
    o = jnp.where(lane >= HEAD_DIM, o, 0.0)
    for h in range(hg):
        o_ref[0, :, h * LANES:(h + 1) * LANES] = o[h * tq:(h + 1) * tq].astype(o_ref.dtype)


def _flash(q, kv, *, hg, tq, window, chunk, selb=None, sinks=None):
    n, t, hc = q.shape
    g = KV_GROUPS
    use_sel = selb is not None
    use_sink = sinks is not None
    rows = hg * tq
    tk = tq if window is None else tq + window
    if use_sel:
        assert t // SLC_BLOCK <= LANES - HEAD_DIM, "selection bias needs one upper query lane per block"
    in_specs = []
    args = []
    if use_sink:
        in_specs.append(pl.BlockSpec(memory_space=pltpu.SMEM))
        args.append(sinks)
    in_specs += [
        pl.BlockSpec((1, tq, hg * LANES), lambda b, gg, i: (b, i, gg)),
        pl.BlockSpec((1, t, LANES), lambda b, gg, i: (b, 0, gg)),
    ]
    args += [q, kv]
    if use_sel:
        in_specs.append(pl.BlockSpec((1, 1, tq, LANES), lambda b, gg, i: (b, gg, i, 0)))
        args.append(selb)
    return pl.pallas_call(
        functools.partial(_flash_kernel, hg=hg, tq=tq, tk=tk, window=window, use_sel=use_sel,
                          use_sink=use_sink, chunk=chunk),
        out_shape=jax.ShapeDtypeStruct((n, t, hc), BF16),
        grid=(n, g, t // tq),
        in_specs=in_specs,
        out_specs=pl.BlockSpec((1, tq, hg * LANES), lambda b, gg, i: (b, i, gg)),
        scratch_shapes=[pltpu.VMEM((rows, LANES), BF16), pltpu.VMEM((rows, LANES), F32),
                        pltpu.VMEM((rows, LANES), F32)],
        compiler_params=_cparams(("parallel", "parallel", "parallel")),
        name="flash",
    )(*args)


def _attn_kernel(*refs, hg, tq, window, q_off, k_off, use_sel, use_sink):
    refs = list(refs)
    sink_ref = refs.pop(0) if use_sink else None
    q_ref = refs.pop(0)
    kv_ref = refs.pop(0)
    sel_ref = refs.pop(0) if use_sel else None
    e_ref = refs.pop(0) if use_sel else None
    (o_ref,) = refs
    g = pl.program_id(1)
    qs = _stack_heads(q_ref[0], hg)
    qpos = _row_qpos(hg, tq, q_off)
    kv = kv_ref[0]
    tk = kv.shape[0]
    s = _dot_nt(qs, kv)
    d = qpos - (k_off + lax.broadcasted_iota(jnp.int32, (1, tk), 1))
    if window is None:
        s = jnp.where(d >= 0, s, NEG_INF)
    else:
        s = jnp.where(lax.bitcast_convert_type(d, jnp.uint32) <= jnp.uint32(window), s, NEG_INF)
    if use_sel:
        sel = jnp.concatenate([sel_ref[0, 0]] * hg, axis=0).astype(BF16)
        s = jnp.where(_dot(sel, e_ref[...]) > 0.5, s, NEG_INF)
    m = jnp.max(s, axis=-1, keepdims=True)
    if use_sink:
        sink = jnp.concatenate([jnp.full((tq, 1), sink_ref[g * hg + h], F32) for h in range(hg)], axis=0)
        m = jnp.maximum(m, sink)
    p = jnp.exp(s - m)
    den = jnp.sum(p, axis=-1, keepdims=True)
    if use_sink:
        den = den + jnp.exp(sink - m)
    o = jnp.where(m > 0.5 * NEG_INF, _dot(p.astype(BF16), kv) / jnp.maximum(den, 1e-30), 0.0)
    _store_heads(o_ref, (0,), o, hg, tq)


def _attn(q, kv, *, hg, window, q_off, k_off, sel=None, expand=None, sinks=None):
    n, tq, hc = q.shape
    tkv = kv.shape[1]
    g = KV_GROUPS
    use_sel = sel is not None
    use_sink = sinks is not None
    in_specs = []
    args = []
    if use_sink:
        in_specs.append(pl.BlockSpec(memory_space=pltpu.SMEM))
        args.append(sinks)
    in_specs += [
        pl.BlockSpec((1, tq, hg * LANES), lambda b, gg: (b, 0, gg)),
        pl.BlockSpec((1, tkv, LANES), lambda b, gg: (b, 0, gg)),
    ]
    args += [q, kv]
    if use_sel:
        in_specs.append(pl.BlockSpec((1, 1, tq, sel.shape[-1]), lambda b, gg: (b, gg, 0, 0)))
        in_specs.append(pl.BlockSpec(expand.shape, lambda b, gg: (0, 0)))
        args += [sel, expand]
    return pl.pallas_call(
        functools.partial(_attn_kernel, hg=hg, tq=tq, window=window, q_off=q_off, k_off=k_off,
                          use_sel=use_sel, use_sink=use_sink),
        out_shape=jax.ShapeDtypeStruct((n, tq, hc), F32),
        grid=(n, g),
        in_specs=in_specs,
        out_specs=pl.BlockSpec((1, tq, hg * LANES), lambda b, gg: (b, 0, gg)),
        compiler_params=_cparams(("parallel", "parallel")),
        name="attn",
    )(*args)


def _out0_kernel(x_ref, pool_ref, oc_ref, os_ref, ow_ref, gt_ref, eg_ref, wa_ref, wb_ref, o_ref):
    ghi, glo = _split_bf16(gt_ref[...])
    o = None
    for b, br in enumerate((oc_ref, os_ref, ow_ref)):
        gate = _dot(ghi, eg_ref[b]) + _dot(glo, eg_ref[b])
        term = gate * br[...].astype(F32)
        o = term if o is None else o + term
    mix = _dot(pool_ref[...], wa_ref[...]) + _dot(o.astype(BF16), wb_ref[...])
    o_ref[...] = x_ref[...] + mix


def _out0(x, pool_o, o_cmp, o_slc, o_win, gates, eg, wa, wb, tm=512):
    m, d = x.shape
    hc = o_cmp.shape[1]
    row = lambda w: pl.BlockSpec((tm, w), lambda i: (i, 0))
    return pl.pallas_call(
        _out0_kernel,
        out_shape=jax.ShapeDtypeStruct((m, d), F32),
        grid=(m // tm,),
        in_specs=[row(d), row(POOL_CH), row(hc), row(hc), row(hc), row(LANES),
                  pl.BlockSpec(eg.shape, lambda i: (0, 0, 0)),
                  pl.BlockSpec(wa.shape, lambda i: (0, 0)),
                  pl.BlockSpec(wb.shape, lambda i: (0, 0))],
        out_specs=row(d),
        compiler_params=_cparams(("parallel",)),
        name="out0",
    )(x, pool_o, o_cmp, o_slc, o_win, gates, eg, wa, wb)


def _ffn_kernel(x_ref, g_ref, wg_ref, wu_ref, wd_ref, o_ref, h_sc, acc_sc):
    f = pl.program_id(1)

    @pl.when(f == 0)
    def _():
        x = x_ref[...]
        ms = jnp.mean(x * x, axis=-1, keepdims=True)
        h_sc[...] = (x * lax.rsqrt(ms + RMS_EPS) * g_ref[...]).astype(BF16)
        acc_sc[...] = x

    h = h_sc[...]
    a = _dot(h, wg_ref[...])
    u = _dot(h, wu_ref[...])
    act = (a * jax.nn.sigmoid(a) * u).astype(BF16)
    acc_sc[...] += _dot(act, wd_ref[...])

    @pl.when(f == pl.num_programs(1) - 1)
    def _():
        o_ref[...] = acc_sc[...]


def _ffn(x, g, wg, wu, wd, tm=512, nf=2):
    m, d = x.shape
    fdim = wg.shape[1]
    tf = fdim // nf
    return pl.pallas_call(
        _ffn_kernel,
        out_shape=jax.ShapeDtypeStruct((m, d), F32),
        grid=(m // tm, nf),
        in_specs=[
            pl.BlockSpec((tm, d), lambda i, f: (i, 0)),
            pl.BlockSpec((1, d), lambda i, f: (0, 0)),
            pl.BlockSpec((d, tf), lambda i, f: (0, f)),
            pl.BlockSpec((d, tf), lambda i, f: (0, f)),
            pl.BlockSpec((tf, d), lambda i, f: (f, 0)),
        ],
        out_specs=pl.BlockSpec((tm, d), lambda i, f: (i, 0)),
        scratch_shapes=[pltpu.VMEM((tm, d), BF16), pltpu.VMEM((tm, d), F32)],
        compiler_params=_cparams(("parallel", "arbitrary")),
        name="ffn",
    )(x, g, wg, wu, wd)


def _out1_kernel(x_ref, o_ref_in, w_ref, g_ref, rwh_ref, rwl_ref, rb_ref, xo_ref, h_ref, r_ref):
    x = x_ref[...] + _dot(o_ref_in[...], w_ref[...])
    xo_ref[...] = x
    ms = jnp.mean(x * x, axis=-1, keepdims=True)
    h = x * lax.rsqrt(ms + RMS_EPS) * g_ref[...]
    h_ref[...] = h.astype(BF16)
    hi, lo = _split_bf16(h)
    logits = _dot(hi, rwh_ref[...]) + _dot(lo, rwh_ref[...]) + _dot(hi, rwl_ref[...]) + rb_ref[...]
    lane = lax.broadcasted_iota(jnp.int32, logits.shape, 1)
    logits = jnp.where(lane < N_EXPERTS, logits, -jnp.inf)
    m1 = jnp.max(logits, axis=-1, keepdims=True)
    i1 = jnp.min(jnp.where(logits == m1, lane, LANES), axis=-1, keepdims=True)
    rest = jnp.where(lane == i1, -jnp.inf, logits)
    m2 = jnp.max(rest, axis=-1, keepdims=True)
    i2 = jnp.min(jnp.where(rest == m2, lane, LANES), axis=-1, keepdims=True)
    e2 = jnp.exp(m2 - m1)
    g1 = 1.0 / (1.0 + e2)
    g2 = e2 / (1.0 + e2)
    r = jnp.where(lane == 0, i1.astype(F32), 0.0)
    r = jnp.where(lane == 1, i2.astype(F32), r)
    r = jnp.where(lane == 2, g1, r)
    r = jnp.where(lane == 3, g2, r)
    r_ref[...] = r


def _out1(x, o, w, g, rwh, rwl, rb, tm=512):
    m, d = x.shape
    hc = o.shape[1]
    row = lambda wd: pl.BlockSpec((tm, wd), lambda i: (i, 0))
    full = lambda a: pl.BlockSpec(a.shape, lambda i: (0, 0))
    return pl.pallas_call(
        _out1_kernel,
        out_shape=[jax.ShapeDtypeStruct((m, d), F32), jax.ShapeDtypeStruct((m, d), BF16),
                   jax.ShapeDtypeStruct((m, LANES), F32)],
        grid=(m // tm,),
        in_specs=[row(d), row(hc), full(w), full(g), full(rwh), full(rwl), full(rb)],
        out_specs=[row(d), row(d), row(LANES)],
        compiler_params=_cparams(("parallel",)),
        name="out1",
    )(x, o, w, g, rwh, rwl, rb)


def _moe_kernel(be_ref, nb_ref, x_ref, wg_ref, wu_ref, wd_ref, o_ref, acc_sc):
    b = pl.program_id(0)
    f = pl.program_id(1)
    used = b < nb_ref[0]

    @pl.when(f == 0)
    def _():
        acc_sc[...] = jnp.zeros_like(acc_sc)

    @pl.when(used)
    def _():
        x = x_ref[...]
        a = _dot(x, wg_ref[0])
        u = _dot(x, wu_ref[0])
        act = (a * jax.nn.sigmoid(a) * u).astype(BF16)
        acc_sc[...] += _dot(act, wd_ref[0])

    @pl.when(f == pl.num_programs(1) - 1)
    def _():
        o_ref[...] = acc_sc[...]


def _moe(xs, blk_e, n_used, wg, wu, wd, tm, tf):
    cap, d = xs.shape
    fdim = wg.shape[2]
    grid_spec = pltpu.PrefetchScalarGridSpec(
        num_scalar_prefetch=2,
        grid=(cap // tm, fdim // tf),
        in_specs=[
            pl.BlockSpec((tm, d), lambda b, f, be, nb: (b, 0)),
            pl.BlockSpec((1, d, tf), lambda b, f, be, nb: (be[b], 0, f)),
            pl.BlockSpec((1, d, tf), lambda b, f, be, nb: (be[b], 0, f)),
            pl.BlockSpec((1, tf, d), lambda b, f, be, nb: (be[b], f, 0)),
        ],
        out_specs=pl.BlockSpec((tm, d), lambda b, f, be, nb: (b, 0)),
        scratch_shapes=[pltpu.VMEM((tm, d), F32)],
    )
    return pl.pallas_call(
        _moe_kernel,
        out_shape=jax.ShapeDtypeStruct((cap, d), F32),
        grid_spec=grid_spec,
        compiler_params=_cparams(("parallel", "arbitrary")),
        name="moe",
    )(blk_e, n_used, xs, wg, wu, wd)


def _pad_heads_cols(w, heads):
    d = w.shape[0]
    w = w.reshape(d, heads, HEAD_DIM)
    return jnp.pad(w, ((0, 0), (0, 0), (0, LANES - HEAD_DIM))).reshape(d, heads * LANES)


def _pad_heads_rows(w, heads):
    d = w.shape[1]
    w = w.reshape(heads, HEAD_DIM, d)
    return jnp.pad(w, ((0, 0), (LANES - HEAD_DIM, 0), (0, 0))).reshape(heads * LANES, d)


def _head_gain(gain, heads, scale):
    g = jnp.pad(gain.astype(F32) * scale, (0, LANES - HEAD_DIM))
    return jnp.tile(g, heads)


def _key_gain(gain):
    return jnp.concatenate([gain.astype(F32), jnp.ones((HEAD_DIM,), F32)])


def _cover_t(n_cmp_pad, n_cmp, n_blocks, nb_pad, row_off):
    m = jnp.arange(n_cmp_pad)
    c_start = m * CMP_STRIDE
    c_end = c_start + CMP_BLOCK - 1
    jb = jnp.arange(nb_pad) - row_off
    b_start = jb * SLC_BLOCK
    b_end = b_start + SLC_BLOCK - 1
    cov = (c_start[None, :] <= b_end[:, None]) & (c_end[None, :] >= b_start[:, None])
    cov = cov & (m[None, :] < n_cmp) & (jb[:, None] >= 0) & (jb[:, None] < n_blocks)
    return cov.astype(BF16)


def _route(r, n_tok, tm):
    top_e = r[:, 0:2].astype(jnp.int32)
    gate = r[:, 2:4]
    n_asg = n_tok * 2
    flat_e = top_e.reshape(n_asg)
    order = jnp.argsort(flat_e)
    sorted_e = flat_e[order]
    counts = jnp.sum(flat_e[:, None] == jnp.arange(N_EXPERTS)[None, :], axis=0).astype(jnp.int32)
    padded = (counts + tm - 1) // tm * tm
    start = jnp.cumsum(counts) - counts
    pend = jnp.cumsum(padded)
    pstart = pend - padded
    dest_sorted = pstart[sorted_e] + jnp.arange(n_asg, dtype=jnp.int32) - start[sorted_e]
    dest = jnp.zeros((n_asg,), jnp.int32).at[order].set(dest_sorted.astype(jnp.int32))
    n_blk = n_asg // tm + N_EXPERTS
    cap = n_blk * tm
    row_tok = jnp.full((cap,), n_tok, jnp.int32).at[dest].set(jnp.arange(n_asg, dtype=jnp.int32) // 2)
    blk_e = jnp.minimum(jnp.sum(jnp.arange(n_blk)[:, None] * tm >= pend[None, :], axis=1), N_EXPERTS - 1)
    n_used = (pend[-1] // tm).astype(jnp.int32).reshape(1)
    return gate, dest, row_tok, blk_e.astype(jnp.int32), n_used


def kernel(x_prompt, x_sample, state_pool, cache_nsa_cmp, cache_nsa_slc, cache_nsa_win, cache_swa, page_table, norm0_mix, w_in0, pool_w, pool_scale, nsa_q_gain, nsa_k_gain, cmp_k_w1, cmp_k_w2, cmp_k_pe, cmp_v_w1, cmp_v_w2, cmp_v_pe, w_out0, norm0_ffn, ffn_w_gate, ffn_w_up, ffn_w_down, norm1_mix, w_in1, swa_q_gain, swa_k_gain, swa_sinks, w_out1, norm1_ffn, router_w, router_b, moe_w_gate, moe_w_up, moe_w_down):
    nb, t, d = x_prompt.shape
    ns, s_len, _ = x_sample.shape
    mp = nb * t
    msz = ns * s_len
    past = page_table.shape[1] * PAGE_SIZE
    n_phys = cache_nsa_cmp.shape[0]
    total = past + s_len
    g = KV_GROUPS

    x_all = jnp.concatenate([x_prompt.reshape(mp, d), x_sample.reshape(msz, d)], axis=0)

    c_q = POOL_CH
    c_kv = c_q + NSA_HEADS * HEAD_DIM
    c_gate = c_kv + 3 * KV_COLS
    n_gate = 3 * NSA_HEADS
    w0 = jnp.concatenate([
        w_in0[:, :c_q],
        _pad_heads_cols(w_in0[:, c_q:c_kv], NSA_HEADS),
        w_in0[:, c_kv:c_gate],
        jnp.pad(w_in0[:, c_gate:], ((0, 0), (0, LANES - n_gate))),
    ], axis=1).astype(BF16)
    qw = NSA_HEADS * LANES
    aux0 = jnp.concatenate([
        jnp.ones((POOL_CH,), F32),
        _head_gain(nsa_q_gain, NSA_HEADS, ATTN_SCALE),
        jnp.ones((KV_COLS,), F32),
        jnp.tile(_key_gain(nsa_k_gain[1]), g),
        jnp.tile(_key_gain(nsa_k_gain[2]), g),
        jnp.ones((LANES,), F32),
    ]).reshape(1, -1)
    o0 = POOL_CH
    o1 = o0 + qw
    o2 = o1 + KV_COLS
    o3 = o2 + KV_COLS
    o4 = o3 + KV_COLS
    segs0 = (("raw", 0, POOL_CH), ("qnorm", o0, qw), ("raw", o1, KV_COLS), ("kvnorm", o2, KV_COLS),
             ("kvnorm", o3, KV_COLS), ("sigmoid", o4, LANES))
    outs0 = ((POOL_CH, F32), (qw, F32), (KV_COLS, F32), (KV_COLS, F32), (KV_COLS, BF16),
             (KV_COLS, F32), (KV_COLS, BF16), (LANES, F32))
    u_all, q_all, cmp_all, slc_all, slc_bf, win_all, win_bf, gates_all = _proj(
        x_all, norm0_mix.reshape(1, d), w0, aux0, segs0, outs0, 512, "proj0")

    pool_w_bf = pool_w.astype(BF16)
    pool_scale2 = pool_scale.reshape(1, POOL_CH).astype(F32)

    u_p = u_all[:mp].reshape(nb, t, POOL_CH)
    u_s = u_all[mp:].reshape(ns, s_len, POOL_CH)
    pool_o_p = _pool_prompt(u_p, pool_w_bf, pool_scale2)
    u_ext = jnp.concatenate([state_pool.astype(F32), u_s], axis=1)
    x_ext = jnp.pad(u_ext, ((0, 0), (1, 0), (0, 0)))
    pool_o_s = _pool_sample(x_ext, pool_w_bf, pool_scale2, s_len)
    pool_o = jnp.concatenate([pool_o_p.reshape(mp, POOL_CH), pool_o_s.reshape(msz, POOL_CH).astype(BF16)], axis=0)

    def w1_parts(w1):
        return w1.reshape(2, CMP_STRIDE, HEAD_DIM, CMP_HIDDEN)

    w_sel = jnp.stack([w1_parts(cmp_k_w1), w1_parts(cmp_v_w1)] * g, axis=0)
    w1_bd = jnp.einsum("cpjdh,ce->jcdpeh", w_sel, jnp.eye(2 * g, dtype=F32))
    w1_bd = w1_bd.reshape(CMP_STRIDE, KV_COLS, 2 * 2 * g * CMP_HIDDEN).astype(BF16)
    w2_sel = jnp.stack([cmp_k_w2, cmp_v_w2] * g, axis=0)
    w2_bd = jnp.einsum("chd,ce->ched", w2_sel, jnp.eye(2 * g, dtype=F32))
    w2_bd = w2_bd.reshape(2 * g * CMP_HIDDEN, KV_COLS).astype(BF16)
    pek = jnp.pad(cmp_k_pe.reshape(1, -1), ((0, 7), (0, 0))).astype(BF16)
    pev = jnp.pad(cmp_v_pe.reshape(1, -1), ((0, 7), (0, 0))).astype(BF16)
    w1k = cmp_k_w1.astype(BF16)
    w1v = cmp_v_w1.astype(BF16)
    kg0 = _key_gain(nsa_k_gain[0]).reshape(1, LANES)
    seg_cols = CMP_STRIDE * KV_COLS
    half = 2 * g * CMP_HIDDEN

    n_seg_p = t // CMP_STRIDE
    pa_p, pb_p = _compress1(cmp_all[:mp].reshape(nb * n_seg_p, seg_cols), w1_bd, 256)
    kcv_p = _compress2(pa_p.reshape(nb, n_seg_p, half), pb_p.reshape(nb, n_seg_p, half),
                       jnp.zeros((nb, 8, half), F32), pek, pev, w1k, w1v, w2_bd, kg0)
    n_seg_s = past // CMP_STRIDE
    past_cmp = cache_nsa_cmp.reshape(n_phys, PAGE_SIZE * KV_COLS)[page_table]
    pa_s, pb_s = _compress1(past_cmp.reshape(ns * n_seg_s, seg_cols), w1_bd, 256)
    pad_new = -(-total // CMP_STRIDE) * CMP_STRIDE - past
    new_seg = jnp.pad(cmp_all[mp:].reshape(ns, s_len * KV_COLS), ((0, 0), (0, (pad_new - s_len) * KV_COLS)))
    _, pb_new = _compress1(new_seg, w1_bd, ns)
    pb_new = jnp.pad(pb_new.reshape(ns, 1, half), ((0, 0), (0, 7), (0, 0)))
    kcv_s = _compress2(pa_s.reshape(ns, n_seg_s, half), pb_s.reshape(ns, n_seg_s, half),
                       pb_new, pek, pev, w1k, w1v, w2_bd, kg0)

    hg0 = NSA_HEADS // g
    q_p = q_all[:mp].reshape(nb, t, qw)
    q_s = q_all[mp:].reshape(ns, s_len, qw)
    nblk_p = t // SLC_BLOCK
    cov_p = _cover_t(n_seg_p, n_seg_p - 1, nblk_p, LANES, HEAD_DIM)
    o_cmp_p, selb_p = _cmp_attn(q_p, kcv_p, cov_p, hg=hg0, tq=256, bn=1, q_off=0, n_blocks=nblk_p,
                                out_dtype=BF16, row_off=HEAD_DIM, as_bias=True)
    nblk_s = -(-total // SLC_BLOCK)
    nbp_s = -(-nblk_s // LANES) * LANES
    n_cmp_s = -(-total // CMP_STRIDE) - 1
    cov_s = _cover_t(n_seg_s, n_cmp_s, nblk_s, nbp_s, 0)
    o_cmp_s, sel_s = _cmp_attn(q_s, kcv_s, cov_s, hg=hg0, tq=s_len, bn=LANES // s_len, q_off=past,
                               n_blocks=nblk_s, out_dtype=F32, row_off=0, as_bias=False)

    slc_p = slc_bf[:mp].reshape(nb, t, KV_COLS)
    o_slc_p = _flash(q_p, slc_p, hg=hg0, tq=512, window=None, chunk=256, selb=selb_p)
    tk_s = -(-total // 512) * 512
    past_slc = cache_nsa_slc.reshape(n_phys, PAGE_SIZE, KV_COLS)[page_table].astype(BF16)
    kv_slc_s = jnp.concatenate([
        past_slc.reshape(ns, past, KV_COLS), slc_bf[mp:].reshape(ns, s_len, KV_COLS),
        jnp.zeros((ns, tk_s - total, KV_COLS), BF16)], axis=1)
    expand_s = (jnp.arange(tk_s)[None, :] // SLC_BLOCK == jnp.arange(nbp_s)[:, None]).astype(BF16)
    o_slc_s = _attn(q_s, kv_slc_s, hg=hg0, window=None, q_off=past, k_off=0, sel=sel_s, expand=expand_s)

    win_p_rows = win_bf[:mp].reshape(nb, t, KV_COLS)
    o_win_p = _flash(q_p, win_p_rows, hg=hg0, tq=256, window=NSA_WINDOW, chunk=256)
    wb_len = cache_nsa_win.shape[1]
    tk_w = -(-(wb_len + s_len) // 16) * 16
    kv_win_s = jnp.concatenate([
        cache_nsa_win.astype(BF16).reshape(ns, wb_len, KV_COLS), win_bf[mp:].reshape(ns, s_len, KV_COLS),
        jnp.zeros((ns, tk_w - wb_len - s_len, KV_COLS), BF16)], axis=1)
    o_win_s = _attn(q_s, kv_win_s, hg=hg0, window=NSA_WINDOW, q_off=past, k_off=past - wb_len)

    cat = lambda a, b: jnp.concatenate([a.reshape(mp, -1), b.reshape(msz, -1).astype(BF16)], axis=0)
    eg = (jnp.arange(LANES)[None, :, None] ==
          (jnp.arange(3)[:, None, None] * NSA_HEADS + jnp.arange(qw)[None, None, :] // LANES)).astype(BF16)
    wa0 = w_out0[:POOL_CH].astype(BF16)
    wb0 = _pad_heads_rows(w_out0[POOL_CH:], NSA_HEADS).astype(BF16)
    x1 = _out0(x_all, pool_o, cat(o_cmp_p, o_cmp_s), cat(o_slc_p, o_slc_s), cat(o_win_p, o_win_s),
               gates_all, eg, wa0, wb0)
    x2 = _ffn(x1, norm0_ffn.reshape(1, d), ffn_w_gate.astype(BF16), ffn_w_up.astype(BF16),
              ffn_w_down.astype(BF16))

    c_q1 = SWA_HEADS * HEAD_DIM
    qw1 = SWA_HEADS * LANES
    w1p = jnp.concatenate([_pad_heads_cols(w_in1[:, :c_q1], SWA_HEADS), w_in1[:, c_q1:]], axis=1).astype(BF16)
    aux1 = jnp.concatenate([_head_gain(swa_q_gain, SWA_HEADS, ATTN_SCALE),
                            jnp.tile(_key_gain(swa_k_gain), g)]).reshape(1, -1)
    segs1 = (("qnorm", 0, qw1), ("kvnorm", qw1, KV_COLS))
    outs1 = ((qw1, F32), (KV_COLS, F32), (KV_COLS, BF16))
    q1_all, swa_all, swa_bf = _proj(x2, norm1_mix.reshape(1, d), w1p, aux1, segs1, outs1, 512, "proj1")
    hg1 = SWA_HEADS // g
    sinks = swa_sinks.astype(F32)
    q1_p = q1_all[:mp].reshape(nb, t, qw1)
    q1_s = q1_all[mp:].reshape(ns, s_len, qw1)
    o1_p = _flash(q1_p, swa_bf[:mp].reshape(nb, t, KV_COLS), hg=hg1, tq=128, window=SWA_WINDOW, chunk=256,
                  sinks=sinks)
    sb_len = cache_swa.shape[1]
    tk_1 = -(-(sb_len + s_len) // 16) * 16
    kv_swa_s = jnp.concatenate([
        cache_swa.astype(BF16).reshape(ns, sb_len, KV_COLS), swa_bf[mp:].reshape(ns, s_len, KV_COLS),
        jnp.zeros((ns, tk_1 - sb_len - s_len, KV_COLS), BF16)], axis=1)
    o1_s = _attn(q1_s, kv_swa_s, hg=hg1, window=SWA_WINDOW, q_off=past, k_off=past - sb_len, sinks=sinks)

    rw = jnp.pad(router_w.astype(F32), ((0, 0), (0, LANES - N_EXPERTS)))
    rwh, rwl = _split_bf16(rw)
    rb = jnp.pad(router_b.astype(F32), (0, LANES - N_EXPERTS)).reshape(1, LANES)
    x3, h3, r = _out1(x2, cat(o1_p, o1_s), _pad_heads_rows(w_out1, SWA_HEADS).astype(BF16),
                      norm1_ffn.reshape(1, d), rwh, rwl, rb)
    m_all = mp + msz
    tm_moe = 512
    gate, dest, row_tok, blk_e, n_used = _route(r, m_all, tm_moe)
    xs = jnp.concatenate([h3, jnp.zeros((1, d), BF16)], axis=0)[row_tok]
    ys = _moe(xs, blk_e, n_used, moe_w_gate.astype(BF16), moe_w_up.astype(BF16), moe_w_down.astype(BF16),
              tm_moe, 512)
    ysel = ys[dest].reshape(m_all, 2, d)
    x4 = x3 + (gate[:, 0:1] * ysel[:, 0] + gate[:, 1:2] * ysel[:, 1])

    y_p = x4[:mp].reshape(nb, t, d)
    y_s = x4[mp:].reshape(ns, s_len, d)
    rows5 = lambda a, n, l: a.reshape(n, l, g, 2, HEAD_DIM)
    keep_w = min(NSA_WINDOW, t)
    keep_s = min(SWA_WINDOW, t)
    pool_p = u_p[:, t - POOL_STATE:]
    cmp_p = rows5(cmp_all[:mp], nb, t)
    slc_p_out = rows5(slc_all[:mp], nb, t)
    win_p_out = rows5(win_all[:mp], nb, t)[:, t - keep_w:]
    swa_p_out = rows5(swa_all[:mp], nb, t)[:, t - keep_s:]
    pool_s = u_ext[:, -POOL_STATE:]
    cmp_s = rows5(cmp_all[mp:], ns, s_len)
    slc_s_out = rows5(slc_all[mp:], ns, s_len)
    win_s_out = jnp.concatenate([cache_nsa_win.astype(F32), rows5(win_all[mp:], ns, s_len)], axis=1)[:, -wb_len:]
    swa_s_out = jnp.concatenate([cache_swa.astype(F32), rows5(swa_all[mp:], ns, s_len)], axis=1)[:, -sb_len:]
    return (y_p, y_s, pool_p, cmp_p, slc_p_out, win_p_out, swa_p_out, pool_s, cmp_s, slc_s_out, win_s_out,
            swa_s_out)
```

```python
import functools

import jax
import jax.numpy as jnp
from jax import lax
from jax.experimental import pallas as pl
from jax.experimental.pallas import tpu as pltpu

F32 = jnp.float32
BF16 = jnp.bfloat16

LANES = 128
MXU_COLS = 256
HEAD_DIM = 64
D_MODEL = 1024
POOL_WINDOWS = (2, 4, 8, 16)
POOL_CH = 512
POOL_GROUP = 128
POOL_STATE = 15
POOL_HALO = 16
NSA_HEADS = 8
SWA_HEADS = 16
KV_GROUPS = 2
KV_COLS = KV_GROUPS * 2 * HEAD_DIM
CMP_BLOCK = 32
CMP_STRIDE = 16
CMP_HIDDEN = 128
SLC_BLOCK = 64
SLC_SHIFT = SLC_BLOCK.bit_length() - 1
SLC_TOPN = 16
NSA_WINDOW = 512
SWA_WINDOW = 128
PAGE_SIZE = 128
N_EXPERTS = 8
RMS_EPS = 1e-6
NEG_INF = -1e30
FORCE_SCORE = 1e9
PAD_SCORE = -3e38
ATTN_SCALE = HEAD_DIM ** -0.5
VMEM_LIMIT = 56 * 1024 * 1024


def _cparams(sem):
    return pltpu.CompilerParams(dimension_semantics=sem, vmem_limit_bytes=VMEM_LIMIT)


def _split_bf16(x):
    hi = x.astype(BF16)
    lo = (x - hi.astype(F32)).astype(BF16)
    return hi, lo


def _dot(a, b):
    return jnp.dot(a, b, preferred_element_type=F32)


def _dot_nt(a, b):
    return lax.dot_general(a, b, (((1,), (1,)), ((), ())), preferred_element_type=F32)


def _proj_kernel(x_ref, g_ref, w_ref, aux_ref, *out_refs, segs):
    x = x_ref[...]
    ms = jnp.mean(x * x, axis=-1, keepdims=True)
    h = (x * lax.rsqrt(ms + RMS_EPS) * g_ref[...]).astype(BF16)
    lane = lax.broadcasted_iota(jnp.int32, (1, LANES), 1)
    is_key = lane < HEAD_DIM
    oi = 0
    for kind, c0, width in segs:
        if kind == "qnorm":
            for pair in range(width // MXU_COLS):
                cp = c0 + pair * MXU_COLS
                z = _dot(h, w_ref[:, cp:cp + MXU_COLS])
                for hh in range(MXU_COLS // LANES):
                    zh = z[:, hh * LANES:(hh + 1) * LANES]
                    msq = jnp.sum(zh * zh, axis=-1, keepdims=True) * (1.0 / HEAD_DIM)
                    lo = cp - c0 + hh * LANES
                    gain = aux_ref[:, cp + hh * LANES:cp + (hh + 1) * LANES]
                    out_refs[oi][:, lo:lo + LANES] = zh * lax.rsqrt(msq + RMS_EPS) * gain
            oi += 1
            continue
        z = _dot(h, w_ref[:, c0:c0 + width])
        aux = aux_ref[:, c0:c0 + width]
        if kind == "raw":
            out_refs[oi][...] = z
            oi += 1
        elif kind == "sigmoid":
            out_refs[oi][...] = jax.nn.sigmoid(z)
            oi += 1
        elif kind == "kvnorm":
            for gg in range(width // LANES):
                sl = slice(gg * LANES, (gg + 1) * LANES)
                zb = z[:, sl]
                msq = jnp.sum(jnp.where(is_key, zb * zb, 0.0), axis=-1, keepdims=True) * (1.0 / HEAD_DIM)
                y = jnp.where(is_key, zb * lax.rsqrt(msq + RMS_EPS) * aux[:, sl], zb)
                out_refs[oi][:, sl] = y
                out_refs[oi + 1][:, sl] = y.astype(BF16)
            oi += 2
        else:
            raise ValueError(kind)


def _proj(x, g, w, aux, segs, out_defs, tm, name):
    m, d = x.shape
    c = w.shape[1]
    out_shape = [jax.ShapeDtypeStruct((m, wd), dt) for wd, dt in out_defs]
    out_specs = [pl.BlockSpec((tm, wd), lambda i: (i, 0)) for wd, _ in out_defs]
    return pl.pallas_call(
        functools.partial(_proj_kernel, segs=segs),
        out_shape=out_shape,
        grid=(m // tm,),
        in_specs=[
            pl.BlockSpec((tm, d), lambda i: (i, 0)),
            pl.BlockSpec((1, d), lambda i: (0, 0)),
            pl.BlockSpec((d, c), lambda i: (0, 0)),
            pl.BlockSpec((1, c), lambda i: (0, 0)),
        ],
        out_specs=out_specs,
        compiler_params=_cparams(("parallel",)),
        name=name,
    )(x, g, w, aux)


def _pool_prompt_kernel(u_ref, halo_ref, pw_ref, ps_ref, o_ref, e_ref, *, tp):
    i = pl.program_id(1)
    e_ref[0:POOL_HALO, :] = jnp.where(i > 0, halo_ref[0], 0.0)
    e_ref[POOL_HALO:, :] = u_ref[0]
    t = i * tp + lax.broadcasted_iota(jnp.int32, (tp, 1), 0)
    for gi, w in enumerate(POOL_WINDOWS):
        sl = slice(gi * POOL_GROUP, (gi + 1) * POOL_GROUP)
        tok = e_ref[POOL_HALO:POOL_HALO + tp, sl]
        s = tok
        for k in range(1, w):
            s = s + e_ref[POOL_HALO - k:POOL_HALO - k + tp, sl]
        cnt = jnp.minimum(t + 1, w).astype(F32)
        dg = s / cnt - tok
        y = _dot(dg.astype(BF16), pw_ref[gi]) * ps_ref[:, sl]
        o_ref[0, :, sl] = y.astype(o_ref.dtype)


def _pool_prompt(u, pool_w, pool_scale, tp=512):
    n, t, c = u.shape
    hb = tp // POOL_HALO
    return pl.pallas_call(
        functools.partial(_pool_prompt_kernel, tp=tp),
        out_shape=jax.ShapeDtypeStruct((n, t, c), BF16),
        grid=(n, t // tp),
        in_specs=[
            pl.BlockSpec((1, tp, c), lambda b, i: (b, i, 0)),
            pl.BlockSpec((1, POOL_HALO, c), lambda b, i: (b, jnp.maximum(i * hb - 1, 0), 0)),
            pl.BlockSpec((len(POOL_WINDOWS), POOL_GROUP, POOL_GROUP), lambda b, i: (0, 0, 0)),
            pl.BlockSpec((1, c), lambda b, i: (0, 0)),
        ],
        out_specs=pl.BlockSpec((1, tp, c), lambda b, i: (b, i, 0)),
        scratch_shapes=[pltpu.VMEM((tp + POOL_HALO, c), F32)],
        compiler_params=_cparams(("parallel", "parallel")),
        name="pool_prompt",
    )(u, u, pool_w, pool_scale)


def _pool_sample_kernel(x_ref, pw_ref, ps_ref, o_ref, *, bn, s_len):
    base = 1 + POOL_STATE
    for gi, w in enumerate(POOL_WINDOWS):
        sl = slice(gi * POOL_GROUP, (gi + 1) * POOL_GROUP)
        tok = x_ref[:, base:base + s_len, sl]
        s = tok
        for k in range(1, w):
            s = s + x_ref[:, base - k:base - k + s_len, sl]
        dg = (s / float(w) - tok).reshape(bn * s_len, POOL_GROUP)
        y = _dot(dg.astype(BF16), pw_ref[gi]) * ps_ref[:, sl]
        o_ref[:, :, sl] = y.reshape(bn, s_len, POOL_GROUP).astype(o_ref.dtype)


def _pool_sample(x_ext, pool_w, pool_scale, s_len, bn=16):
    n, l, c = x_ext.shape
    return pl.pallas_call(
        functools.partial(_pool_sample_kernel, bn=bn, s_len=s_len),
        out_shape=jax.ShapeDtypeStruct((n, s_len, c), F32),
        grid=(n // bn,),
        in_specs=[
            pl.BlockSpec((bn, l, c), lambda b: (b, 0, 0)),
            pl.BlockSpec((len(POOL_WINDOWS), POOL_GROUP, POOL_GROUP), lambda b: (0, 0, 0)),
            pl.BlockSpec((1, c), lambda b: (0, 0)),
        ],
        out_specs=pl.BlockSpec((bn, s_len, c), lambda b: (b, 0, 0)),
        compiler_params=_cparams(("parallel",)),
        name="pool_sample",
    )(x_ext, pool_w, pool_scale)


def _compress1_kernel(x_ref, w_ref, pa_ref, pb_ref):
    acc = None
    for j in range(CMP_STRIDE):
        xj = x_ref[:, j * KV_COLS:(j + 1) * KV_COLS].astype(BF16)
        d = _dot(xj, w_ref[j])
        acc = d if acc is None else acc + d
    half = 4 * CMP_HIDDEN
    pa_ref[...] = acc[:, :half]
    pb_ref[...] = acc[:, half:]


def _compress1(x, w_bd, ts):
    r, c = x.shape
    ts = min(ts, r)
    half = 4 * CMP_HIDDEN
    return pl.pallas_call(
        _compress1_kernel,
        out_shape=[jax.ShapeDtypeStruct((r, half), F32), jax.ShapeDtypeStruct((r, half), F32)],
        grid=(r // ts,),
        in_specs=[
            pl.BlockSpec((ts, c), lambda i: (i, 0)),
            pl.BlockSpec(w_bd.shape, lambda i: (0, 0, 0)),
        ],
        out_specs=[pl.BlockSpec((ts, half), lambda i: (i, 0)), pl.BlockSpec((ts, half), lambda i: (i, 0))],
        compiler_params=_cparams(("parallel",)),
        name="compress1",
    )(x, w_bd)


def _gelu_tanh(x):
    return 0.5 * x * (1.0 + jnp.tanh(0.7978845608028654 * (x + 0.044715 * (x * x * x))))


def _compress_tail(pa, pb, pb_new, pek_ref, pev_ref, w1k_ref, w1v_ref, w2_ref, kg_ref, o_ref):
    s = pa.shape[0]
    row = lax.broadcasted_iota(jnp.int32, (s, 1), 0)
    pb_next = jnp.where(row == s - 1, pb_new, pltpu.roll(pb, s - 1, 0))
    hk = _dot(pek_ref[...], w1k_ref[...])[0:1, :]
    hv = _dot(pev_ref[...], w1v_ref[...])[0:1, :]
    pe_h = jnp.concatenate([hk, hv, hk, hv], axis=1)
    a = _gelu_tanh(pa + pb_next + pe_h)
    o = _dot(a.astype(BF16), w2_ref[...])
    lane = lax.broadcasted_iota(jnp.int32, (1, LANES), 1)
    is_key = lane < HEAD_DIM
    for gg in range(KV_GROUPS):
        sl = slice(gg * LANES, (gg + 1) * LANES)
        ob = o[:, sl]
        msq = jnp.sum(jnp.where(is_key, ob * ob, 0.0), axis=-1, keepdims=True) * (1.0 / HEAD_DIM)
        y = jnp.where(is_key, ob * lax.rsqrt(msq + RMS_EPS) * kg_ref[...], ob)
        o_ref[0, :, sl] = y.astype(o_ref.dtype)


def _compress2_kernel(pa_ref, pb_ref, pbn_ref, pek_ref, pev_ref, w1k_ref, w1v_ref, w2_ref, kg_ref, o_ref):
    _compress_tail(pa_ref[0], pb_ref[0], pbn_ref[0][0:1, :], pek_ref, pev_ref, w1k_ref, w1v_ref, w2_ref,
                   kg_ref, o_ref)


def _fetch_pages(pt_ref, cache_ref, buf, sem, b, slot, n_pages, page_rows):
    def body(p, c):
        pltpu.make_async_copy(cache_ref.at[pt_ref[b * n_pages + p]],
                              buf.at[slot, pl.ds(p * page_rows, page_rows)], sem.at[slot]).start()
        return c

    lax.fori_loop(0, n_pages, body, 0)


def _wait_pages(cache_ref, buf, sem, slot, n_pages, page_rows):
    def body(p, c):
        pltpu.make_async_copy(cache_ref.at[0], buf.at[slot, pl.ds(p * page_rows, page_rows)], sem.at[slot]).wait()
        return c

    lax.fori_loop(0, n_pages, body, 0)


def _paged_prefetch(pt_ref, cache_ref, buf, sem, n_pages, page_rows):
    b = pl.program_id(0)
    slot = b % 2

    @pl.when(b == 0)
    def _():
        _fetch_pages(pt_ref, cache_ref, buf, sem, 0, 0, n_pages, page_rows)

    @pl.when(b + 1 < pl.num_programs(0))
    def _():
        _fetch_pages(pt_ref, cache_ref, buf, sem, b + 1, 1 - slot, n_pages, page_rows)

    _wait_pages(cache_ref, buf, sem, slot, n_pages, page_rows)
    return slot


def _paged_compress_kernel(pt_ref, cache_ref, pbn_ref, w1_ref, pek_ref, pev_ref, w1k_ref, w1v_ref, w2_ref, kg_ref,
                           o_ref, buf, sem, *, n_pages):
    seg_per_page = PAGE_SIZE // CMP_STRIDE
    slot = _paged_prefetch(pt_ref, cache_ref, buf, sem, n_pages, seg_per_page)
    hid2 = 2 * CMP_HIDDEN
    pas, pbs = [], []
    for gg in range(KV_GROUPS):
        xg = jnp.concatenate(
            [buf[slot, :, j * KV_COLS + gg * LANES:j * KV_COLS + (gg + 1) * LANES].astype(BF16)
             for j in range(CMP_STRIDE)], axis=1)
        pg = _dot(xg, w1_ref[...])
        pas.append(pg[:, :hid2])
        pbs.append(pg[:, hid2:])
    _compress_tail(jnp.concatenate(pas, axis=1), jnp.concatenate(pbs, axis=1), pbn_ref[0][0:1, :],
                   pek_ref, pev_ref, w1k_ref, w1v_ref, w2_ref, kg_ref, o_ref)


def _paged_compress(page_table, cache, pb_new, w1_gs, pek, pev, w1k, w1v, w2_bd, kg):
    ns, n_pages = page_table.shape
    seg_per_page = PAGE_SIZE // CMP_STRIDE
    s = n_pages * seg_per_page
    seg_cols = CMP_STRIDE * KV_COLS
    c = pb_new.shape[-1]
    full2 = lambda b, pt: (0, 0)
    grid_spec = pltpu.PrefetchScalarGridSpec(
        num_scalar_prefetch=1,
        grid=(ns,),
        in_specs=[
            pl.BlockSpec(memory_space=pl.ANY),
            pl.BlockSpec((1, 8, c), lambda b, pt: (b, 0, 0)),
            pl.BlockSpec(w1_gs.shape, full2),
            pl.BlockSpec(pek.shape, full2),
            pl.BlockSpec(pev.shape, full2),
            pl.BlockSpec(w1k.shape, full2),
            pl.BlockSpec(w1v.shape, full2),
            pl.BlockSpec(w2_bd.shape, full2),
            pl.BlockSpec(kg.shape, full2),
        ],
        out_specs=pl.BlockSpec((1, s, KV_COLS), lambda b, pt: (b, 0, 0)),
        scratch_shapes=[pltpu.VMEM((2, s, seg_cols), F32), pltpu.SemaphoreType.DMA((2,))],
    )
    return pl.pallas_call(
        functools.partial(_paged_compress_kernel, n_pages=n_pages),
        out_shape=jax.ShapeDtypeStruct((ns, s, KV_COLS), BF16),
        grid_spec=grid_spec,
        compiler_params=_cparams(("arbitrary",)),
        name="paged_compress",
    )(page_table.reshape(-1), cache.reshape(cache.shape[0], seg_per_page, seg_cols), pb_new, w1_gs, pek, pev,
      w1k, w1v, w2_bd, kg)


def _compress2(pa, pb, pb_new, pek, pev, w1k, w1v, w2_bd, kg):
    n, s, c = pa.shape
    full2 = lambda b: (0, 0)
    return pl.pallas_call(
        _compress2_kernel,
        out_shape=jax.ShapeDtypeStruct((n, s, KV_COLS), BF16),
        grid=(n,),
        in_specs=[
            pl.BlockSpec((1, s, c), lambda b: (b, 0, 0)),
            pl.BlockSpec((1, s, c), lambda b: (b, 0, 0)),
            pl.BlockSpec((1, 8, c), lambda b: (b, 0, 0)),
            pl.BlockSpec(pek.shape, full2),
            pl.BlockSpec(pev.shape, full2),
            pl.BlockSpec(w1k.shape, full2),
            pl.BlockSpec(w1v.shape, full2),
            pl.BlockSpec(w2_bd.shape, full2),
            pl.BlockSpec(kg.shape, full2),
        ],
        out_specs=pl.BlockSpec((1, s, KV_COLS), lambda b: (b, 0, 0)),
        compiler_params=_cparams(("parallel",)),
        name="compress2",
    )(pa, pb, pb_new, pek, pev, w1k, w1v, w2_bd, kg)


def _stack_heads(q, hg):
    return jnp.concatenate([q[:, h * LANES:(h + 1) * LANES] for h in range(hg)], axis=0).astype(BF16)


def _row_qpos(hg, tq, q0):
    r = lax.broadcasted_iota(jnp.int32, (hg * tq, 1), 0)
    return q0 + (r & (tq - 1))


def _store_heads(o_ref, lead, o, hg, tq):
    lane = lax.broadcasted_iota(jnp.int32, (1, LANES), 1)
    for h in range(hg):
        blk = jnp.where(lane >= HEAD_DIM, o[h * tq:(h + 1) * tq], 0.0)
        o_ref[lead + (slice(None), slice(h * LANES, (h + 1) * LANES))] = blk.astype(o_ref.dtype)


def _cmp_attn_kernel(q_ref, kc_ref, cov_ref, o_ref, sel_ref, *, bn, hg, tq, q_off, n_blocks, row_off, as_bias):
    i = pl.program_id(2)
    s_len = kc_ref.shape[1]
    q0 = q_off + i * tq
    qpos = _row_qpos(hg, tq, q0)
    blk_end = lax.broadcasted_iota(jnp.int32, (1, s_len), 1) * CMP_STRIDE + (CMP_BLOCK - 1)
    cond = blk_end <= qpos
    psums = []
    for b in range(bn):
        qs = _stack_heads(q_ref[b], hg)
        kc = kc_ref[b]
        s = jnp.where(cond, _dot_nt(qs, kc), NEG_INF)
        m = jnp.max(s, axis=-1, keepdims=True)
        p = jnp.where(cond, jnp.exp(s - m), 0.0)
        den = jnp.sum(p, axis=-1, keepdims=True)
        p = p / jnp.maximum(den, 1e-30)
        o = _dot(p.astype(BF16), kc)
        _store_heads(o_ref, (b,), o, hg, tq)
        ps = p[0:tq]
        for h in range(1, hg):
            ps = ps + p[h * tq:(h + 1) * tq]
        psums.append(ps)
    ps = psums[0] if bn == 1 else jnp.concatenate(psums, axis=0)
    hi, lo = _split_bf16(ps)
    cov = cov_ref[...]
    score = _dot_nt(cov, hi) + _dot_nt(cov, lo)
    nb_pad, nl = score.shape
    j = lax.broadcasted_iota(jnp.int32, (nb_pad, nl), 0) - row_off
    t = q0 + (lax.broadcasted_iota(jnp.int32, (1, nl), 1) & (tq - 1))
    cur = t >> SLC_SHIFT
    forced = (j == 0) | (j == cur) | (j == cur - 1)
    valid = j * SLC_BLOCK <= t
    score = jnp.where(valid, jnp.where(forced, FORCE_SCORE, score), NEG_INF)
    is_block = lax.bitcast_convert_type(j, jnp.uint32) < jnp.uint32(n_blocks)
    score = jnp.where(is_block, score, PAD_SCORE)
    sel = jnp.zeros((nb_pad, nl), F32)
    for _ in range(min(SLC_TOPN, n_blocks)):
        mx = jnp.max(score, axis=0, keepdims=True)
        jm = jnp.min(jnp.where(score == mx, j, nb_pad), axis=0, keepdims=True)
        pick = j == jm
        sel = jnp.where(pick, 1.0, sel)
        score = jnp.where(pick, -jnp.inf, score)
    if as_bias:
        sel = jnp.where(is_block, (sel - 1.0) * (-NEG_INF), 0.0)
    sel_t = sel.T
    for b in range(bn):
        sel_ref[b, 0] = sel_t[b * tq:(b + 1) * tq]


def _cmp_attn(q, kc, cov_t, *, hg, tq, bn, q_off, n_blocks, out_dtype, row_off, as_bias):
    n, t, hc = q.shape
    s_len = kc.shape[1]
    nb_pad = cov_t.shape[0]
    g = KV_GROUPS
    return pl.pallas_call(
        functools.partial(_cmp_attn_kernel, bn=bn, hg=hg, tq=tq, q_off=q_off, n_blocks=n_blocks,
                          row_off=row_off, as_bias=as_bias),
        out_shape=[jax.ShapeDtypeStruct((n, t, hc), out_dtype), jax.ShapeDtypeStruct((n, g, t, nb_pad), F32)],
        grid=(n // bn, g, t // tq),
        in_specs=[
            pl.BlockSpec((bn, tq, hg * LANES), lambda b, gg, i: (b, i, gg)),
            pl.BlockSpec((bn, s_len, LANES), lambda b, gg, i: (b, 0, gg)),
            pl.BlockSpec((nb_pad, s_len), lambda b, gg, i: (0, 0)),
        ],
        out_specs=[
            pl.BlockSpec((bn, tq, hg * LANES), lambda b, gg, i: (b, i, gg)),
            pl.BlockSpec((bn, 1, tq, nb_pad), lambda b, gg, i: (b, gg, i, 0)),
        ],
        compiler_params=_cparams(("parallel", "parallel", "parallel")),
        name="cmp_attn",
    )(q, kc, cov_t)


def _flash_kernel(*refs, hg, tq, tk, window, use_sel, use_sink, chunk):
    refs = list(refs)
    sink_ref = refs.pop(0) if use_sink else None
    q_ref = refs.pop(0)
    kv_ref = refs.pop(0)
    selb_ref = refs.pop(0) if use_sel else None
    o_ref, qs_sc, m_sc, acc_sc = refs
    g = pl.program_id(1)
    i = pl.program_id(2)
    rows = hg * tq
    q0 = i * tq
    for h in range(hg):
        qh = q_ref[0, :, h * LANES:(h + 1) * LANES]
        if use_sel:
            qh = qh + selb_ref[0, 0]
        qs_sc[h * tq:(h + 1) * tq, :] = qh.astype(BF16)
    m_sc[...] = jnp.full((rows, LANES), NEG_INF, F32)
    acc_sc[...] = jnp.zeros((rows, LANES), F32)

    def tile(k0, masked):
        kv = kv_ref[0, pl.ds(k0, tk), :]
        klane = lax.broadcasted_iota(jnp.int32, (tk, LANES), 1)
        if use_sel:
            krow = lax.broadcasted_iota(jnp.int32, (tk, LANES), 0)
            onehot = jnp.where(klane - HEAD_DIM == ((k0 + krow) >> SLC_SHIFT), 1.0, 0.0).astype(BF16)
            kaug = jnp.where(klane < HEAD_DIM, kv, onehot)
        else:
            kaug = kv
        vaug = jnp.where(klane < HEAD_DIM, jnp.ones_like(kv), kv)
        for c in range(rows // chunk):
            r0 = c * chunk
            s = _dot_nt(qs_sc[r0:r0 + chunk, :], kaug)
            if masked:
                qpos = q0 + ((r0 + lax.broadcasted_iota(jnp.int32, (chunk, 1), 0)) & (tq - 1))
                d = qpos - (k0 + lax.broadcasted_iota(jnp.int32, (1, tk), 1))
                if window is None:
                    s = jnp.where(d >= 0, s, NEG_INF)
                else:
                    s = jnp.where(lax.bitcast_convert_type(d, jnp.uint32) <= jnp.uint32(window), s, NEG_INF)
            m_prev = m_sc[r0:r0 + chunk, :]
            m_new = jnp.maximum(m_prev, jnp.max(s, axis=-1, keepdims=True))
            alpha = jnp.exp(m_prev - m_new)
            p = jnp.exp(s - jnp.concatenate([m_new] * (tk // LANES), axis=1))
            acc_sc[r0:r0 + chunk, :] = alpha * acc_sc[r0:r0 + chunk, :] + _dot(p.astype(BF16), vaug)
            m_sc[r0:r0 + chunk, :] = m_new

    if window is None:
        def body(jt, c):
            tile(pl.multiple_of(jt * tk, tk), False)
            return c

        lax.fori_loop(0, i, body, 0)
        tile(pl.multiple_of(i * tk, tk), True)
    else:
        tile(pl.multiple_of(jnp.maximum(q0 - window, 0), LANES), True)

    acc = acc_sc[...]
    den = pltpu.roll(acc, HEAD_DIM, 1)
    m = m_sc[...]
    if use_sink:
        sink = jnp.concatenate([jnp.full((tq, LANES), sink_ref[g * hg + h], F32) for h in range(hg)], axis=0)
        m_fin = jnp.maximum(m, sink)
        w = jnp.exp(m - m_fin)
        o = acc * w / jnp.maximum(den * w + jnp.exp(sink - m_fin), 1e-30)
    else:
        o = jnp.where(m > 0.5 * NEG_INF, acc / jnp.maximum(den, 1e-30), 0.0)
    lane = lax.broadcasted_iota(jnp.int32, (1, LANES), 1)
    o = jnp.where(lane >= HEAD_DIM, o, 0.0)
    for h in range(hg):
        o_ref[0, :, h * LANES:(h + 1) * LANES] = o[h * tq:(h + 1) * tq].astype(o_ref.dtype)


def _flash(q, kv, *, hg, tq, window, chunk, selb=None, sinks=None):
    n, t, hc = q.shape
    g = KV_GROUPS
    use_sel = selb is not None
    use_sink = sinks is not None
    rows = hg * tq
    tk = tq if window is None else tq + window
    if use_sel:
        assert t // SLC_BLOCK <= LANES - HEAD_DIM, "selection bias needs one upper query lane per block"
    in_specs = []
    args = []
    if use_sink:
        in_specs.append(pl.BlockSpec(memory_space=pltpu.SMEM))
        args.append(sinks)
    in_specs += [
        pl.BlockSpec((1, tq, hg * LANES), lambda b, gg, i: (b, i, gg)),
        pl.BlockSpec((1, t, LANES), lambda b, gg, i: (b, 0, gg)),
    ]
    args += [q, kv]
    if use_sel:
        in_specs.append(pl.BlockSpec((1, 1, tq, LANES), lambda b, gg, i: (b, gg, i, 0)))
        args.append(selb)
    return pl.pallas_call(
        functools.partial(_flash_kernel, hg=hg, tq=tq, tk=tk, window=window, use_sel=use_sel,
                          use_sink=use_sink, chunk=chunk),
        out_shape=jax.ShapeDtypeStruct((n, t, hc), BF16),
        grid=(n, g, t // tq),
        in_specs=in_specs,
        out_specs=pl.BlockSpec((1, tq, hg * LANES), lambda b, gg, i: (b, i, gg)),
        scratch_shapes=[pltpu.VMEM((rows, LANES), BF16), pltpu.VMEM((rows, LANES), F32),
                        pltpu.VMEM((rows, LANES), F32)],
        compiler_params=_cparams(("parallel", "parallel", "parallel")),
        name="flash",
    )(*args)


def _attn_kernel(*refs, hg, tq, window, q_off, k_off, use_sink):
    refs = list(refs)
    sink_ref = refs.pop(0) if use_sink else None
    q_ref = refs.pop(0)
    kv_ref = refs.pop(0)
    (o_ref,) = refs
    g = pl.program_id(1)
    qs = _stack_heads(q_ref[0], hg)
    qpos = _row_qpos(hg, tq, q_off)
    kv = kv_ref[0]
    tk = kv.shape[0]
    s = _dot_nt(qs, kv)
    d = qpos - (k_off + lax.broadcasted_iota(jnp.int32, (1, tk), 1))
    if window is None:
        s = jnp.where(d >= 0, s, NEG_INF)
    else:
        s = jnp.where(lax.bitcast_convert_type(d, jnp.uint32) <= jnp.uint32(window), s, NEG_INF)
    m = jnp.max(s, axis=-1, keepdims=True)
    if use_sink:
        sink = jnp.concatenate([jnp.full((tq, 1), sink_ref[g * hg + h], F32) for h in range(hg)], axis=0)
        m = jnp.maximum(m, sink)
    p = jnp.exp(s - m)
    den = jnp.sum(p, axis=-1, keepdims=True)
    if use_sink:
        den = den + jnp.exp(sink - m)
    o = jnp.where(m > 0.5 * NEG_INF, _dot(p.astype(BF16), kv) / jnp.maximum(den, 1e-30), 0.0)
    _store_heads(o_ref, (0,), o, hg, tq)


def _attn(q, kv, *, hg, window, q_off, k_off, sinks=None):
    n, tq, hc = q.shape
    tkv = kv.shape[1]
    g = KV_GROUPS
    use_sink = sinks is not None
    in_specs = []
    args = []
    if use_sink:
        in_specs.append(pl.BlockSpec(memory_space=pltpu.SMEM))
        args.append(sinks)
    in_specs += [
        pl.BlockSpec((1, tq, hg * LANES), lambda b, gg: (b, 0, gg)),
        pl.BlockSpec((1, tkv, LANES), lambda b, gg: (b, 0, gg)),
    ]
    args += [q, kv]
    return pl.pallas_call(
        functools.partial(_attn_kernel, hg=hg, tq=tq, window=window, q_off=q_off, k_off=k_off,
                          use_sink=use_sink),
        out_shape=jax.ShapeDtypeStruct((n, tq, hc), F32),
        grid=(n, g),
        in_specs=in_specs,
        out_specs=pl.BlockSpec((1, tq, hg * LANES), lambda b, gg: (b, 0, gg)),
        compiler_params=_cparams(("parallel", "parallel")),
        name="attn",
    )(*args)


def _paged_slc_kernel(pt_ref, cache_ref, q_ref, new_ref, sel_ref, e_ref, o_ref, buf, sem, *, n_pages, hg, tq,
                      q_off):
    b = pl.program_id(0)
    past = n_pages * PAGE_SIZE
    tk = buf.shape[1]

    @pl.when(b == 0)
    def _():
        for sl in range(2):
            buf[sl, past + tq:, :] = jnp.zeros((tk - past - tq, KV_COLS), F32)

    slot = _paged_prefetch(pt_ref, cache_ref, buf, sem, n_pages, PAGE_SIZE)
    buf[slot, past:past + tq, :] = new_ref[0]
    kv = buf[slot].astype(BF16)
    rows = hg * tq
    zero = jnp.zeros((rows, LANES), F32)
    q = q_ref[0]
    stack = lambda g: jnp.concatenate([q[:, (g * hg + h) * LANES:(g * hg + h + 1) * LANES] for h in range(hg)],
                                      axis=0)
    q2 = jnp.concatenate([jnp.concatenate([stack(0), zero], axis=1),
                          jnp.concatenate([zero, stack(1)], axis=1)], axis=0).astype(BF16)
    s = _dot_nt(q2, kv)
    r = lax.broadcasted_iota(jnp.int32, (2 * rows, 1), 0)
    qpos = q_off + (r & (tq - 1))
    s = jnp.where(qpos - lax.broadcasted_iota(jnp.int32, (1, tk), 1) >= 0, s, NEG_INF)
    sel = jnp.concatenate([sel_ref[0, 0]] * hg + [sel_ref[0, 1]] * hg, axis=0).astype(BF16)
    s = jnp.where(_dot(sel, e_ref[...]) > 0.5, s, NEG_INF)
    m = jnp.max(s, axis=-1, keepdims=True)
    p = jnp.exp(s - m)
    den = jnp.sum(p, axis=-1, keepdims=True)
    o = jnp.where(m > 0.5 * NEG_INF, _dot(p.astype(BF16), kv) / jnp.maximum(den, 1e-30), 0.0)
    lane = lax.broadcasted_iota(jnp.int32, (1, LANES), 1)
    for g in range(KV_GROUPS):
        for h in range(hg):
            blk = o[g * rows + h * tq:g * rows + (h + 1) * tq, g * LANES:(g + 1) * LANES]
            c0 = (g * hg + h) * LANES
            o_ref[0, :, c0:c0 + LANES] = jnp.where(lane >= HEAD_DIM, blk, 0.0)


def _paged_slc(page_table, cache, q, new_rows, sel, expand, *, hg, q_off):
    ns, n_pages = page_table.shape
    _, tq, hc = q.shape
    tk = expand.shape[1]
    grid_spec = pltpu.PrefetchScalarGridSpec(
        num_scalar_prefetch=1,
        grid=(ns,),
        in_specs=[
            pl.BlockSpec(memory_space=pl.ANY),
            pl.BlockSpec((1, tq, hc), lambda b, pt: (b, 0, 0)),
            pl.BlockSpec((1, tq, KV_COLS), lambda b, pt: (b, 0, 0)),
            pl.BlockSpec((1, KV_GROUPS, tq, sel.shape[-1]), lambda b, pt: (b, 0, 0, 0)),
            pl.BlockSpec(expand.shape, lambda b, pt: (0, 0)),
        ],
        out_specs=pl.BlockSpec((1, tq, hc), lambda b, pt: (b, 0, 0)),
        scratch_shapes=[pltpu.VMEM((2, tk, KV_COLS), F32), pltpu.SemaphoreType.DMA((2,))],
    )
    return pl.pallas_call(
        functools.partial(_paged_slc_kernel, n_pages=n_pages, hg=hg, tq=tq, q_off=q_off),
        out_shape=jax.ShapeDtypeStruct((ns, tq, hc), F32),
        grid_spec=grid_spec,
        compiler_params=_cparams(("arbitrary",)),
        name="paged_slc",
    )(page_table.reshape(-1), cache.reshape(cache.shape[0], PAGE_SIZE, KV_COLS), q, new_rows, sel, expand)


def _out0_kernel(x_ref, pool_ref, oc_ref, os_ref, ow_ref, gt_ref, eg_ref, wa_ref, wb_ref, o_ref):
    ghi, glo = _split_bf16(gt_ref[...])
    o = None
    for b, br in enumerate((oc_ref, os_ref, ow_ref)):
        gate = _dot(ghi, eg_ref[b]) + _dot(glo, eg_ref[b])
        term = gate * br[...].astype(F32)
        o = term if o is None else o + term
    mix = _dot(pool_ref[...], wa_ref[...]) + _dot(o.astype(BF16), wb_ref[...])
    o_ref[...] = x_ref[...] + mix


def _out0(x, pool_o, o_cmp, o_slc, o_win, gates, eg, wa, wb, tm=512):
    m, d = x.shape
    hc = o_cmp.shape[1]
    row = lambda w: pl.BlockSpec((tm, w), lambda i: (i, 0))
    return pl.pallas_call(
        _out0_kernel,
        out_shape=jax.ShapeDtypeStruct((m, d), F32),
        grid=(m // tm,),
        in_specs=[row(d), row(POOL_CH), row(hc), row(hc), row(hc), row(LANES),
                  pl.BlockSpec(eg.shape, lambda i: (0, 0, 0)),
                  pl.BlockSpec(wa.shape, lambda i: (0, 0)),
                  pl.BlockSpec(wb.shape, lambda i: (0, 0))],
        out_specs=row(d),
        compiler_params=_cparams(("parallel",)),
        name="out0",
    )(x, pool_o, o_cmp, o_slc, o_win, gates, eg, wa, wb)


def _ffn_kernel(x_ref, g_ref, wg_ref, wu_ref, wd_ref, o_ref, h_sc, acc_sc):
    f = pl.program_id(1)

    @pl.when(f == 0)
    def _():
        x = x_ref[...]
        ms = jnp.mean(x * x, axis=-1, keepdims=True)
        h_sc[...] = (x * lax.rsqrt(ms + RMS_EPS) * g_ref[...]).astype(BF16)
        acc_sc[...] = x

    h = h_sc[...]
    a = _dot(h, wg_ref[...])
    u = _dot(h, wu_ref[...])
    act = (a * jax.nn.sigmoid(a) * u).astype(BF16)
    acc_sc[...] += _dot(act, wd_ref[...])

    @pl.when(f == pl.num_programs(1) - 1)
    def _():
        o_ref[...] = acc_sc[...]


def _ffn(x, g, wg, wu, wd, tm=512, nf=2):
    m, d = x.shape
    fdim = wg.shape[1]
    tf = fdim // nf
    return pl.pallas_call(
        _ffn_kernel,
        out_shape=jax.ShapeDtypeStruct((m, d), F32),
        grid=(m // tm, nf),
        in_specs=[
            pl.BlockSpec((tm, d), lambda i, f: (i, 0)),
            pl.BlockSpec((1, d), lambda i, f: (0, 0)),
            pl.BlockSpec((d, tf), lambda i, f: (0, f)),
            pl.BlockSpec((d, tf), lambda i, f: (0, f)),
            pl.BlockSpec((tf, d), lambda i, f: (f, 0)),
        ],
        out_specs=pl.BlockSpec((tm, d), lambda i, f: (i, 0)),
        scratch_shapes=[pltpu.VMEM((tm, d), BF16), pltpu.VMEM((tm, d), F32)],
        compiler_params=_cparams(("parallel", "arbitrary")),
        name="ffn",
    )(x, g, wg, wu, wd)


def _out1_kernel(x_ref, o_ref_in, w_ref, g_ref, rwh_ref, rwl_ref, rb_ref, xo_ref, h_ref, r_ref):
    x = x_ref[...] + _dot(o_ref_in[...], w_ref[...])
    xo_ref[...] = x
    ms = jnp.mean(x * x, axis=-1, keepdims=True)
    h = x * lax.rsqrt(ms + RMS_EPS) * g_ref[...]
    h_ref[...] = h.astype(BF16)
    hi, lo = _split_bf16(h)
    logits = _dot(hi, rwh_ref[...]) + _dot(lo, rwh_ref[...]) + _dot(hi, rwl_ref[...]) + rb_ref[...]
    lane = lax.broadcasted_iota(jnp.int32, logits.shape, 1)
    logits = jnp.where(lane < N_EXPERTS, logits, -jnp.inf)
    m1 = jnp.max(logits, axis=-1, keepdims=True)
    i1 = jnp.min(jnp.where(logits == m1, lane, LANES), axis=-1, keepdims=True)
    rest = jnp.where(lane == i1, -jnp.inf, logits)
    m2 = jnp.max(rest, axis=-1, keepdims=True)
    i2 = jnp.min(jnp.where(rest == m2, lane, LANES), axis=-1, keepdims=True)
    e2 = jnp.exp(m2 - m1)
    g1 = 1.0 / (1.0 + e2)
    g2 = e2 / (1.0 + e2)
    r = jnp.where(lane == 0, i1.astype(F32), 0.0)
    r = jnp.where(lane == 1, i2.astype(F32), r)
    r = jnp.where(lane == 2, g1, r)
    r = jnp.where(lane == 3, g2, r)
    r_ref[...] = r


def _out1(x, o, w, g, rwh, rwl, rb, tm=512):
    m, d = x.shape
    hc = o.shape[1]
    row = lambda wd: pl.BlockSpec((tm, wd), lambda i: (i, 0))
    full = lambda a: pl.BlockSpec(a.shape, lambda i: (0, 0))
    return pl.pallas_call(
        _out1_kernel,
        out_shape=[jax.ShapeDtypeStruct((m, d), F32), jax.ShapeDtypeStruct((m, d), BF16),
                   jax.ShapeDtypeStruct((m, LANES), F32)],
        grid=(m // tm,),
        in_specs=[row(d), row(hc), full(w), full(g), full(rwh), full(rwl), full(rb)],
        out_specs=[row(d), row(d), row(LANES)],
        compiler_params=_cparams(("parallel",)),
        name="out1",
    )(x, o, w, g, rwh, rwl, rb)


def _moe_kernel(be_ref, nb_ref, x_ref, wg_ref, wu_ref, wd_ref, o_ref, acc_sc):
    b = pl.program_id(0)
    f = pl.program_id(1)
    used = b < nb_ref[0]

    @pl.when(f == 0)
    def _():
        acc_sc[...] = jnp.zeros_like(acc_sc)

    @pl.when(used)
    def _():
        x = x_ref[...]
        a = _dot(x, wg_ref[0])
        u = _dot(x, wu_ref[0])
        act = (a * jax.nn.sigmoid(a) * u).astype(BF16)
        acc_sc[...] += _dot(act, wd_ref[0])

    @pl.when(f == pl.num_programs(1) - 1)
    def _():
        o_ref[...] = acc_sc[...]


def _moe(xs, blk_e, n_used, wg, wu, wd, tm, tf):
    cap, d = xs.shape
    fdim = wg.shape[2]
    grid_spec = pltpu.PrefetchScalarGridSpec(
        num_scalar_prefetch=2,
        grid=(cap // tm, fdim // tf),
        in_specs=[
            pl.BlockSpec((tm, d), lambda b, f, be, nb: (b, 0)),
            pl.BlockSpec((1, d, tf), lambda b, f, be, nb: (be[b], 0, f)),
            pl.BlockSpec((1, d, tf), lambda b, f, be, nb: (be[b], 0, f)),
            pl.BlockSpec((1, tf, d), lambda b, f, be, nb: (be[b], f, 0)),
        ],
        out_specs=pl.BlockSpec((tm, d), lambda b, f, be, nb: (b, 0)),
        scratch_shapes=[pltpu.VMEM((tm, d), F32)],
    )
    return pl.pallas_call(
        _moe_kernel,
        out_shape=jax.ShapeDtypeStruct((cap, d), F32),
        grid_spec=grid_spec,
        compiler_params=_cparams(("parallel", "arbitrary")),
        name="moe",
    )(blk_e, n_used, xs, wg, wu, wd)


def _pad_heads_cols(w, heads):
    d = w.shape[0]
    w = w.reshape(d, heads, HEAD_DIM)
    return jnp.pad(w, ((0, 0), (0, 0), (0, LANES - HEAD_DIM))).reshape(d, heads * LANES)


def _pad_heads_rows(w, heads):
    d = w.shape[1]
    w = w.reshape(heads, HEAD_DIM, d)
    return jnp.pad(w, ((0, 0), (LANES - HEAD_DIM, 0), (0, 0))).reshape(heads * LANES, d)


def _head_gain(gain, heads, scale):
    g = jnp.pad(gain.astype(F32) * scale, (0, LANES - HEAD_DIM))
    return jnp.tile(g, heads)


def _key_gain(gain):
    return jnp.concatenate([gain.astype(F32), jnp.ones((HEAD_DIM,), F32)])


def _cover_t(n_cmp_pad, n_cmp, n_blocks, nb_pad, row_off):
    m = jnp.arange(n_cmp_pad)
    c_start = m * CMP_STRIDE
    c_end = c_start + CMP_BLOCK - 1
    jb = jnp.arange(nb_pad) - row_off
    b_start = jb * SLC_BLOCK
    b_end = b_start + SLC_BLOCK - 1
    cov = (c_start[None, :] <= b_end[:, None]) & (c_end[None, :] >= b_start[:, None])
    cov = cov & (m[None, :] < n_cmp) & (jb[:, None] >= 0) & (jb[:, None] < n_blocks)
    return cov.astype(BF16)


def _route(r, n_tok, tm):
    top_e = r[:, 0:2].astype(jnp.int32)
    gate = r[:, 2:4]
    n_asg = n_tok * 2
    flat_e = top_e.reshape(n_asg)
    order = jnp.argsort(flat_e)
    sorted_e = flat_e[order]
    counts = jnp.sum(flat_e[:, None] == jnp.arange(N_EXPERTS)[None, :], axis=0).astype(jnp.int32)
    padded = (counts + tm - 1) // tm * tm
    start = jnp.cumsum(counts) - counts
    pend = jnp.cumsum(padded)
    pstart = pend - padded
    dest_sorted = pstart[sorted_e] + jnp.arange(n_asg, dtype=jnp.int32) - start[sorted_e]
    dest = jnp.zeros((n_asg,), jnp.int32).at[order].set(dest_sorted.astype(jnp.int32))
    n_blk = n_asg // tm + N_EXPERTS
    cap = n_blk * tm
    row_tok = jnp.full((cap,), n_tok, jnp.int32).at[dest].set(jnp.arange(n_asg, dtype=jnp.int32) // 2)
    blk_e = jnp.minimum(jnp.sum(jnp.arange(n_blk)[:, None] * tm >= pend[None, :], axis=1), N_EXPERTS - 1)
    n_used = (pend[-1] // tm).astype(jnp.int32).reshape(1)
    return gate, dest, row_tok, blk_e.astype(jnp.int32), n_used


def kernel(x_prompt, x_sample, state_pool, cache_nsa_cmp, cache_nsa_slc, cache_nsa_win, cache_swa, page_table, norm0_mix, w_in0, pool_w, pool_scale, nsa_q_gain, nsa_k_gain, cmp_k_w1, cmp_k_w2, cmp_k_pe, cmp_v_w1, cmp_v_w2, cmp_v_pe, w_out0, norm0_ffn, ffn_w_gate, ffn_w_up, ffn_w_down, norm1_mix, w_in1, swa_q_gain, swa_k_gain, swa_sinks, w_out1, norm1_ffn, router_w, router_b, moe_w_gate, moe_w_up, moe_w_down):
    nb, t, d = x_prompt.shape
    ns, s_len, _ = x_sample.shape
    mp = nb * t
    msz = ns * s_len
    past = page_table.shape[1] * PAGE_SIZE
    total = past + s_len
    g = KV_GROUPS

    x_all = jnp.concatenate([x_prompt.reshape(mp, d), x_sample.reshape(msz, d)], axis=0)

    c_q = POOL_CH
    c_kv = c_q + NSA_HEADS * HEAD_DIM
    c_gate = c_kv + 3 * KV_COLS
    n_gate = 3 * NSA_HEADS
    w0 = jnp.concatenate([
        w_in0[:, :c_q],
        _pad_heads_cols(w_in0[:, c_q:c_kv], NSA_HEADS),
        w_in0[:, c_kv:c_gate],
        jnp.pad(w_in0[:, c_gate:], ((0, 0), (0, LANES - n_gate))),
    ], axis=1).astype(BF16)
    qw = NSA_HEADS * LANES
    aux0 = jnp.concatenate([
        jnp.ones((POOL_CH,), F32),
        _head_gain(nsa_q_gain, NSA_HEADS, ATTN_SCALE),
        jnp.ones((KV_COLS,), F32),
        jnp.tile(_key_gain(nsa_k_gain[1]), g),
        jnp.tile(_key_gain(nsa_k_gain[2]), g),
        jnp.ones((LANES,), F32),
    ]).reshape(1, -1)
    o0 = POOL_CH
    o1 = o0 + qw
    o2 = o1 + KV_COLS
    o3 = o2 + KV_COLS
    o4 = o3 + KV_COLS
    segs0 = (("raw", 0, POOL_CH), ("qnorm", o0, qw), ("raw", o1, KV_COLS), ("kvnorm", o2, KV_COLS),
             ("kvnorm", o3, KV_COLS), ("sigmoid", o4, LANES))
    outs0 = ((POOL_CH, F32), (qw, F32), (KV_COLS, F32), (KV_COLS, F32), (KV_COLS, BF16),
             (KV_COLS, F32), (KV_COLS, BF16), (LANES, F32))
    u_all, q_all, cmp_all, slc_all, slc_bf, win_all, win_bf, gates_all = _proj(
        x_all, norm0_mix.reshape(1, d), w0, aux0, segs0, outs0, 512, "proj0")

    pool_w_bf = pool_w.astype(BF16)
    pool_scale2 = pool_scale.reshape(1, POOL_CH).astype(F32)

    u_p = u_all[:mp].reshape(nb, t, POOL_CH)
    u_s = u_all[mp:].reshape(ns, s_len, POOL_CH)
    pool_o_p = _pool_prompt(u_p, pool_w_bf, pool_scale2)
    u_ext = jnp.concatenate([state_pool.astype(F32), u_s], axis=1)
    x_ext = jnp.pad(u_ext, ((0, 0), (1, 0), (0, 0)))
    pool_o_s = _pool_sample(x_ext, pool_w_bf, pool_scale2, s_len)
    pool_o = jnp.concatenate([pool_o_p.reshape(mp, POOL_CH), pool_o_s.reshape(msz, POOL_CH).astype(BF16)], axis=0)

    def w1_parts(w1):
        return w1.reshape(2, CMP_STRIDE, HEAD_DIM, CMP_HIDDEN)

    w_sel = jnp.stack([w1_parts(cmp_k_w1), w1_parts(cmp_v_w1)] * g, axis=0)
    w1_bd = jnp.einsum("cpjdh,ce->jcdpeh", w_sel, jnp.eye(2 * g, dtype=F32))
    w1_bd = w1_bd.reshape(CMP_STRIDE, KV_COLS, 2 * 2 * g * CMP_HIDDEN).astype(BF16)
    w2_sel = jnp.stack([cmp_k_w2, cmp_v_w2] * g, axis=0)
    w2_bd = jnp.einsum("chd,ce->ched", w2_sel, jnp.eye(2 * g, dtype=F32))
    w2_bd = w2_bd.reshape(2 * g * CMP_HIDDEN, KV_COLS).astype(BF16)
    pek = jnp.pad(cmp_k_pe.reshape(1, -1), ((0, 7), (0, 0))).astype(BF16)
    pev = jnp.pad(cmp_v_pe.reshape(1, -1), ((0, 7), (0, 0))).astype(BF16)
    w1k = cmp_k_w1.astype(BF16)
    w1v = cmp_v_w1.astype(BF16)
    kg0 = _key_gain(nsa_k_gain[0]).reshape(1, LANES)
    seg_cols = CMP_STRIDE * KV_COLS
    half = 2 * g * CMP_HIDDEN

    n_seg_p = t // CMP_STRIDE
    pa_p, pb_p = _compress1(cmp_all[:mp].reshape(nb * n_seg_p, seg_cols), w1_bd, 256)
    kcv_p = _compress2(pa_p.reshape(nb, n_seg_p, half), pb_p.reshape(nb, n_seg_p, half),
                       jnp.zeros((nb, 8, half), F32), pek, pev, w1k, w1v, w2_bd, kg0)
    n_seg_s = past // CMP_STRIDE
    pad_new = -(-total // CMP_STRIDE) * CMP_STRIDE - past
    assert pad_new == CMP_STRIDE, "the new rows must fit one segment"
    new_seg = jnp.pad(cmp_all[mp:].reshape(ns, s_len * KV_COLS), ((0, 0), (0, (pad_new - s_len) * KV_COLS)))
    _, pb_new = _compress1(new_seg, w1_bd, ns)
    pb_new = jnp.pad(pb_new.reshape(ns, 1, half), ((0, 0), (0, 7), (0, 0)))
    w_kv = jnp.stack([w1_parts(cmp_k_w1), w1_parts(cmp_v_w1)], axis=0)
    w1_gs = jnp.einsum("kpjdh,ke->jkdpeh", w_kv, jnp.eye(2, dtype=F32))
    w1_gs = w1_gs.reshape(CMP_STRIDE * 2 * HEAD_DIM, 2 * 2 * CMP_HIDDEN).astype(BF16)
    kcv_s = _paged_compress(page_table, cache_nsa_cmp, pb_new, w1_gs, pek, pev, w1k, w1v, w2_bd, kg0)

    hg0 = NSA_HEADS // g
    q_p = q_all[:mp].reshape(nb, t, qw)
    q_s = q_all[mp:].reshape(ns, s_len, qw)
    nblk_p = t // SLC_BLOCK
    cov_p = _cover_t(n_seg_p, n_seg_p - 1, nblk_p, LANES, HEAD_DIM)
    o_cmp_p, selb_p = _cmp_attn(q_p, kcv_p, cov_p, hg=hg0, tq=256, bn=1, q_off=0, n_blocks=nblk_p,
                                out_dtype=BF16, row_off=HEAD_DIM, as_bias=True)
    nblk_s = -(-total // SLC_BLOCK)
    nbp_s = -(-nblk_s // LANES) * LANES
    n_cmp_s = -(-total // CMP_STRIDE) - 1
    cov_s = _cover_t(n_seg_s, n_cmp_s, nblk_s, nbp_s, 0)
    o_cmp_s, sel_s = _cmp_attn(q_s, kcv_s, cov_s, hg=hg0, tq=s_len, bn=LANES // s_len, q_off=past,
                               n_blocks=nblk_s, out_dtype=F32, row_off=0, as_bias=False)

    slc_p = slc_bf[:mp].reshape(nb, t, KV_COLS)
    o_slc_p = _flash(q_p, slc_p, hg=hg0, tq=512, window=None, chunk=256, selb=selb_p)
    tk_s = -(-total // PAGE_SIZE) * PAGE_SIZE
    expand_s = (jnp.arange(tk_s)[None, :] // SLC_BLOCK == jnp.arange(nbp_s)[:, None]).astype(BF16)
    o_slc_s = _paged_slc(page_table, cache_nsa_slc, q_s, slc_all[mp:].reshape(ns, s_len, KV_COLS), sel_s,
                         expand_s, hg=hg0, q_off=past)

    win_p_rows = win_bf[:mp].reshape(nb, t, KV_COLS)
    o_win_p = _flash(q_p, win_p_rows, hg=hg0, tq=256, window=NSA_WINDOW, chunk=256)
    wb_len = cache_nsa_win.shape[1]
    tk_w = -(-(wb_len + s_len) // 16) * 16
    kv_win_s = jnp.concatenate([
        cache_nsa_win.astype(BF16).reshape(ns, wb_len, KV_COLS), win_bf[mp:].reshape(ns, s_len, KV_COLS),
        jnp.zeros((ns, tk_w - wb_len - s_len, KV_COLS), BF16)], axis=1)
    o_win_s = _attn(q_s, kv_win_s, hg=hg0, window=NSA_WINDOW, q_off=past, k_off=past - wb_len)

    cat = lambda a, b: jnp.concatenate([a.reshape(mp, -1), b.reshape(msz, -1).astype(BF16)], axis=0)
    eg = (jnp.arange(LANES)[None, :, None] ==
          (jnp.arange(3)[:, None, None] * NSA_HEADS + jnp.arange(qw)[None, None, :] // LANES)).astype(BF16)
    wa0 = w_out0[:POOL_CH].astype(BF16)
    wb0 = _pad_heads_rows(w_out0[POOL_CH:], NSA_HEADS).astype(BF16)
    x1 = _out0(x_all, pool_o, cat(o_cmp_p, o_cmp_s), cat(o_slc_p, o_slc_s), cat(o_win_p, o_win_s),
               gates_all, eg, wa0, wb0)
    x2 = _ffn(x1, norm0_ffn.reshape(1, d), ffn_w_gate.astype(BF16), ffn_w_up.astype(BF16),
              ffn_w_down.astype(BF16))

    c_q1 = SWA_HEADS * HEAD_DIM
    qw1 = SWA_HEADS * LANES
    w1p = jnp.concatenate([_pad_heads_cols(w_in1[:, :c_q1], SWA_HEADS), w_in1[:, c_q1:]], axis=1).astype(BF16)
    aux1 = jnp.concatenate([_head_gain(swa_q_gain, SWA_HEADS, ATTN_SCALE),
                            jnp.tile(_key_gain(swa_k_gain), g)]).reshape(1, -1)
    segs1 = (("qnorm", 0, qw1), ("kvnorm", qw1, KV_COLS))
    outs1 = ((qw1, F32), (KV_COLS, F32), (KV_COLS, BF16))
    q1_all, swa_all, swa_bf = _proj(x2, norm1_mix.reshape(1, d), w1p, aux1, segs1, outs1, 512, "proj1")
    hg1 = SWA_HEADS // g
    sinks = swa_sinks.astype(F32)
    q1_p = q1_all[:mp].reshape(nb, t, qw1)
    q1_s = q1_all[mp:].reshape(ns, s_len, qw1)
    o1_p = _flash(q1_p, swa_bf[:mp].reshape(nb, t, KV_COLS), hg=hg1, tq=128, window=SWA_WINDOW, chunk=256,
                  sinks=sinks)
    sb_len = cache_swa.shape[1]
    tk_1 = -(-(sb_len + s_len) // 16) * 16
    kv_swa_s = jnp.concatenate([
        cache_swa.astype(BF16).reshape(ns, sb_len, KV_COLS), swa_bf[mp:].reshape(ns, s_len, KV_COLS),
        jnp.zeros((ns, tk_1 - sb_len - s_len, KV_COLS), BF16)], axis=1)
    o1_s = _attn(q1_s, kv_swa_s, hg=hg1, window=SWA_WINDOW, q_off=past, k_off=past - sb_len, sinks=sinks)

    rw = jnp.pad(router_w.astype(F32), ((0, 0), (0, LANES - N_EXPERTS)))
    rwh, rwl = _split_bf16(rw)
    rb = jnp.pad(router_b.astype(F32), (0, LANES - N_EXPERTS)).reshape(1, LANES)
    x3, h3, r = _out1(x2, cat(o1_p, o1_s), _pad_heads_rows(w_out1, SWA_HEADS).astype(BF16),
                      norm1_ffn.reshape(1, d), rwh, rwl, rb)
    m_all = mp + msz
    tm_moe = 512
    gate, dest, row_tok, blk_e, n_used = _route(r, m_all, tm_moe)
    xs = jnp.concatenate([h3, jnp.zeros((1, d), BF16)], axis=0)[row_tok]
    ys = _moe(xs, blk_e, n_used, moe_w_gate.astype(BF16), moe_w_up.astype(BF16), moe_w_down.astype(BF16),
              tm_moe, 512)
    ysel = ys[dest].reshape(m_all, 2, d)
    x4 = x3 + (gate[:, 0:1] * ysel[:, 0] + gate[:, 1:2] * ysel[:, 1])

    y_p = x4[:mp].reshape(nb, t, d)
    y_s = x4[mp:].reshape(ns, s_len, d)
    rows5 = lambda a, n, l: a.reshape(n, l, g, 2, HEAD_DIM)
    keep_w = min(NSA_WINDOW, t)
    keep_s = min(SWA_WINDOW, t)
    pool_p = u_p[:, t - POOL_STATE:]
    cmp_p = rows5(cmp_all[:mp], nb, t)
    slc_p_out = rows5(slc_all[:mp], nb, t)
    win_p_out = rows5(win_all[:mp], nb, t)[:, t - keep_w:]
    swa_p_out = rows5(swa_all[:mp], nb, t)[:, t - keep_s:]
    pool_s = u_ext[:, -POOL_STATE:]
    cmp_s = rows5(cmp_all[mp:], ns, s_len)
    slc_s_out = rows5(slc_all[mp:], ns, s_len)
    win_s_out = jnp.concatenate([cache_nsa_win.astype(F32), rows5(win_all[mp:], ns, s_len)], axis=1)[:, -wb_len:]
    swa_s_out = jnp.concatenate([cache_swa.astype(F32), rows5(swa_all[mp:], ns, s_len)], axis=1)[:, -sb_len:]
    return (y_p, y_s, pool_p, cmp_p, slc_p_out, win_p_out, swa_p_out, pool_s, cmp_s, slc_s_out, win_s_out,
            swa_s_out)
```

```python
import functools

import jax
import jax.numpy as jnp
from jax import lax
from jax.experimental import pallas as pl
from jax.experimental.pallas import tpu as pltpu

F32 = jnp.float32
BF16 = jnp.bfloat16

LANES = 128
MXU_COLS = 256
HEAD_DIM = 64
D_MODEL = 1024
POOL_WINDOWS = (2, 4, 8, 16)
POOL_CH = 512
POOL_GROUP = 128
POOL_STATE = 15
POOL_HALO = 16
NSA_HEADS = 8
SWA_HEADS = 16
KV_GROUPS = 2
KV_COLS = KV_GROUPS * 2 * HEAD_DIM
CMP_BLOCK = 32
CMP_STRIDE = 16
CMP_HIDDEN = 128
SLC_BLOCK = 64
SLC_SHIFT = SLC_BLOCK.bit_length() - 1
SLC_TOPN = 16
NSA_WINDOW = 512
SWA_WINDOW = 128
PAGE_SIZE = 128
N_EXPERTS = 8
RMS_EPS = 1e-6
NEG_INF = -1e30
FORCE_SCORE = 1e9
PAD_SCORE = -3e38
ATTN_SCALE = HEAD_DIM ** -0.5
VMEM_LIMIT = 56 * 1024 * 1024


def _cparams(sem):
    return pltpu.CompilerParams(dimension_semantics=sem, vmem_limit_bytes=VMEM_LIMIT)


def _split_bf16(x):
    hi = x.astype(BF16)
    lo = (x - hi.astype(F32)).astype(BF16)
    return hi, lo


def _dot(a, b):
    return jnp.dot(a, b, preferred_element_type=F32)


def _dot_nt(a, b):
    return lax.dot_general(a, b, (((1,), (1,)), ((), ())), preferred_element_type=F32)


def _proj_kernel(x_ref, g_ref, w_ref, aux_ref, *out_refs, segs):
    x = x_ref[...]
    ms = jnp.mean(x * x, axis=-1, keepdims=True)
    h = (x * lax.rsqrt(ms + RMS_EPS) * g_ref[...]).astype(BF16)
    lane = lax.broadcasted_iota(jnp.int32, (1, LANES), 1)
    is_key = lane < HEAD_DIM
    oi = 0
    for kind, c0, width in segs:
        if kind == "qnorm":
            for pair in range(width // MXU_COLS):
                cp = c0 + pair * MXU_COLS
                z = _dot(h, w_ref[:, cp:cp + MXU_COLS])
                for hh in range(MXU_COLS // LANES):
                    zh = z[:, hh * LANES:(hh + 1) * LANES]
                    msq = jnp.sum(zh * zh, axis=-1, keepdims=True) * (1.0 / HEAD_DIM)
                    lo = cp - c0 + hh * LANES
                    gain = aux_ref[:, cp + hh * LANES:cp + (hh + 1) * LANES]
                    out_refs[oi][:, lo:lo + LANES] = zh * lax.rsqrt(msq + RMS_EPS) * gain
            oi += 1
            continue
        z = _dot(h, w_ref[:, c0:c0 + width])
        aux = aux_ref[:, c0:c0 + width]
        if kind == "raw":
            out_refs[oi][...] = z
            oi += 1
        elif kind == "sigmoid":
            out_refs[oi][...] = jax.nn.sigmoid(z)
            oi += 1
        elif kind == "kvnorm":
            for gg in range(width // LANES):
                sl = slice(gg * LANES, (gg + 1) * LANES)
                zb = z[:, sl]
                msq = jnp.sum(jnp.where(is_key, zb * zb, 0.0), axis=-1, keepdims=True) * (1.0 / HEAD_DIM)
                y = jnp.where(is_key, zb * lax.rsqrt(msq + RMS_EPS) * aux[:, sl], zb)
                out_refs[oi][:, sl] = y
                out_refs[oi + 1][:, sl] = y.astype(BF16)
            oi += 2
        else:
            raise ValueError(kind)


def _proj(x, g, w, aux, segs, out_defs, tm, name):
    m, d = x.shape
    c = w.shape[1]
    out_shape = [jax.ShapeDtypeStruct((m, wd), dt) for wd, dt in out_defs]
    out_specs = [pl.BlockSpec((tm, wd), lambda i: (i, 0)) for wd, _ in out_defs]
    return pl.pallas_call(
        functools.partial(_proj_kernel, segs=segs),
        out_shape=out_shape,
        grid=(m // tm,),
        in_specs=[
            pl.BlockSpec((tm, d), lambda i: (i, 0)),
            pl.BlockSpec((1, d), lambda i: (0, 0)),
            pl.BlockSpec((d, c), lambda i: (0, 0)),
            pl.BlockSpec((1, c), lambda i: (0, 0)),
        ],
        out_specs=out_specs,
        compiler_params=_cparams(("parallel",)),
        name=name,
    )(x, g, w, aux)


def _pool_prompt_kernel(u_ref, halo_ref, pw_ref, ps_ref, o_ref, e_ref, *, tp):
    i = pl.program_id(1)
    e_ref[0:POOL_HALO, :] = jnp.where(i > 0, halo_ref[0], 0.0)
    e_ref[POOL_HALO:, :] = u_ref[0]
    t = i * tp + lax.broadcasted_iota(jnp.int32, (tp, 1), 0)
    for gi, w in enumerate(POOL_WINDOWS):
        sl = slice(gi * POOL_GROUP, (gi + 1) * POOL_GROUP)
        tok = e_ref[POOL_HALO:POOL_HALO + tp, sl]
        s = tok
        for k in range(1, w):
            s = s + e_ref[POOL_HALO - k:POOL_HALO - k + tp, sl]
        cnt = jnp.minimum(t + 1, w).astype(F32)
        dg = s / cnt - tok
        y = _dot(dg.astype(BF16), pw_ref[gi]) * ps_ref[:, sl]
        o_ref[0, :, sl] = y.astype(o_ref.dtype)


def _pool_prompt(u, pool_w, pool_scale, tp=512):
    n, t, c = u.shape
    hb = tp // POOL_HALO
    return pl.pallas_call(
        functools.partial(_pool_prompt_kernel, tp=tp),
        out_shape=jax.ShapeDtypeStruct((n, t, c), BF16),
        grid=(n, t // tp),
        in_specs=[
            pl.BlockSpec((1, tp, c), lambda b, i: (b, i, 0)),
            pl.BlockSpec((1, POOL_HALO, c), lambda b, i: (b, jnp.maximum(i * hb - 1, 0), 0)),
            pl.BlockSpec((len(POOL_WINDOWS), POOL_GROUP, POOL_GROUP), lambda b, i: (0, 0, 0)),
            pl.BlockSpec((1, c), lambda b, i: (0, 0)),
        ],
        out_specs=pl.BlockSpec((1, tp, c), lambda b, i: (b, i, 0)),
        scratch_shapes=[pltpu.VMEM((tp + POOL_HALO, c), F32)],
        compiler_params=_cparams(("parallel", "parallel")),
        name="pool_prompt",
    )(u, u, pool_w, pool_scale)


def _pool_sample_kernel(x_ref, pw_ref, ps_ref, o_ref, *, bn, s_len):
    base = 1 + POOL_STATE
    for gi, w in enumerate(POOL_WINDOWS):
        sl = slice(gi * POOL_GROUP, (gi + 1) * POOL_GROUP)
        tok = x_ref[:, base:base + s_len, sl]
        s = tok
        for k in range(1, w):
            s = s + x_ref[:, base - k:base - k + s_len, sl]
        dg = (s / float(w) - tok).reshape(bn * s_len, POOL_GROUP)
        y = _dot(dg.astype(BF16), pw_ref[gi]) * ps_ref[:, sl]
        o_ref[:, :, sl] = y.reshape(bn, s_len, POOL_GROUP).astype(o_ref.dtype)


def _pool_sample(x_ext, pool_w, pool_scale, s_len, bn=16):
    n, l, c = x_ext.shape
    return pl.pallas_call(
        functools.partial(_pool_sample_kernel, bn=bn, s_len=s_len),
        out_shape=jax.ShapeDtypeStruct((n, s_len, c), F32),
        grid=(n // bn,),
        in_specs=[
            pl.BlockSpec((bn, l, c), lambda b: (b, 0, 0)),
            pl.BlockSpec((len(POOL_WINDOWS), POOL_GROUP, POOL_GROUP), lambda b: (0, 0, 0)),
            pl.BlockSpec((1, c), lambda b: (0, 0)),
        ],
        out_specs=pl.BlockSpec((bn, s_len, c), lambda b: (b, 0, 0)),
        compiler_params=_cparams(("parallel",)),
        name="pool_sample",
    )(x_ext, pool_w, pool_scale)


def _compress1_kernel(x_ref, w_ref, pa_ref, pb_ref):
    acc = None
    for j in range(CMP_STRIDE):
        xj = x_ref[:, j * KV_COLS:(j + 1) * KV_COLS].astype(BF16)
        d = _dot(xj, w_ref[j])
        acc = d if acc is None else acc + d
    half = 4 * CMP_HIDDEN
    pa_ref[...] = acc[:, :half]
    pb_ref[...] = acc[:, half:]


def _compress1(x, w_bd, ts):
    r, c = x.shape
    ts = min(ts, r)
    half = 4 * CMP_HIDDEN
    return pl.pallas_call(
        _compress1_kernel,
        out_shape=[jax.ShapeDtypeStruct((r, half), F32), jax.ShapeDtypeStruct((r, half), F32)],
        grid=(r // ts,),
        in_specs=[
            pl.BlockSpec((ts, c), lambda i: (i, 0)),
            pl.BlockSpec(w_bd.shape, lambda i: (0, 0, 0)),
        ],
        out_specs=[pl.BlockSpec((ts, half), lambda i: (i, 0)), pl.BlockSpec((ts, half), lambda i: (i, 0))],
        compiler_params=_cparams(("parallel",)),
        name="compress1",
    )(x, w_bd)


def _gelu_tanh(x):
    return 0.5 * x * (1.0 + jnp.tanh(0.7978845608028654 * (x + 0.044715 * (x * x * x))))


def _compress_tail(pa, pb, pb_new, pek_ref, pev_ref, w1k_ref, w1v_ref, w2_ref, kg_ref, o_ref):
    s = pa.shape[0]
    row = lax.broadcasted_iota(jnp.int32, (s, 1), 0)
    pb_next = jnp.where(row == s - 1, pb_new, pltpu.roll(pb, s - 1, 0))
    hk = _dot(pek_ref[...], w1k_ref[...])[0:1, :]
    hv = _dot(pev_ref[...], w1v_ref[...])[0:1, :]
    pe_h = jnp.concatenate([hk, hv, hk, hv], axis=1)
    a = _gelu_tanh(pa + pb_next + pe_h)
    o = _dot(a.astype(BF16), w2_ref[...])
    lane = lax.broadcasted_iota(jnp.int32, (1, LANES), 1)
    is_key = lane < HEAD_DIM
    for gg in range(KV_GROUPS):
        sl = slice(gg * LANES, (gg + 1) * LANES)
        ob = o[:, sl]
        msq = jnp.sum(jnp.where(is_key, ob * ob, 0.0), axis=-1, keepdims=True) * (1.0 / HEAD_DIM)
        y = jnp.where(is_key, ob * lax.rsqrt(msq + RMS_EPS) * kg_ref[...], ob)
        o_ref[0, :, sl] = y.astype(o_ref.dtype)


def _compress2_kernel(pa_ref, pb_ref, pbn_ref, pek_ref, pev_ref, w1k_ref, w1v_ref, w2_ref, kg_ref, o_ref):
    _compress_tail(pa_ref[0], pb_ref[0], pbn_ref[0][0:1, :], pek_ref, pev_ref, w1k_ref, w1v_ref, w2_ref,
                   kg_ref, o_ref)


def _pages_t(cache):
    return jnp.transpose(cache, (0, 2, 3, 4, 1)).reshape(cache.shape[0], KV_COLS, cache.shape[1])


def _paged_prefetch(pt_ref, cache_ref, dst, sem, n_pages):
    b = pl.program_id(0)
    slot = b % 2

    def fetch(bb, sl):
        def body(p, c):
            pltpu.make_async_copy(cache_ref.at[pt_ref[bb * n_pages + p]], dst(sl, p), sem.at[sl]).start()
            return c

        lax.fori_loop(0, n_pages, body, 0)

    @pl.when(b == 0)
    def _():
        fetch(0, 0)

    @pl.when(b + 1 < pl.num_programs(0))
    def _():
        fetch(b + 1, 1 - slot)

    def wait(p, c):
        pltpu.make_async_copy(cache_ref.at[0], dst(slot, p), sem.at[slot]).wait()
        return c

    lax.fori_loop(0, n_pages, wait, 0)
    return slot


def _paged_compress_kernel(pt_ref, cache_ref, pbn_ref, w1_ref, pek_ref, pev_ref, w1k_ref, w1v_ref, w2_ref, kg_ref,
                           o_ref, buf, rows_sc, sem, *, n_pages):
    slot = _paged_prefetch(
        pt_ref, cache_ref, lambda sl, p: buf.at[sl, pl.ds(pl.multiple_of(p * KV_COLS, KV_COLS), KV_COLS)],
        sem, n_pages)
    for p in range(n_pages):
        for gg in range(KV_GROUPS):
            f0 = p * KV_COLS + gg * LANES
            rows_sc[gg, p * PAGE_SIZE:(p + 1) * PAGE_SIZE, :] = buf[slot, f0:f0 + LANES, :].T
    n_seg = n_pages * (PAGE_SIZE // CMP_STRIDE)
    hid2 = 2 * CMP_HIDDEN
    pas, pbs = [], []
    for gg in range(KV_GROUPS):
        xg = jnp.concatenate(
            [rows_sc[gg, pl.ds(j, n_seg, stride=CMP_STRIDE), :].astype(BF16)
             for j in range(CMP_STRIDE)], axis=1)
        pg = _dot(xg, w1_ref[...])
        pas.append(pg[:, :hid2])
        pbs.append(pg[:, hid2:])
    _compress_tail(jnp.concatenate(pas, axis=1), jnp.concatenate(pbs, axis=1), pbn_ref[0][0:1, :],
                   pek_ref, pev_ref, w1k_ref, w1v_ref, w2_ref, kg_ref, o_ref)


def _paged_compress(page_table, cache_t, pb_new, w1_gs, pek, pev, w1k, w1v, w2_bd, kg):
    ns, n_pages = page_table.shape
    s = n_pages * (PAGE_SIZE // CMP_STRIDE)
    c = pb_new.shape[-1]
    full2 = lambda b, pt: (0, 0)
    grid_spec = pltpu.PrefetchScalarGridSpec(
        num_scalar_prefetch=1,
        grid=(ns,),
        in_specs=[
            pl.BlockSpec(memory_space=pl.ANY),
            pl.BlockSpec((1, 8, c), lambda b, pt: (b, 0, 0)),
            pl.BlockSpec(w1_gs.shape, full2),
            pl.BlockSpec(pek.shape, full2),
            pl.BlockSpec(pev.shape, full2),
            pl.BlockSpec(w1k.shape, full2),
            pl.BlockSpec(w1v.shape, full2),
            pl.BlockSpec(w2_bd.shape, full2),
            pl.BlockSpec(kg.shape, full2),
        ],
        out_specs=pl.BlockSpec((1, s, KV_COLS), lambda b, pt: (b, 0, 0)),
        scratch_shapes=[pltpu.VMEM((2, n_pages * KV_COLS, PAGE_SIZE), F32),
                        pltpu.VMEM((KV_GROUPS, n_pages * PAGE_SIZE, LANES), F32), pltpu.SemaphoreType.DMA((2,))],
    )
    return pl.pallas_call(
        functools.partial(_paged_compress_kernel, n_pages=n_pages),
        out_shape=jax.ShapeDtypeStruct((ns, s, KV_COLS), BF16),
        grid_spec=grid_spec,
        compiler_params=_cparams(("arbitrary",)),
        name="paged_compress",
    )(page_table.reshape(-1), cache_t, pb_new, w1_gs, pek, pev, w1k, w1v, w2_bd, kg)


def _compress2(pa, pb, pb_new, pek, pev, w1k, w1v, w2_bd, kg):
    n, s, c = pa.shape
    full2 = lambda b: (0, 0)
    return pl.pallas_call(
        _compress2_kernel,
        out_shape=jax.ShapeDtypeStruct((n, s, KV_COLS), BF16),
        grid=(n,),
        in_specs=[
            pl.BlockSpec((1, s, c), lambda b: (b, 0, 0)),
            pl.BlockSpec((1, s, c), lambda b: (b, 0, 0)),
            pl.BlockSpec((1, 8, c), lambda b: (b, 0, 0)),
            pl.BlockSpec(pek.shape, full2),
            pl.BlockSpec(pev.shape, full2),
            pl.BlockSpec(w1k.shape, full2),
            pl.BlockSpec(w1v.shape, full2),
            pl.BlockSpec(w2_bd.shape, full2),
            pl.BlockSpec(kg.shape, full2),
        ],
        out_specs=pl.BlockSpec((1, s, KV_COLS), lambda b: (b, 0, 0)),
        compiler_params=_cparams(("parallel",)),
        name="compress2",
    )(pa, pb, pb_new, pek, pev, w1k, w1v, w2_bd, kg)


def _stack_heads(q, hg):
    return jnp.concatenate([q[:, h * LANES:(h + 1) * LANES] for h in range(hg)], axis=0).astype(BF16)


def _row_qpos(hg, tq, q0):
    r = lax.broadcasted_iota(jnp.int32, (hg * tq, 1), 0)
    return q0 + (r & (tq - 1))


def _store_heads(o_ref, lead, o, hg, tq):
    lane = lax.broadcasted_iota(jnp.int32, (1, LANES), 1)
    for h in range(hg):
        blk = jnp.where(lane >= HEAD_DIM, o[h * tq:(h + 1) * tq], 0.0)
        o_ref[lead + (slice(None), slice(h * LANES, (h + 1) * LANES))] = blk.astype(o_ref.dtype)


def _cmp_attn_kernel(q_ref, kc_ref, cov_ref, o_ref, sel_ref, *, bn, hg, tq, q_off, n_blocks, row_off, as_bias):
    i = pl.program_id(2)
    s_len = kc_ref.shape[1]
    q0 = q_off + i * tq
    qpos = _row_qpos(hg, tq, q0)
    blk_end = lax.broadcasted_iota(jnp.int32, (1, s_len), 1) * CMP_STRIDE + (CMP_BLOCK - 1)
    cond = blk_end <= qpos
    psums = []
    for b in range(bn):
        qs = _stack_heads(q_ref[b], hg)
        kc = kc_ref[b]
        s = jnp.where(cond, _dot_nt(qs, kc), NEG_INF)
        m = jnp.max(s, axis=-1, keepdims=True)
        p = jnp.where(cond, jnp.exp(s - m), 0.0)
        den = jnp.sum(p, axis=-1, keepdims=True)
        p = p / jnp.maximum(den, 1e-30)
        o = _dot(p.astype(BF16), kc)
        _store_heads(o_ref, (b,), o, hg, tq)
        ps = p[0:tq]
        for h in range(1, hg):
            ps = ps + p[h * tq:(h + 1) * tq]
        psums.append(ps)
    ps = psums[0] if bn == 1 else jnp.concatenate(psums, axis=0)
    hi, lo = _split_bf16(ps)
    cov = cov_ref[...]
    score = _dot_nt(cov, hi) + _dot_nt(cov, lo)
    nb_pad, nl = score.shape
    j = lax.broadcasted_iota(jnp.int32, (nb_pad, nl), 0) - row_off
    t = q0 + (lax.broadcasted_iota(jnp.int32, (1, nl), 1) & (tq - 1))
    cur = t >> SLC_SHIFT
    forced = (j == 0) | (j == cur) | (j == cur - 1)
    valid = j * SLC_BLOCK <= t
    score = jnp.where(valid, jnp.where(forced, FORCE_SCORE, score), NEG_INF)
    is_block = lax.bitcast_convert_type(j, jnp.uint32) < jnp.uint32(n_blocks)
    score = jnp.where(is_block, score, PAD_SCORE)
    sel = jnp.zeros((nb_pad, nl), F32)
    for _ in range(min(SLC_TOPN, n_blocks)):
        mx = jnp.max(score, axis=0, keepdims=True)
        jm = jnp.min(jnp.where(score == mx, j, nb_pad), axis=0, keepdims=True)
        pick = j == jm
        sel = jnp.where(pick, 1.0, sel)
        score = jnp.where(pick, -jnp.inf, score)
    if as_bias:
        sel = jnp.where(is_block, (sel - 1.0) * (-NEG_INF), 0.0)
    sel_t = sel.T
    for b in range(bn):
        sel_ref[b, 0] = sel_t[b * tq:(b + 1) * tq]


def _cmp_attn(q, kc, cov_t, *, hg, tq, bn, q_off, n_blocks, out_dtype, row_off, as_bias):
    n, t, hc = q.shape
    s_len = kc.shape[1]
    nb_pad = cov_t.shape[0]
    g = KV_GROUPS
    return pl.pallas_call(
        functools.partial(_cmp_attn_kernel, bn=bn, hg=hg, tq=tq, q_off=q_off, n_blocks=n_blocks,
                          row_off=row_off, as_bias=as_bias),
        out_shape=[jax.ShapeDtypeStruct((n, t, hc), out_dtype), jax.ShapeDtypeStruct((n, g, t, nb_pad), F32)],
        grid=(n // bn, g, t // tq),
        in_specs=[
            pl.BlockSpec((bn, tq, hg * LANES), lambda b, gg, i: (b, i, gg)),
            pl.BlockSpec((bn, s_len, LANES), lambda b, gg, i: (b, 0, gg)),
            pl.BlockSpec((nb_pad, s_len), lambda b, gg, i: (0, 0)),
        ],
        out_specs=[
            pl.BlockSpec((bn, tq, hg * LANES), lambda b, gg, i: (b, i, gg)),
            pl.BlockSpec((bn, 1, tq, nb_pad), lambda b, gg, i: (b, gg, i, 0)),
        ],
        compiler_params=_cparams(("parallel", "parallel", "parallel")),
        name="cmp_attn",
    )(q, kc, cov_t)


def _flash_kernel(*refs, hg, tq, tk, window, use_sel, use_sink, chunk):
    refs = list(refs)
    sink_ref = refs.pop(0) if use_sink else None
    q_ref = refs.pop(0)
    kv_ref = refs.pop(0)
    selb_ref = refs.pop(0) if use_sel else None
    o_ref, qs_sc, m_sc, acc_sc = refs
    g = pl.program_id(1)
    i = pl.program_id(2)
    rows = hg * tq
    q0 = i * tq
    for h in range(hg):
        qh = q_ref[0, :, h * LANES:(h + 1) * LANES]
        if use_sel:
            qh = qh + selb_ref[0, 0]
        qs_sc[h * tq:(h + 1) * tq, :] = qh.astype(BF16)
    m_sc[...] = jnp.full((rows, LANES), NEG_INF, F32)
    acc_sc[...] = jnp.zeros((rows, LANES), F32)

    def tile(k0, masked):
        kv = kv_ref[0, pl.ds(k0, tk), :]
        klane = lax.broadcasted_iota(jnp.int32, (tk, LANES), 1)
        if use_sel:
            krow = lax.broadcasted_iota(jnp.int32, (tk, LANES), 0)
            onehot = jnp.where(klane - HEAD_DIM == ((k0 + krow) >> SLC_SHIFT), 1.0, 0.0).astype(BF16)
            kaug = jnp.where(klane < HEAD_DIM, kv, onehot)
        else:
            kaug = kv
        vaug = jnp.where(klane < HEAD_DIM, jnp.ones_like(kv), kv)
        for c in range(rows // chunk):
            r0 = c * chunk
            s = _dot_nt(qs_sc[r0:r0 + chunk, :], kaug)
            if masked:
                qpos = q0 + ((r0 + lax.broadcasted_iota(jnp.int32, (chunk, 1), 0)) & (tq - 1))
                d = qpos - (k0 + lax.broadcasted_iota(jnp.int32, (1, tk), 1))
                if window is None:
                    s = jnp.where(d >= 0, s, NEG_INF)
                else:
                    s = jnp.where(lax.bitcast_convert_type(d, jnp.uint32) <= jnp.uint32(window), s, NEG_INF)
            m_prev = m_sc[r0:r0 + chunk, :]
            m_new = jnp.maximum(m_prev, jnp.max(s, axis=-1, keepdims=True))
            alpha = jnp.exp(m_prev - m_new)
            p = jnp.exp(s - jnp.concatenate([m_new] * (tk // LANES), axis=1))
            acc_sc[r0:r0 + chunk, :] = alpha * acc_sc[r0:r0 + chunk, :] + _dot(p.astype(BF16), vaug)
            m_sc[r0:r0 + chunk, :] = m_new

    if window is None:
        def body(jt, c):
            tile(pl.multiple_of(jt * tk, tk), False)
            return c

        lax.fori_loop(0, i, body, 0)
        tile(pl.multiple_of(i * tk, tk), True)
    else:
        tile(pl.multiple_of(jnp.maximum(q0 - window, 0), LANES), True)

    acc = acc_sc[...]
    den = pltpu.roll(acc, HEAD_DIM, 1)
    m = m_sc[...]
    if use_sink:
        sink = jnp.concatenate([jnp.full((tq, LANES), sink_ref[g * hg + h], F32) for h in range(hg)], axis=0)
        m_fin = jnp.maximum(m, sink)
        w = jnp.exp(m - m_fin)
        o = acc * w / jnp.maximum(den * w + jnp.exp(sink - m_fin), 1e-30)
    else:
        o = jnp.where(m > 0.5 * NEG_INF, acc / jnp.maximum(den, 1e-30), 0.0)
    lane = lax.broadcasted_iota(jnp.int32, (1, LANES), 1)
    o = jnp.where(lane >= HEAD_DIM, o, 0.0)
    for h in range(hg):
        o_ref[0, :, h * LANES:(h + 1) * LANES] = o[h * tq:(h + 1) * tq].astype(o_ref.dtype)


def _flash(q, kv, *, hg, tq, window, chunk, selb=None, sinks=None):
    n, t, hc = q.shape
    g = KV_GROUPS
    use_sel = selb is not None
    use_sink = sinks is not None
    rows = hg * tq
    tk = tq if window is None else tq + window
    if use_sel:
        assert t // SLC_BLOCK <= LANES - HEAD_DIM, "selection bias needs one upper query lane per block"
    in_specs = []
    args = []
    if use_sink:
        in_specs.append(pl.BlockSpec(memory_space=pltpu.SMEM))
        args.append(sinks)
    in_specs += [
        pl.BlockSpec((1, tq, hg * LANES), lambda b, gg, i: (b, i, gg)),
        pl.BlockSpec((1, t, LANES), lambda b, gg, i: (b, 0, gg)),
    ]
    args += [q, kv]
    if use_sel:
        in_specs.append(pl.BlockSpec((1, 1, tq, LANES), lambda b, gg, i: (b, gg, i, 0)))
        args.append(selb)
    return pl.pallas_call(
        functools.partial(_flash_kernel, hg=hg, tq=tq, tk=tk, window=window, use_sel=use_sel,
                          use_sink=use_sink, chunk=chunk),
        out_shape=jax.ShapeDtypeStruct((n, t, hc), BF16),
        grid=(n, g, t // tq),
        in_specs=in_specs,
        out_specs=pl.BlockSpec((1, tq, hg * LANES), lambda b, gg, i: (b, i, gg)),
        scratch_shapes=[pltpu.VMEM((rows, LANES), BF16), pltpu.VMEM((rows, LANES), F32),
                        pltpu.VMEM((rows, LANES), F32)],
        compiler_params=_cparams(("parallel", "parallel", "parallel")),
        name="flash",
    )(*args)


def _attn_kernel(*refs, hg, tq, window, q_off, k_off, use_sink):
    refs = list(refs)
    sink_ref = refs.pop(0) if use_sink else None
    q_ref = refs.pop(0)
    kv_ref = refs.pop(0)
    (o_ref,) = refs
    g = pl.program_id(1)
    qs = _stack_heads(q_ref[0], hg)
    qpos = _row_qpos(hg, tq, q_off)
    kv = kv_ref[0]
    tk = kv.shape[0]
    s = _dot_nt(qs, kv)
    d = qpos - (k_off + lax.broadcasted_iota(jnp.int32, (1, tk), 1))
    if window is None:
        s = jnp.where(d >= 0, s, NEG_INF)
    else:
        s = jnp.where(lax.bitcast_convert_type(d, jnp.uint32) <= jnp.uint32(window), s, NEG_INF)
    m = jnp.max(s, axis=-1, keepdims=True)
    if use_sink:
        sink = jnp.concatenate([jnp.full((tq, 1), sink_ref[g * hg + h], F32) for h in range(hg)], axis=0)
        m = jnp.maximum(m, sink)
    p = jnp.exp(s - m)
    den = jnp.sum(p, axis=-1, keepdims=True)
    if use_sink:
        den = den + jnp.exp(sink - m)
    o = jnp.where(m > 0.5 * NEG_INF, _dot(p.astype(BF16), kv) / jnp.maximum(den, 1e-30), 0.0)
    _store_heads(o_ref, (0,), o, hg, tq)


def _attn(q, kv, *, hg, window, q_off, k_off, sinks=None):
    n, tq, hc = q.shape
    tkv = kv.shape[1]
    g = KV_GROUPS
    use_sink = sinks is not None
    in_specs = []
    args = []
    if use_sink:
        in_specs.append(pl.BlockSpec(memory_space=pltpu.SMEM))
        args.append(sinks)
    in_specs += [
        pl.BlockSpec((1, tq, hg * LANES), lambda b, gg: (b, 0, gg)),
        pl.BlockSpec((1, tkv, LANES), lambda b, gg: (b, 0, gg)),
    ]
    args += [q, kv]
    return pl.pallas_call(
        functools.partial(_attn_kernel, hg=hg, tq=tq, window=window, q_off=q_off, k_off=k_off,
                          use_sink=use_sink),
        out_shape=jax.ShapeDtypeStruct((n, tq, hc), F32),
        grid=(n, g),
        in_specs=in_specs,
        out_specs=pl.BlockSpec((1, tq, hg * LANES), lambda b, gg: (b, 0, gg)),
        compiler_params=_cparams(("parallel", "parallel")),
        name="attn",
    )(*args)


def _paged_slc_kernel(pt_ref, cache_ref, q_ref, new_ref, sel_ref, e_ref, o_ref, buf, sem, *, n_pages, hg, tq,
                      q_off):
    past = n_pages * PAGE_SIZE
    tk = buf.shape[2]
    slot = _paged_prefetch(
        pt_ref, cache_ref, lambda sl, p: buf.at[sl, :, pl.ds(pl.multiple_of(p * PAGE_SIZE, PAGE_SIZE), PAGE_SIZE)],
        sem, n_pages)
    buf[slot, :, past:] = new_ref[0]
    kv_t = buf[slot].astype(BF16)
    rows = hg * tq
    zero = jnp.zeros((rows, LANES), F32)
    q = q_ref[0]
    stack = lambda g: jnp.concatenate([q[:, (g * hg + h) * LANES:(g * hg + h + 1) * LANES] for h in range(hg)],
                                      axis=0)
    q2 = jnp.concatenate([jnp.concatenate([stack(0), zero], axis=1),
                          jnp.concatenate([zero, stack(1)], axis=1)], axis=0).astype(BF16)
    s = _dot(q2, kv_t)
    r = lax.broadcasted_iota(jnp.int32, (2 * rows, 1), 0)
    qpos = q_off + (r & (tq - 1))
    s = jnp.where(qpos - lax.broadcasted_iota(jnp.int32, (1, tk), 1) >= 0, s, NEG_INF)
    sel = jnp.concatenate([sel_ref[0, 0]] * hg + [sel_ref[0, 1]] * hg, axis=0).astype(BF16)
    s = jnp.where(_dot(sel, e_ref[...]) > 0.5, s, NEG_INF)
    m = jnp.max(s, axis=-1, keepdims=True)
    p = jnp.exp(s - m)
    den = jnp.sum(p, axis=-1, keepdims=True)
    o = jnp.where(m > 0.5 * NEG_INF, _dot_nt(p.astype(BF16), kv_t) / jnp.maximum(den, 1e-30), 0.0)
    lane = lax.broadcasted_iota(jnp.int32, (1, LANES), 1)
    for g in range(KV_GROUPS):
        for h in range(hg):
            blk = o[g * rows + h * tq:g * rows + (h + 1) * tq, g * LANES:(g + 1) * LANES]
            c0 = (g * hg + h) * LANES
            o_ref[0, :, c0:c0 + LANES] = jnp.where(lane >= HEAD_DIM, blk, 0.0)


def _paged_slc(page_table, cache_t, q, new_t, sel, expand, *, hg, q_off):
    ns, n_pages = page_table.shape
    _, tq, hc = q.shape
    tk = expand.shape[1]
    assert tk == (n_pages + 1) * PAGE_SIZE and new_t.shape == (ns, KV_COLS, PAGE_SIZE)
    grid_spec = pltpu.PrefetchScalarGridSpec(
        num_scalar_prefetch=1,
        grid=(ns,),
        in_specs=[
            pl.BlockSpec(memory_space=pl.ANY),
            pl.BlockSpec((1, tq, hc), lambda b, pt: (b, 0, 0)),
            pl.BlockSpec((1, KV_COLS, PAGE_SIZE), lambda b, pt: (b, 0, 0)),
            pl.BlockSpec((1, KV_GROUPS, tq, sel.shape[-1]), lambda b, pt: (b, 0, 0, 0)),
            pl.BlockSpec(expand.shape, lambda b, pt: (0, 0)),
        ],
        out_specs=pl.BlockSpec((1, tq, hc), lambda b, pt: (b, 0, 0)),
        scratch_shapes=[pltpu.VMEM((2, KV_COLS, tk), F32), pltpu.SemaphoreType.DMA((2,))],
    )
    return pl.pallas_call(
        functools.partial(_paged_slc_kernel, n_pages=n_pages, hg=hg, tq=tq, q_off=q_off),
        out_shape=jax.ShapeDtypeStruct((ns, tq, hc), F32),
        grid_spec=grid_spec,
        compiler_params=_cparams(("arbitrary",)),
        name="paged_slc",
    )(page_table.reshape(-1), cache_t, q, new_t, sel, expand)


def _out0_kernel(x_ref, pool_ref, oc_ref, os_ref, ow_ref, gt_ref, eg_ref, wa_ref, wb_ref, o_ref):
    ghi, glo = _split_bf16(gt_ref[...])
    o = None
    for b, br in enumerate((oc_ref, os_ref, ow_ref)):
        gate = _dot(ghi, eg_ref[b]) + _dot(glo, eg_ref[b])
        term = gate * br[...].astype(F32)
        o = term if o is None else o + term
    mix = _dot(pool_ref[...], wa_ref[...]) + _dot(o.astype(BF16), wb_ref[...])
    o_ref[...] = x_ref[...] + mix


def _out0(x, pool_o, o_cmp, o_slc, o_win, gates, eg, wa, wb, tm=512):
    m, d = x.shape
    hc = o_cmp.shape[1]
    row = lambda w: pl.BlockSpec((tm, w), lambda i: (i, 0))
    return pl.pallas_call(
        _out0_kernel,
        out_shape=jax.ShapeDtypeStruct((m, d), F32),
        grid=(m // tm,),
        in_specs=[row(d), row(POOL_CH), row(hc), row(hc), row(hc), row(LANES),
                  pl.BlockSpec(eg.shape, lambda i: (0, 0, 0)),
                  pl.BlockSpec(wa.shape, lambda i: (0, 0)),
                  pl.BlockSpec(wb.shape, lambda i: (0, 0))],
        out_specs=row(d),
        compiler_params=_cparams(("parallel",)),
        name="out0",
    )(x, pool_o, o_cmp, o_slc, o_win, gates, eg, wa, wb)


def _ffn_kernel(x_ref, g_ref, wg_ref, wu_ref, wd_ref, o_ref, h_sc, acc_sc):
    f = pl.program_id(1)

    @pl.when(f == 0)
    def _():
        x = x_ref[...]
        ms = jnp.mean(x * x, axis=-1, keepdims=True)
        h_sc[...] = (x * lax.rsqrt(ms + RMS_EPS) * g_ref[...]).astype(BF16)
        acc_sc[...] = x

    h = h_sc[...]
    a = _dot(h, wg_ref[...])
    u = _dot(h, wu_ref[...])
    act = (a * jax.nn.sigmoid(a) * u).astype(BF16)
    acc_sc[...] += _dot(act, wd_ref[...])

    @pl.when(f == pl.num_programs(1) - 1)
    def _():
        o_ref[...] = acc_sc[...]


def _ffn(x, g, wg, wu, wd, tm=512, nf=2):
    m, d = x.shape
    fdim = wg.shape[1]
    tf = fdim // nf
    return pl.pallas_call(
        _ffn_kernel,
        out_shape=jax.ShapeDtypeStruct((m, d), F32),
        grid=(m // tm, nf),
        in_specs=[
            pl.BlockSpec((tm, d), lambda i, f: (i, 0)),
            pl.BlockSpec((1, d), lambda i, f: (0, 0)),
            pl.BlockSpec((d, tf), lambda i, f: (0, f)),
            pl.BlockSpec((d, tf), lambda i, f: (0, f)),
            pl.BlockSpec((tf, d), lambda i, f: (f, 0)),
        ],
        out_specs=pl.BlockSpec((tm, d), lambda i, f: (i, 0)),
        scratch_shapes=[pltpu.VMEM((tm, d), BF16), pltpu.VMEM((tm, d), F32)],
        compiler_params=_cparams(("parallel", "arbitrary")),
        name="ffn",
    )(x, g, wg, wu, wd)


def _out1_kernel(x_ref, o_ref_in, w_ref, g_ref, rwh_ref, rwl_ref, rb_ref, xo_ref, h_ref, r_ref):
    x = x_ref[...] + _dot(o_ref_in[...], w_ref[...])
    xo_ref[...] = x
    ms = jnp.mean(x * x, axis=-1, keepdims=True)
    h = x * lax.rsqrt(ms + RMS_EPS) * g_ref[...]
    h_ref[...] = h
    hi, lo = _split_bf16(h)
    logits = _dot(hi, rwh_ref[...]) + _dot(lo, rwh_ref[...]) + _dot(hi, rwl_ref[...]) + rb_ref[...]
    lane = lax.broadcasted_iota(jnp.int32, logits.shape, 1)
    logits = jnp.where(lane < N_EXPERTS, logits, -jnp.inf)
    m1 = jnp.max(logits, axis=-1, keepdims=True)
    i1 = jnp.min(jnp.where(logits == m1, lane, LANES), axis=-1, keepdims=True)
    rest = jnp.where(lane == i1, -jnp.inf, logits)
    m2 = jnp.max(rest, axis=-1, keepdims=True)
    i2 = jnp.min(jnp.where(rest == m2, lane, LANES), axis=-1, keepdims=True)
    e2 = jnp.exp(m2 - m1)
    g1 = 1.0 / (1.0 + e2)
    g2 = e2 / (1.0 + e2)
    r = jnp.where(lane == 0, i1.astype(F32), 0.0)
    r = jnp.where(lane == 1, i2.astype(F32), r)
    r = jnp.where(lane == 2, g1, r)
    r = jnp.where(lane == 3, g2, r)
    r_ref[...] = r


def _out1(x, o, w, g, rwh, rwl, rb, tm=512):
    m, d = x.shape
    hc = o.shape[1]
    row = lambda wd: pl.BlockSpec((tm, wd), lambda i: (i, 0))
    full = lambda a: pl.BlockSpec(a.shape, lambda i: (0, 0))
    return pl.pallas_call(
        _out1_kernel,
        out_shape=[jax.ShapeDtypeStruct((m, d), F32), jax.ShapeDtypeStruct((m, d), F32),
                   jax.ShapeDtypeStruct((m, LANES), F32)],
        grid=(m // tm,),
        in_specs=[row(d), row(hc), full(w), full(g), full(rwh), full(rwl), full(rb)],
        out_specs=[row(d), row(d), row(LANES)],
        compiler_params=_cparams(("parallel",)),
        name="out1",
    )(x, o, w, g, rwh, rwl, rb)


def _moe_kernel(be_ref, nb_ref, x_ref, wg_ref, wu_ref, wd_ref, o_ref, acc_sc):
    b = pl.program_id(0)
    f = pl.program_id(1)
    used = b < nb_ref[0]

    @pl.when(f == 0)
    def _():
        acc_sc[...] = jnp.zeros_like(acc_sc)

    @pl.when(used)
    def _():
        x = x_ref[...]
        a = _dot(x, wg_ref[0])
        u = _dot(x, wu_ref[0])
        act = (a * jax.nn.sigmoid(a) * u).astype(BF16)
        acc_sc[...] += _dot(act, wd_ref[0])

    @pl.when(f == pl.num_programs(1) - 1)
    def _():
        o_ref[...] = acc_sc[...]


def _moe(xs, blk_e, n_used, wg, wu, wd, tm, tf):
    cap, d = xs.shape
    fdim = wg.shape[2]
    grid_spec = pltpu.PrefetchScalarGridSpec(
        num_scalar_prefetch=2,
        grid=(cap // tm, fdim // tf),
        in_specs=[
            pl.BlockSpec((tm, d), lambda b, f, be, nb: (b, 0)),
            pl.BlockSpec((1, d, tf), lambda b, f, be, nb: (be[b], 0, f)),
            pl.BlockSpec((1, d, tf), lambda b, f, be, nb: (be[b], 0, f)),
            pl.BlockSpec((1, tf, d), lambda b, f, be, nb: (be[b], f, 0)),
        ],
        out_specs=pl.BlockSpec((tm, d), lambda b, f, be, nb: (b, 0)),
        scratch_shapes=[pltpu.VMEM((tm, d), F32)],
    )
    return pl.pallas_call(
        _moe_kernel,
        out_shape=jax.ShapeDtypeStruct((cap, d), F32),
        grid_spec=grid_spec,
        compiler_params=_cparams(("parallel", "arbitrary")),
        name="moe",
    )(blk_e, n_used, xs, wg, wu, wd)


def _start_rows(idx_ref, src_ref, dst, sem, base, n_rows):
    def start(r, c):
        pltpu.make_async_copy(src_ref.at[pl.ds(idx_ref[base + r], 1)], dst.at[pl.ds(r, 1)], sem).start()
        return c

    lax.fori_loop(0, n_rows, start, 0, unroll=8)


def _wait_rows(src_ref, dst, sem, n_rows):
    def wait(r, c):
        pltpu.make_async_copy(src_ref.at[pl.ds(0, 1)], dst.at[pl.ds(r, 1)], sem).wait()
        return c

    lax.fori_loop(0, n_rows, wait, 0, unroll=8)


def _dispatch_kernel(idx_ref, h_ref, o_ref, rows_sc, sem, *, tm):
    _start_rows(idx_ref, h_ref, rows_sc, sem.at[0], pl.program_id(0) * tm, tm)
    _wait_rows(h_ref, rows_sc, sem.at[0], tm)
    o_ref[...] = rows_sc[...].astype(o_ref.dtype)


def _dispatch(row_tok, h, tm):
    cap = row_tok.shape[0]
    d = h.shape[1]
    grid_spec = pltpu.PrefetchScalarGridSpec(
        num_scalar_prefetch=1,
        grid=(cap // tm,),
        in_specs=[pl.BlockSpec(memory_space=pl.ANY)],
        out_specs=pl.BlockSpec((tm, d), lambda b, idx: (b, 0)),
        scratch_shapes=[pltpu.VMEM((tm, d), h.dtype), pltpu.SemaphoreType.DMA((1,))],
    )
    return pl.pallas_call(
        functools.partial(_dispatch_kernel, tm=tm),
        out_shape=jax.ShapeDtypeStruct((cap, d), BF16),
        grid_spec=grid_spec,
        compiler_params=_cparams(("arbitrary",)),
        name="moe_dispatch",
    )(row_tok, h)


def _combine_kernel(dest_ref, x_ref, r_ref, ys_ref, o_ref, rows_sc, sem, *, tm, n_tok):
    t0 = pl.program_id(0) * tm
    for k in range(2):
        _start_rows(dest_ref, ys_ref, rows_sc.at[k], sem.at[0], k * n_tok + t0, tm)
    for k in range(2):
        _wait_rows(ys_ref, rows_sc.at[k], sem.at[0], tm)
    r = r_ref[...]
    o_ref[...] = x_ref[...] + (r[:, 2:3] * rows_sc[0] + r[:, 3:4] * rows_sc[1])


def _combine(dest, x, r, ys, tm=512):
    m, d = x.shape
    grid_spec = pltpu.PrefetchScalarGridSpec(
        num_scalar_prefetch=1,
        grid=(m // tm,),
        in_specs=[
            pl.BlockSpec((tm, d), lambda b, idx: (b, 0)),
            pl.BlockSpec((tm, LANES), lambda b, idx: (b, 0)),
            pl.BlockSpec(memory_space=pl.ANY),
        ],
        out_specs=pl.BlockSpec((tm, d), lambda b, idx: (b, 0)),
        scratch_shapes=[pltpu.VMEM((2, tm, d), F32), pltpu.SemaphoreType.DMA((1,))],
    )
    return pl.pallas_call(
        functools.partial(_combine_kernel, tm=tm, n_tok=m),
        out_shape=jax.ShapeDtypeStruct((m, d), F32),
        grid_spec=grid_spec,
        compiler_params=_cparams(("arbitrary",)),
        name="moe_combine",
    )(dest, x, r, ys)


def _pad_heads_cols(w, heads):
    d = w.shape[0]
    w = w.reshape(d, heads, HEAD_DIM)
    return jnp.pad(w, ((0, 0), (0, 0), (0, LANES - HEAD_DIM))).reshape(d, heads * LANES)


def _pad_heads_rows(w, heads):
    d = w.shape[1]
    w = w.reshape(heads, HEAD_DIM, d)
    return jnp.pad(w, ((0, 0), (LANES - HEAD_DIM, 0), (0, 0))).reshape(heads * LANES, d)


def _head_gain(gain, heads, scale):
    g = jnp.pad(gain.astype(F32) * scale, (0, LANES - HEAD_DIM))
    return jnp.tile(g, heads)


def _key_gain(gain):
    return jnp.concatenate([gain.astype(F32), jnp.ones((HEAD_DIM,), F32)])


def _cover_t(n_cmp_pad, n_cmp, n_blocks, nb_pad, row_off):
    m = jnp.arange(n_cmp_pad)
    c_start = m * CMP_STRIDE
    c_end = c_start + CMP_BLOCK - 1
    jb = jnp.arange(nb_pad) - row_off
    b_start = jb * SLC_BLOCK
    b_end = b_start + SLC_BLOCK - 1
    cov = (c_start[None, :] <= b_end[:, None]) & (c_end[None, :] >= b_start[:, None])
    cov = cov & (m[None, :] < n_cmp) & (jb[:, None] >= 0) & (jb[:, None] < n_blocks)
    return cov.astype(BF16)


def _route(r, n_tok, tm):
    top_e = r[:, 0:2].astype(jnp.int32)
    n_asg = n_tok * 2
    flat_e = top_e.reshape(n_asg)
    order = jnp.argsort(flat_e)
    sorted_e = flat_e[order]
    counts = jnp.sum(flat_e[:, None] == jnp.arange(N_EXPERTS)[None, :], axis=0).astype(jnp.int32)
    padded = (counts + tm - 1) // tm * tm
    start = jnp.cumsum(counts) - counts
    pend = jnp.cumsum(padded)
    pstart = pend - padded
    dest_sorted = pstart[sorted_e] + jnp.arange(n_asg, dtype=jnp.int32) - start[sorted_e]
    dest = jnp.zeros((n_asg,), jnp.int32).at[order].set(dest_sorted.astype(jnp.int32))
    n_blk = n_asg // tm + N_EXPERTS
    cap = n_blk * tm
    row_tok = jnp.zeros((cap,), jnp.int32).at[dest].set(jnp.arange(n_asg, dtype=jnp.int32) // 2)
    blk_e = jnp.minimum(jnp.sum(jnp.arange(n_blk)[:, None] * tm >= pend[None, :], axis=1), N_EXPERTS - 1)
    n_used = (pend[-1] // tm).astype(jnp.int32).reshape(1)
    dest_t = dest.reshape(n_tok, 2).T.reshape(n_asg)
    return dest_t, row_tok, blk_e.astype(jnp.int32), n_used


def kernel(x_prompt, x_sample, state_pool, cache_nsa_cmp, cache_nsa_slc, cache_nsa_win, cache_swa, page_table, norm0_mix, w_in0, pool_w, pool_scale, nsa_q_gain, nsa_k_gain, cmp_k_w1, cmp_k_w2, cmp_k_pe, cmp_v_w1, cmp_v_w2, cmp_v_pe, w_out0, norm0_ffn, ffn_w_gate, ffn_w_up, ffn_w_down, norm1_mix, w_in1, swa_q_gain, swa_k_gain, swa_sinks, w_out1, norm1_ffn, router_w, router_b, moe_w_gate, moe_w_up, moe_w_down):
    nb, t, d = x_prompt.shape
    ns, s_len, _ = x_sample.shape
    mp = nb * t
    msz = ns * s_len
    past = page_table.shape[1] * PAGE_SIZE
    total = past + s_len
    g = KV_GROUPS

    x_all = jnp.concatenate([x_prompt.reshape(mp, d), x_sample.reshape(msz, d)], axis=0)

    c_q = POOL_CH
    c_kv = c_q + NSA_HEADS * HEAD_DIM
    c_gate = c_kv + 3 * KV_COLS
    n_gate = 3 * NSA_HEADS
    w0 = jnp.concatenate([
        w_in0[:, :c_q],
        _pad_heads_cols(w_in0[:, c_q:c_kv], NSA_HEADS),
        w_in0[:, c_kv:c_gate],
        jnp.pad(w_in0[:, c_gate:], ((0, 0), (0, LANES - n_gate))),
    ], axis=1).astype(BF16)
    qw = NSA_HEADS * LANES
    aux0 = jnp.concatenate([
        jnp.ones((POOL_CH,), F32),
        _head_gain(nsa_q_gain, NSA_HEADS, ATTN_SCALE),
        jnp.ones((KV_COLS,), F32),
        jnp.tile(_key_gain(nsa_k_gain[1]), g),
        jnp.tile(_key_gain(nsa_k_gain[2]), g),
        jnp.ones((LANES,), F32),
    ]).reshape(1, -1)
    o0 = POOL_CH
    o1 = o0 + qw
    o2 = o1 + KV_COLS
    o3 = o2 + KV_COLS
    o4 = o3 + KV_COLS
    segs0 = (("raw", 0, POOL_CH), ("qnorm", o0, qw), ("raw", o1, KV_COLS), ("kvnorm", o2, KV_COLS),
             ("kvnorm", o3, KV_COLS), ("sigmoid", o4, LANES))
    outs0 = ((POOL_CH, F32), (qw, F32), (KV_COLS, F32), (KV_COLS, F32), (KV_COLS, BF16),
             (KV_COLS, F32), (KV_COLS, BF16), (LANES, F32))
    u_all, q_all, cmp_all, slc_all, slc_bf, win_all, win_bf, gates_all = _proj(
        x_all, norm0_mix.reshape(1, d), w0, aux0, segs0, outs0, 512, "proj0")

    pool_w_bf = pool_w.astype(BF16)
    pool_scale2 = pool_scale.reshape(1, POOL_CH).astype(F32)

    u_p = u_all[:mp].reshape(nb, t, POOL_CH)
    u_s = u_all[mp:].reshape(ns, s_len, POOL_CH)
    pool_o_p = _pool_prompt(u_p, pool_w_bf, pool_scale2)
    u_ext = jnp.concatenate([state_pool.astype(F32), u_s], axis=1)
    x_ext = jnp.pad(u_ext, ((0, 0), (1, 0), (0, 0)))
    pool_o_s = _pool_sample(x_ext, pool_w_bf, pool_scale2, s_len)
    pool_o = jnp.concatenate([pool_o_p.reshape(mp, POOL_CH), pool_o_s.reshape(msz, POOL_CH).astype(BF16)], axis=0)

    def w1_parts(w1):
        return w1.reshape(2, CMP_STRIDE, HEAD_DIM, CMP_HIDDEN)

    w_sel = jnp.stack([w1_parts(cmp_k_w1), w1_parts(cmp_v_w1)] * g, axis=0)
    w1_bd = jnp.einsum("cpjdh,ce->jcdpeh", w_sel, jnp.eye(2 * g, dtype=F32))
    w1_bd = w1_bd.reshape(CMP_STRIDE, KV_COLS, 2 * 2 * g * CMP_HIDDEN).astype(BF16)
    w2_sel = jnp.stack([cmp_k_w2, cmp_v_w2] * g, axis=0)
    w2_bd = jnp.einsum("chd,ce->ched", w2_sel, jnp.eye(2 * g, dtype=F32))
    w2_bd = w2_bd.reshape(2 * g * CMP_HIDDEN, KV_COLS).astype(BF16)
    pek = jnp.pad(cmp_k_pe.reshape(1, -1), ((0, 7), (0, 0))).astype(BF16)
    pev = jnp.pad(cmp_v_pe.reshape(1, -1), ((0, 7), (0, 0))).astype(BF16)
    w1k = cmp_k_w1.astype(BF16)
    w1v = cmp_v_w1.astype(BF16)
    kg0 = _key_gain(nsa_k_gain[0]).reshape(1, LANES)
    seg_cols = CMP_STRIDE * KV_COLS
    half = 2 * g * CMP_HIDDEN

    n_seg_p = t // CMP_STRIDE
    pa_p, pb_p = _compress1(cmp_all[:mp].reshape(nb * n_seg_p, seg_cols), w1_bd, 256)
    kcv_p = _compress2(pa_p.reshape(nb, n_seg_p, half), pb_p.reshape(nb, n_seg_p, half),
                       jnp.zeros((nb, 8, half), F32), pek, pev, w1k, w1v, w2_bd, kg0)
    n_seg_s = past // CMP_STRIDE
    pad_new = -(-total // CMP_STRIDE) * CMP_STRIDE - past
    assert pad_new == CMP_STRIDE, "the new rows must fit one segment"
    new_seg = jnp.pad(cmp_all[mp:].reshape(ns, s_len * KV_COLS), ((0, 0), (0, (pad_new - s_len) * KV_COLS)))
    _, pb_new = _compress1(new_seg, w1_bd, ns)
    pb_new = jnp.pad(pb_new.reshape(ns, 1, half), ((0, 0), (0, 7), (0, 0)))
    w_kv = jnp.stack([w1_parts(cmp_k_w1), w1_parts(cmp_v_w1)], axis=0)
    w1_gs = jnp.einsum("kpjdh,ke->jkdpeh", w_kv, jnp.eye(2, dtype=F32))
    w1_gs = w1_gs.reshape(CMP_STRIDE * 2 * HEAD_DIM, 2 * 2 * CMP_HIDDEN).astype(BF16)
    kcv_s = _paged_compress(page_table, _pages_t(cache_nsa_cmp), pb_new, w1_gs, pek, pev, w1k, w1v, w2_bd, kg0)

    hg0 = NSA_HEADS // g
    q_p = q_all[:mp].reshape(nb, t, qw)
    q_s = q_all[mp:].reshape(ns, s_len, qw)
    nblk_p = t // SLC_BLOCK
    cov_p = _cover_t(n_seg_p, n_seg_p - 1, nblk_p, LANES, HEAD_DIM)
    o_cmp_p, selb_p = _cmp_attn(q_p, kcv_p, cov_p, hg=hg0, tq=256, bn=1, q_off=0, n_blocks=nblk_p,
                                out_dtype=BF16, row_off=HEAD_DIM, as_bias=True)
    nblk_s = -(-total // SLC_BLOCK)
    nbp_s = -(-nblk_s // LANES) * LANES
    n_cmp_s = -(-total // CMP_STRIDE) - 1
    cov_s = _cover_t(n_seg_s, n_cmp_s, nblk_s, nbp_s, 0)
    o_cmp_s, sel_s = _cmp_attn(q_s, kcv_s, cov_s, hg=hg0, tq=s_len, bn=LANES // s_len, q_off=past,
                               n_blocks=nblk_s, out_dtype=F32, row_off=0, as_bias=False)

    slc_p = slc_bf[:mp].reshape(nb, t, KV_COLS)
    o_slc_p = _flash(q_p, slc_p, hg=hg0, tq=512, window=None, chunk=256, selb=selb_p)
    tk_s = -(-total // PAGE_SIZE) * PAGE_SIZE
    expand_s = (jnp.arange(tk_s)[None, :] // SLC_BLOCK == jnp.arange(nbp_s)[:, None]).astype(BF16)
    new_slc_t = jnp.pad(jnp.transpose(slc_all[mp:].reshape(ns, s_len, KV_COLS), (0, 2, 1)),
                        ((0, 0), (0, 0), (0, PAGE_SIZE - s_len)))
    o_slc_s = _paged_slc(page_table, _pages_t(cache_nsa_slc), q_s, new_slc_t, sel_s, expand_s, hg=hg0, q_off=past)

    win_p_rows = win_bf[:mp].reshape(nb, t, KV_COLS)
    o_win_p = _flash(q_p, win_p_rows, hg=hg0, tq=256, window=NSA_WINDOW, chunk=256)
    wb_len = cache_nsa_win.shape[1]
    tk_w = -(-(wb_len + s_len) // 16) * 16
    kv_win_s = jnp.concatenate([
        cache_nsa_win.astype(BF16).reshape(ns, wb_len, KV_COLS), win_bf[mp:].reshape(ns, s_len, KV_COLS),
        jnp.zeros((ns, tk_w - wb_len - s_len, KV_COLS), BF16)], axis=1)
    o_win_s = _attn(q_s, kv_win_s, hg=hg0, window=NSA_WINDOW, q_off=past, k_off=past - wb_len)

    cat = lambda a, b: jnp.concatenate([a.reshape(mp, -1), b.reshape(msz, -1).astype(BF16)], axis=0)
    eg = (jnp.arange(LANES)[None, :, None] ==
          (jnp.arange(3)[:, None, None] * NSA_HEADS + jnp.arange(qw)[None, None, :] // LANES)).astype(BF16)
    wa0 = w_out0[:POOL_CH].astype(BF16)
    wb0 = _pad_heads_rows(w_out0[POOL_CH:], NSA_HEADS).astype(BF16)
    x1 = _out0(x_all, pool_o, cat(o_cmp_p, o_cmp_s), cat(o_slc_p, o_slc_s), cat(o_win_p, o_win_s),
               gates_all, eg, wa0, wb0)
    x2 = _ffn(x1, norm0_ffn.reshape(1, d), ffn_w_gate.astype(BF16), ffn_w_up.astype(BF16),
              ffn_w_down.astype(BF16))

    c_q1 = SWA_HEADS * HEAD_DIM
    qw1 = SWA_HEADS * LANES
    w1p = jnp.concatenate([_pad_heads_cols(w_in1[:, :c_q1], SWA_HEADS), w_in1[:, c_q1:]], axis=1).astype(BF16)
    aux1 = jnp.concatenate([_head_gain(swa_q_gain, SWA_HEADS, ATTN_SCALE),
                            jnp.tile(_key_gain(swa_k_gain), g)]).reshape(1, -1)
    segs1 = (("qnorm", 0, qw1), ("kvnorm", qw1, KV_COLS))
    outs1 = ((qw1, F32), (KV_COLS, F32), (KV_COLS, BF16))
    q1_all, swa_all, swa_bf = _proj(x2, norm1_mix.reshape(1, d), w1p, aux1, segs1, outs1, 512, "proj1")
    hg1 = SWA_HEADS // g
    sinks = swa_sinks.astype(F32)
    q1_p = q1_all[:mp].reshape(nb, t, qw1)
    q1_s = q1_all[mp:].reshape(ns, s_len, qw1)
    o1_p = _flash(q1_p, swa_bf[:mp].reshape(nb, t, KV_COLS), hg=hg1, tq=128, window=SWA_WINDOW, chunk=256,
                  sinks=sinks)
    sb_len = cache_swa.shape[1]
    tk_1 = -(-(sb_len + s_len) // 16) * 16
    kv_swa_s = jnp.concatenate([
        cache_swa.astype(BF16).reshape(ns, sb_len, KV_COLS), swa_bf[mp:].reshape(ns, s_len, KV_COLS),
        jnp.zeros((ns, tk_1 - sb_len - s_len, KV_COLS), BF16)], axis=1)
    o1_s = _attn(q1_s, kv_swa_s, hg=hg1, window=SWA_WINDOW, q_off=past, k_off=past - sb_len, sinks=sinks)

    rw = jnp.pad(router_w.astype(F32), ((0, 0), (0, LANES - N_EXPERTS)))
    rwh, rwl = _split_bf16(rw)
    rb = jnp.pad(router_b.astype(F32), (0, LANES - N_EXPERTS)).reshape(1, LANES)
    x3, h3, r = _out1(x2, cat(o1_p, o1_s), _pad_heads_rows(w_out1, SWA_HEADS).astype(BF16),
                      norm1_ffn.reshape(1, d), rwh, rwl, rb)
    m_all = mp + msz
    tm_moe = 512
    dest_t, row_tok, blk_e, n_used = _route(r, m_all, tm_moe)
    n_blk = row_tok.shape[0] // tm_moe
    xs = _dispatch(row_tok, h3, tm_moe * max(k for k in (4, 2, 1) if n_blk % k == 0))
    ys = _moe(xs, blk_e, n_used, moe_w_gate.astype(BF16), moe_w_up.astype(BF16), moe_w_down.astype(BF16),
              tm_moe, 512)
    x4 = _combine(dest_t, x3, r, ys)

    y_p = x4[:mp].reshape(nb, t, d)
    y_s = x4[mp:].reshape(ns, s_len, d)
    rows5 = lambda a, n, l: a.reshape(n, l, g, 2, HEAD_DIM)
    keep_w = min(NSA_WINDOW, t)
    keep_s = min(SWA_WINDOW, t)
    pool_p = u_p[:, t - POOL_STATE:]
    cmp_p = rows5(cmp_all[:mp], nb, t)
    slc_p_out = rows5(slc_all[:mp], nb, t)
    win_p_out = rows5(win_all[:mp], nb, t)[:, t - keep_w:]
    swa_p_out = rows5(swa_all[:mp], nb, t)[:, t - keep_s:]
    pool_s = u_ext[:, -POOL_STATE:]
    cmp_s = rows5(cmp_all[mp:], ns, s_len)
    slc_s_out = rows5(slc_all[mp:], ns, s_len)
    win_s_out = jnp.concatenate([cache_nsa_win.astype(F32), rows5(win_all[mp:], ns, s_len)], axis=1)[:, -wb_len:]
    swa_s_out = jnp.concatenate([cache_swa.astype(F32), rows5(swa_all[mp:], ns, s_len)], axis=1)[:, -sb_len:]
    return (y_p, y_s, pool_p, cmp_p, slc_p_out, win_p_out, swa_p_out, pool_s, cmp_s, slc_s_out, win_s_out,
            swa_s_out)
```

```python
import functools

import jax
import jax.numpy as jnp
from jax import lax
from jax.experimental import pallas as pl
from jax.experimental.pallas import tpu as pltpu

F32 = jnp.float32
BF16 = jnp.bfloat16

LANES = 128
MXU_COLS = 256
HEAD_DIM = 64
D_MODEL = 1024
POOL_WINDOWS = (2, 4, 8, 16)
POOL_CH = 512
POOL_GROUP = 128
POOL_STATE = 15
POOL_HALO = 16
NSA_HEADS = 8
SWA_HEADS = 16
KV_GROUPS = 2
KV_COLS = KV_GROUPS * 2 * HEAD_DIM
CMP_BLOCK = 32
CMP_STRIDE = 16
CMP_HIDDEN = 128
SLC_BLOCK = 64
SLC_SHIFT = SLC_BLOCK.bit_length() - 1
SLC_TOPN = 16
NSA_WINDOW = 512
SWA_WINDOW = 128
PAGE_SIZE = 128
N_EXPERTS = 8
RMS_EPS = 1e-6
NEG_INF = -1e30
FORCE_SCORE = 1e9
PAD_SCORE = -3e38
ATTN_SCALE = HEAD_DIM ** -0.5
VMEM_LIMIT = 56 * 1024 * 1024


def _cparams(sem):
    return pltpu.CompilerParams(dimension_semantics=sem, vmem_limit_bytes=VMEM_LIMIT)


def _split_bf16(x):
    hi = x.astype(BF16)
    lo = (x - hi.astype(F32)).astype(BF16)
    return hi, lo


def _dot(a, b):
    return jnp.dot(a, b, preferred_element_type=F32)


def _dot_nt(a, b):
    return lax.dot_general(a, b, (((1,), (1,)), ((), ())), preferred_element_type=F32)


def _row_part_specs(parts, tm):
    specs, bounds, t0 = [], [], 0
    for a in parts:
        nt = a.shape[0] // tm
        assert nt * tm == a.shape[0]
        specs.append(pl.BlockSpec((tm, a.shape[1]), lambda i, t0=t0, nt=nt: (jnp.clip(i - t0, 0, nt - 1), 0)))
        t0 += nt
        bounds.append(t0)
    return specs, tuple(bounds)


def _row_part(refs, bounds):
    i = pl.program_id(0)
    v = refs[-1][...]
    for k in range(len(refs) - 2, -1, -1):
        v = jnp.where(i < bounds[k], refs[k][...], v)
    return v


def _proj_kernel(*refs, segs, bounds):
    np_ = len(bounds)
    x = _row_part(refs[:np_], bounds)
    g_ref, w_ref, aux_ref = refs[np_:np_ + 3]
    out_refs = refs[np_ + 3:]
    ms = jnp.mean(x * x, axis=-1, keepdims=True)
    h = (x * lax.rsqrt(ms + RMS_EPS) * g_ref[...]).astype(BF16)
    lane = lax.broadcasted_iota(jnp.int32, (1, LANES), 1)
    is_key = lane < HEAD_DIM
    oi = 0
    for kind, c0, width in segs:
        if kind == "qnorm":
            for pair in range(width // MXU_COLS):
                cp = c0 + pair * MXU_COLS
                z = _dot(h, w_ref[:, cp:cp + MXU_COLS])
                for hh in range(MXU_COLS // LANES):
                    zh = z[:, hh * LANES:(hh + 1) * LANES]
                    msq = jnp.sum(zh * zh, axis=-1, keepdims=True) * (1.0 / HEAD_DIM)
                    lo = cp - c0 + hh * LANES
                    gain = aux_ref[:, cp + hh * LANES:cp + (hh + 1) * LANES]
                    out_refs[oi][:, lo:lo + LANES] = zh * lax.rsqrt(msq + RMS_EPS) * gain
            oi += 1
            continue
        z = _dot(h, w_ref[:, c0:c0 + width])
        aux = aux_ref[:, c0:c0 + width]
        if kind == "raw":
            out_refs[oi][...] = z
            oi += 1
        elif kind == "sigmoid":
            out_refs[oi][...] = jax.nn.sigmoid(z)
            oi += 1
        elif kind == "kvnorm":
            for gg in range(width // LANES):
                sl = slice(gg * LANES, (gg + 1) * LANES)
                zb = z[:, sl]
                msq = jnp.sum(jnp.where(is_key, zb * zb, 0.0), axis=-1, keepdims=True) * (1.0 / HEAD_DIM)
                y = jnp.where(is_key, zb * lax.rsqrt(msq + RMS_EPS) * aux[:, sl], zb)
                out_refs[oi][:, sl] = y
                out_refs[oi + 1][:, sl] = y.astype(BF16)
            oi += 2
        else:
            raise ValueError(kind)


def _proj(x_parts, g, w, aux, segs, out_defs, tm, name):
    m = sum(a.shape[0] for a in x_parts)
    d = x_parts[0].shape[1]
    c = w.shape[1]
    out_shape = [jax.ShapeDtypeStruct((m, wd), dt) for wd, dt in out_defs]
    out_specs = [pl.BlockSpec((tm, wd), lambda i: (i, 0)) for wd, _ in out_defs]
    x_specs, bounds = _row_part_specs(x_parts, tm)
    return pl.pallas_call(
        functools.partial(_proj_kernel, segs=segs, bounds=bounds),
        out_shape=out_shape,
        grid=(m // tm,),
        in_specs=x_specs + [
            pl.BlockSpec((1, d), lambda i: (0, 0)),
            pl.BlockSpec((d, c), lambda i: (0, 0)),
            pl.BlockSpec((1, c), lambda i: (0, 0)),
        ],
        out_specs=out_specs,
        compiler_params=_cparams(("parallel",)),
        name=name,
    )(*x_parts, g, w, aux)


def _pool_prompt_kernel(u_ref, halo_ref, pw_ref, ps_ref, o_ref, e_ref, *, tp):
    i = pl.program_id(1)
    e_ref[0:POOL_HALO, :] = jnp.where(i > 0, halo_ref[0], 0.0)
    e_ref[POOL_HALO:, :] = u_ref[0]
    t = i * tp + lax.broadcasted_iota(jnp.int32, (tp, 1), 0)
    for gi, w in enumerate(POOL_WINDOWS):
        sl = slice(gi * POOL_GROUP, (gi + 1) * POOL_GROUP)
        tok = e_ref[POOL_HALO:POOL_HALO + tp, sl]
        s = tok
        for k in range(1, w):
            s = s + e_ref[POOL_HALO - k:POOL_HALO - k + tp, sl]
        cnt = jnp.minimum(t + 1, w).astype(F32)
        dg = s / cnt - tok
        y = _dot(dg.astype(BF16), pw_ref[gi]) * ps_ref[:, sl]
        o_ref[0, :, sl] = y.astype(o_ref.dtype)


def _pool_prompt(u, pool_w, pool_scale, tp=512):
    n, t, c = u.shape
    hb = tp // POOL_HALO
    return pl.pallas_call(
        functools.partial(_pool_prompt_kernel, tp=tp),
        out_shape=jax.ShapeDtypeStruct((n, t, c), BF16),
        grid=(n, t // tp),
        in_specs=[
            pl.BlockSpec((1, tp, c), lambda b, i: (b, i, 0)),
            pl.BlockSpec((1, POOL_HALO, c), lambda b, i: (b, jnp.maximum(i * hb - 1, 0), 0)),
            pl.BlockSpec((len(POOL_WINDOWS), POOL_GROUP, POOL_GROUP), lambda b, i: (0, 0, 0)),
            pl.BlockSpec((1, c), lambda b, i: (0, 0)),
        ],
        out_specs=pl.BlockSpec((1, tp, c), lambda b, i: (b, i, 0)),
        scratch_shapes=[pltpu.VMEM((tp + POOL_HALO, c), F32)],
        compiler_params=_cparams(("parallel", "parallel")),
        name="pool_prompt",
    )(u, u, pool_w, pool_scale)


def _pool_sample_kernel(x_ref, pw_ref, ps_ref, o_ref, *, bn, s_len):
    base = 1 + POOL_STATE
    for gi, w in enumerate(POOL_WINDOWS):
        sl = slice(gi * POOL_GROUP, (gi + 1) * POOL_GROUP)
        tok = x_ref[:, base:base + s_len, sl]
        s = tok
        for k in range(1, w):
            s = s + x_ref[:, base - k:base - k + s_len, sl]
        dg = (s / float(w) - tok).reshape(bn * s_len, POOL_GROUP)
        y = _dot(dg.astype(BF16), pw_ref[gi]) * ps_ref[:, sl]
        o_ref[:, :, sl] = y.reshape(bn, s_len, POOL_GROUP).astype(o_ref.dtype)


def _pool_sample(x_ext, pool_w, pool_scale, s_len, bn=16):
    n, l, c = x_ext.shape
    return pl.pallas_call(
        functools.partial(_pool_sample_kernel, bn=bn, s_len=s_len),
        out_shape=jax.ShapeDtypeStruct((n, s_len, c), F32),
        grid=(n // bn,),
        in_specs=[
            pl.BlockSpec((bn, l, c), lambda b: (b, 0, 0)),
            pl.BlockSpec((len(POOL_WINDOWS), POOL_GROUP, POOL_GROUP), lambda b: (0, 0, 0)),
            pl.BlockSpec((1, c), lambda b: (0, 0)),
        ],
        out_specs=pl.BlockSpec((bn, s_len, c), lambda b: (b, 0, 0)),
        compiler_params=_cparams(("parallel",)),
        name="pool_sample",
    )(x_ext, pool_w, pool_scale)


def _compress1_kernel(x_ref, w_ref, pa_ref, pb_ref):
    acc = None
    for j in range(CMP_STRIDE):
        xj = x_ref[:, j * KV_COLS:(j + 1) * KV_COLS].astype(BF16)
        d = _dot(xj, w_ref[j])
        acc = d if acc is None else acc + d
    half = 4 * CMP_HIDDEN
    pa_ref[...] = acc[:, :half]
    pb_ref[...] = acc[:, half:]


def _compress1(x, w_bd, ts):
    r, c = x.shape
    ts = min(ts, r)
    half = 4 * CMP_HIDDEN
    return pl.pallas_call(
        _compress1_kernel,
        out_shape=[jax.ShapeDtypeStruct((r, half), F32), jax.ShapeDtypeStruct((r, half), F32)],
        grid=(r // ts,),
        in_specs=[
            pl.BlockSpec((ts, c), lambda i: (i, 0)),
            pl.BlockSpec(w_bd.shape, lambda i: (0, 0, 0)),
        ],
        out_specs=[pl.BlockSpec((ts, half), lambda i: (i, 0)), pl.BlockSpec((ts, half), lambda i: (i, 0))],
        compiler_params=_cparams(("parallel",)),
        name="compress1",
    )(x, w_bd)


def _gelu_tanh(x):
    return 0.5 * x * (1.0 + jnp.tanh(0.7978845608028654 * (x + 0.044715 * (x * x * x))))


def _compress_tail(pa, pb, pb_new, pek_ref, pev_ref, w1k_ref, w1v_ref, w2_ref, kg_ref, o_ref):
    s = pa.shape[0]
    row = lax.broadcasted_iota(jnp.int32, (s, 1), 0)
    pb_next = jnp.where(row == s - 1, pb_new, pltpu.roll(pb, s - 1, 0))
    hk = _dot(pek_ref[...], w1k_ref[...])[0:1, :]
    hv = _dot(pev_ref[...], w1v_ref[...])[0:1, :]
    pe_h = jnp.concatenate([hk, hv, hk, hv], axis=1)
    a = _gelu_tanh(pa + pb_next + pe_h)
    o = _dot(a.astype(BF16), w2_ref[...])
    lane = lax.broadcasted_iota(jnp.int32, (1, LANES), 1)
    is_key = lane < HEAD_DIM
    for gg in range(KV_GROUPS):
        sl = slice(gg * LANES, (gg + 1) * LANES)
        ob = o[:, sl]
        msq = jnp.sum(jnp.where(is_key, ob * ob, 0.0), axis=-1, keepdims=True) * (1.0 / HEAD_DIM)
        y = jnp.where(is_key, ob * lax.rsqrt(msq + RMS_EPS) * kg_ref[...], ob)
        o_ref[0, :, sl] = y.astype(o_ref.dtype)


def _compress2_kernel(pa_ref, pb_ref, pbn_ref, pek_ref, pev_ref, w1k_ref, w1v_ref, w2_ref, kg_ref, o_ref):
    _compress_tail(pa_ref[0], pb_ref[0], pbn_ref[0][0:1, :], pek_ref, pev_ref, w1k_ref, w1v_ref, w2_ref,
                   kg_ref, o_ref)


def _pages_t(cache):
    return jnp.transpose(cache, (0, 2, 3, 4, 1)).reshape(cache.shape[0], KV_COLS, cache.shape[1])


def _paged_prefetch(pt_ref, cache_ref, dst, sem, n_pages):
    b = pl.program_id(0)
    slot = b % 2

    def fetch(bb, sl):
        def body(p, c):
            pltpu.make_async_copy(cache_ref.at[pt_ref[bb * n_pages + p]], dst(sl, p), sem.at[sl]).start()
            return c

        lax.fori_loop(0, n_pages, body, 0)

    @pl.when(b == 0)
    def _():
        fetch(0, 0)

    @pl.when(b + 1 < pl.num_programs(0))
    def _():
        fetch(b + 1, 1 - slot)

    def wait(p, c):
        pltpu.make_async_copy(cache_ref.at[0], dst(slot, p), sem.at[slot]).wait()
        return c

    lax.fori_loop(0, n_pages, wait, 0)
    return slot


def _paged_compress_kernel(pt_ref, cache_ref, pbn_ref, w1_ref, pek_ref, pev_ref, w1k_ref, w1v_ref, w2_ref, kg_ref,
                           o_ref, buf, rows_sc, sem, *, n_pages):
    slot = _paged_prefetch(
        pt_ref, cache_ref, lambda sl, p: buf.at[sl, pl.ds(pl.multiple_of(p * KV_COLS, KV_COLS), KV_COLS)],
        sem, n_pages)
    for p in range(n_pages):
        for gg in range(KV_GROUPS):
            f0 = p * KV_COLS + gg * LANES
            rows_sc[gg, p * PAGE_SIZE:(p + 1) * PAGE_SIZE, :] = buf[slot, f0:f0 + LANES, :].T
    n_seg = n_pages * (PAGE_SIZE // CMP_STRIDE)
    hid2 = 2 * CMP_HIDDEN
    pas, pbs = [], []
    for gg in range(KV_GROUPS):
        xg = jnp.concatenate(
            [rows_sc[gg, pl.ds(j, n_seg, stride=CMP_STRIDE), :].astype(BF16)
             for j in range(CMP_STRIDE)], axis=1)
        pg = _dot(xg, w1_ref[...])
        pas.append(pg[:, :hid2])
        pbs.append(pg[:, hid2:])
    _compress_tail(jnp.concatenate(pas, axis=1), jnp.concatenate(pbs, axis=1), pbn_ref[0][0:1, :],
                   pek_ref, pev_ref, w1k_ref, w1v_ref, w2_ref, kg_ref, o_ref)


def _paged_compress(page_table, cache_t, pb_new, w1_gs, pek, pev, w1k, w1v, w2_bd, kg):
    ns, n_pages = page_table.shape
    s = n_pages * (PAGE_SIZE // CMP_STRIDE)
    c = pb_new.shape[-1]
    full2 = lambda b, pt: (0, 0)
    grid_spec = pltpu.PrefetchScalarGridSpec(
        num_scalar_prefetch=1,
        grid=(ns,),
        in_specs=[
            pl.BlockSpec(memory_space=pl.ANY),
            pl.BlockSpec((1, 8, c), lambda b, pt: (b, 0, 0)),
            pl.BlockSpec(w1_gs.shape, full2),
            pl.BlockSpec(pek.shape, full2),
            pl.BlockSpec(pev.shape, full2),
            pl.BlockSpec(w1k.shape, full2),
            pl.BlockSpec(w1v.shape, full2),
            pl.BlockSpec(w2_bd.shape, full2),
            pl.BlockSpec(kg.shape, full2),
        ],
        out_specs=pl.BlockSpec((1, s, KV_COLS), lambda b, pt: (b, 0, 0)),
        scratch_shapes=[pltpu.VMEM((2, n_pages * KV_COLS, PAGE_SIZE), F32),
                        pltpu.VMEM((KV_GROUPS, n_pages * PAGE_SIZE, LANES), F32), pltpu.SemaphoreType.DMA((2,))],
    )
    return pl.pallas_call(
        functools.partial(_paged_compress_kernel, n_pages=n_pages),
        out_shape=jax.ShapeDtypeStruct((ns, s, KV_COLS), BF16),
        grid_spec=grid_spec,
        compiler_params=_cparams(("arbitrary",)),
        name="paged_compress",
    )(page_table.reshape(-1), cache_t, pb_new, w1_gs, pek, pev, w1k, w1v, w2_bd, kg)


def _compress2(pa, pb, pb_new, pek, pev, w1k, w1v, w2_bd, kg):
    n, s, c = pa.shape
    full2 = lambda b: (0, 0)
    return pl.pallas_call(
        _compress2_kernel,
        out_shape=jax.ShapeDtypeStruct((n, s, KV_COLS), BF16),
        grid=(n,),
        in_specs=[
            pl.BlockSpec((1, s, c), lambda b: (b, 0, 0)),
            pl.BlockSpec((1, s, c), lambda b: (b, 0, 0)),
            pl.BlockSpec((1, 8, c), lambda b: (b, 0, 0)),
            pl.BlockSpec(pek.shape, full2),
            pl.BlockSpec(pev.shape, full2),
            pl.BlockSpec(w1k.shape, full2),
            pl.BlockSpec(w1v.shape, full2),
            pl.BlockSpec(w2_bd.shape, full2),
            pl.BlockSpec(kg.shape, full2),
        ],
        out_specs=pl.BlockSpec((1, s, KV_COLS), lambda b: (b, 0, 0)),
        compiler_params=_cparams(("parallel",)),
        name="compress2",
    )(pa, pb, pb_new, pek, pev, w1k, w1v, w2_bd, kg)


def _stack_heads(q, hg):
    return jnp.concatenate([q[:, h * LANES:(h + 1) * LANES] for h in range(hg)], axis=0).astype(BF16)


def _row_qpos(hg, tq, q0):
    r = lax.broadcasted_iota(jnp.int32, (hg * tq, 1), 0)
    return q0 + (r & (tq - 1))


def _store_heads(o_ref, lead, o, hg, tq, head0=0):
    lane = lax.broadcasted_iota(jnp.int32, (1, LANES), 1)
    for h in range(hg):
        blk = jnp.where(lane >= HEAD_DIM, o[h * tq:(h + 1) * tq], 0.0)
        c0 = (head0 + h) * LANES
        o_ref[lead + (slice(None), slice(c0, c0 + LANES))] = blk.astype(o_ref.dtype)


def _cmp_attn_kernel(q_ref, kc_ref, cov_ref, o_ref, sel_ref, *, bn, hg, tq, q_off, n_blocks, row_off, as_bias):
    i = pl.program_id(2)
    s_len = kc_ref.shape[1]
    q0 = q_off + i * tq
    qpos = _row_qpos(hg, tq, q0)
    blk_end = lax.broadcasted_iota(jnp.int32, (1, s_len), 1) * CMP_STRIDE + (CMP_BLOCK - 1)
    cond = blk_end <= qpos
    psums = []
    for b in range(bn):
        qs = _stack_heads(q_ref[b], hg)
        kc = kc_ref[b]
        s = jnp.where(cond, _dot_nt(qs, kc), NEG_INF)
        m = jnp.max(s, axis=-1, keepdims=True)
        p = jnp.where(cond, jnp.exp(s - m), 0.0)
        den = jnp.sum(p, axis=-1, keepdims=True)
        p = p / jnp.maximum(den, 1e-30)
        o = _dot(p.astype(BF16), kc)
        _store_heads(o_ref, (b,), o, hg, tq)
        ps = p[0:tq]
        for h in range(1, hg):
            ps = ps + p[h * tq:(h + 1) * tq]
        psums.append(ps)
    ps = psums[0] if bn == 1 else jnp.concatenate(psums, axis=0)
    hi, lo = _split_bf16(ps)
    cov = cov_ref[...]
    score = _dot_nt(cov, hi) + _dot_nt(cov, lo)
    nb_pad, nl = score.shape
    j = lax.broadcasted_iota(jnp.int32, (nb_pad, nl), 0) - row_off
    t = q0 + (lax.broadcasted_iota(jnp.int32, (1, nl), 1) & (tq - 1))
    cur = t >> SLC_SHIFT
    forced = (j == 0) | (j == cur) | (j == cur - 1)
    valid = j * SLC_BLOCK <= t
    score = jnp.where(valid, jnp.where(forced, FORCE_SCORE, score), NEG_INF)
    is_block = lax.bitcast_convert_type(j, jnp.uint32) < jnp.uint32(n_blocks)
    score = jnp.where(is_block, score, PAD_SCORE)
    sel = jnp.zeros((nb_pad, nl), F32)
    for _ in range(min(SLC_TOPN, n_blocks)):
        mx = jnp.max(score, axis=0, keepdims=True)
        jm = jnp.min(jnp.where(score == mx, j, nb_pad), axis=0, keepdims=True)
        pick = j == jm
        sel = jnp.where(pick, 1.0, sel)
        score = jnp.where(pick, -jnp.inf, score)
    if as_bias:
        sel = jnp.where(is_block, (sel - 1.0) * (-NEG_INF), 0.0)
    sel_t = sel.T
    for b in range(bn):
        sel_ref[b, 0] = sel_t[b * tq:(b + 1) * tq]


def _cmp_attn(q, kc, cov_t, *, hg, tq, bn, q_off, n_blocks, out_dtype, row_off, as_bias):
    n, t, hc = q.shape
    s_len = kc.shape[1]
    nb_pad = cov_t.shape[0]
    g = KV_GROUPS
    return pl.pallas_call(
        functools.partial(_cmp_attn_kernel, bn=bn, hg=hg, tq=tq, q_off=q_off, n_blocks=n_blocks,
                          row_off=row_off, as_bias=as_bias),
        out_shape=[jax.ShapeDtypeStruct((n, t, hc), out_dtype), jax.ShapeDtypeStruct((n, g, t, nb_pad), F32)],
        grid=(n // bn, g, t // tq),
        in_specs=[
            pl.BlockSpec((bn, tq, hg * LANES), lambda b, gg, i: (b, i, gg)),
            pl.BlockSpec((bn, s_len, LANES), lambda b, gg, i: (b, 0, gg)),
            pl.BlockSpec((nb_pad, s_len), lambda b, gg, i: (0, 0)),
        ],
        out_specs=[
            pl.BlockSpec((bn, tq, hg * LANES), lambda b, gg, i: (b, i, gg)),
            pl.BlockSpec((bn, 1, tq, nb_pad), lambda b, gg, i: (b, gg, i, 0)),
        ],
        compiler_params=_cparams(("parallel", "parallel", "parallel")),
        name="cmp_attn",
    )(q, kc, cov_t)


def _flash_kernel(*refs, hg, tq, tk, window, use_sel, use_sink):
    refs = list(refs)
    sink_ref = refs.pop(0) if use_sink else None
    q_ref = refs.pop(0)
    kv_ref = refs.pop(0)
    selb_ref = refs.pop(0) if use_sel else None
    o_ref, qs_sc, m_sc, acc_sc = refs
    g = pl.program_id(1)
    i = pl.program_id(2)
    rows = hg * tq
    q0 = i * tq
    for h in range(hg):
        qh = q_ref[:, h * LANES:(h + 1) * LANES]
        if use_sel:
            qh = qh + selb_ref[0, 0]
        qs_sc[h * tq:(h + 1) * tq, :] = qh.astype(BF16)
    m_sc[...] = jnp.full((rows, LANES), NEG_INF, F32)
    acc_sc[...] = jnp.zeros((rows, LANES), F32)

    def tile(k0, masked):
        kv = kv_ref[pl.ds(k0, tk), :]
        klane = lax.broadcasted_iota(jnp.int32, (tk, LANES), 1)
        if use_sel:
            krow = lax.broadcasted_iota(jnp.int32, (tk, LANES), 0)
            onehot = jnp.where(klane - HEAD_DIM == ((k0 + krow) >> SLC_SHIFT), 1.0, 0.0).astype(BF16)
            kaug = jnp.where(klane < HEAD_DIM, kv, onehot)
        else:
            kaug = kv
        vaug = jnp.where(klane < HEAD_DIM, jnp.ones_like(kv), kv)
        s = _dot_nt(qs_sc[...], kaug)
        if masked:
            qpos = q0 + (lax.broadcasted_iota(jnp.int32, (rows, 1), 0) & (tq - 1))
            d = qpos - (k0 + lax.broadcasted_iota(jnp.int32, (1, tk), 1))
            if window is None:
                s = jnp.where(d >= 0, s, NEG_INF)
            else:
                s = jnp.where(lax.bitcast_convert_type(d, jnp.uint32) <= jnp.uint32(window), s, NEG_INF)
        m_prev = m_sc[...]
        m_new = jnp.maximum(m_prev, jnp.max(s, axis=-1, keepdims=True))
        alpha = jnp.exp(m_prev - m_new)
        p = jnp.exp(s - jnp.concatenate([m_new] * (tk // LANES), axis=1))
        acc_sc[...] = alpha * acc_sc[...] + _dot(p.astype(BF16), vaug)
        m_sc[...] = m_new

    if window is None:
        def body(jt, c):
            tile(pl.multiple_of(jt * tk, tk), False)
            return c

        lax.fori_loop(0, i, body, 0)
        tile(pl.multiple_of(i * tk, tk), True)
    else:
        tile(pl.multiple_of(jnp.maximum(q0 - window, 0), LANES), True)

    acc = acc_sc[...]
    den = pltpu.roll(acc, HEAD_DIM, 1)
    m = m_sc[...]
    if use_sink:
        sink = jnp.concatenate([jnp.full((tq, LANES), sink_ref[g * hg + h], F32) for h in range(hg)], axis=0)
        m_fin = jnp.maximum(m, sink)
        w = jnp.exp(m - m_fin)
        o = acc * w / jnp.maximum(den * w + jnp.exp(sink - m_fin), 1e-30)
    else:
        o = jnp.where(m > 0.5 * NEG_INF, acc / jnp.maximum(den, 1e-30), 0.0)
    lane = lax.broadcasted_iota(jnp.int32, (1, LANES), 1)
    o = jnp.where(lane >= HEAD_DIM, o, 0.0)
    for h in range(hg):
        o_ref[:, h * LANES:(h + 1) * LANES] = o[h * tq:(h + 1) * tq].astype(o_ref.dtype)


def _flash(q, kv, *, n, t, hg, tq, window, selb=None, sinks=None):
    hc = q.shape[1]
    nq = t // tq
    g = KV_GROUPS
    use_sel = selb is not None
    use_sink = sinks is not None
    rows = hg * tq
    tk = tq if window is None else tq + window
    if use_sel:
        assert t // SLC_BLOCK <= LANES - HEAD_DIM, "selection bias needs one upper query lane per block"
    in_specs = []
    args = []
    if use_sink:
        in_specs.append(pl.BlockSpec(memory_space=pltpu.SMEM))
        args.append(sinks)
    in_specs += [
        pl.BlockSpec((tq, hg * LANES), lambda b, gg, i: (b * nq + i, gg)),
        pl.BlockSpec((t, LANES), lambda b, gg, i: (b, gg)),
    ]
    args += [q, kv]
    if use_sel:
        in_specs.append(pl.BlockSpec((1, 1, tq, LANES), lambda b, gg, i: (b, gg, i, 0)))
        args.append(selb)
    return pl.pallas_call(
        functools.partial(_flash_kernel, hg=hg, tq=tq, tk=tk, window=window, use_sel=use_sel,
                          use_sink=use_sink),
        out_shape=jax.ShapeDtypeStruct((n * t, hc), BF16),
        grid=(n, g, nq),
        in_specs=in_specs,
        out_specs=pl.BlockSpec((tq, hg * LANES), lambda b, gg, i: (b * nq + i, gg)),
        scratch_shapes=[pltpu.VMEM((rows, LANES), BF16), pltpu.VMEM((rows, LANES), F32),
                        pltpu.VMEM((rows, LANES), F32)],
        compiler_params=_cparams(("parallel", "parallel", "parallel")),
        name="flash",
    )(*args)


def _attn_kernel(*refs, hg, tq, bn, window, q_off, k_off, use_sink):
    refs = list(refs)
    sink_ref = refs.pop(0) if use_sink else None
    q_ref = refs.pop(0)
    kv_ref = refs.pop(0)
    (o_ref,) = refs
    tk = kv_ref.shape[1]
    qpos = _row_qpos(hg, tq, q_off)
    d = qpos - (k_off + lax.broadcasted_iota(jnp.int32, (1, tk), 1))
    if window is None:
        valid = d >= 0
    else:
        valid = lax.bitcast_convert_type(d, jnp.uint32) <= jnp.uint32(window)
    for g in range(KV_GROUPS):
        if use_sink:
            sink = jnp.concatenate([jnp.full((tq, 1), sink_ref[g * hg + h], F32) for h in range(hg)], axis=0)
        for b in range(bn):
            qs = _stack_heads(q_ref[b, :, g * hg * LANES:(g + 1) * hg * LANES], hg)
            kv = kv_ref[b, :, g * LANES:(g + 1) * LANES]
            s = jnp.where(valid, _dot_nt(qs, kv), NEG_INF)
            m = jnp.max(s, axis=-1, keepdims=True)
            if use_sink:
                m = jnp.maximum(m, sink)
            p = jnp.exp(s - m)
            den = jnp.sum(p, axis=-1, keepdims=True)
            if use_sink:
                den = den + jnp.exp(sink - m)
            o = jnp.where(m > 0.5 * NEG_INF, _dot(p.astype(BF16), kv) / jnp.maximum(den, 1e-30), 0.0)
            _store_heads(o_ref, (b,), o, hg, tq, head0=g * hg)


def _attn(q, kv, *, hg, window, q_off, k_off, sinks=None, bn=8):
    n, tq, hc = q.shape
    tkv = kv.shape[1]
    use_sink = sinks is not None
    in_specs = []
    args = []
    if use_sink:
        in_specs.append(pl.BlockSpec(memory_space=pltpu.SMEM))
        args.append(sinks)
    in_specs += [
        pl.BlockSpec((bn, tq, hc), lambda b: (b, 0, 0)),
        pl.BlockSpec((bn, tkv, KV_COLS), lambda b: (b, 0, 0)),
    ]
    args += [q, kv]
    return pl.pallas_call(
        functools.partial(_attn_kernel, hg=hg, tq=tq, bn=bn, window=window, q_off=q_off, k_off=k_off,
                          use_sink=use_sink),
        out_shape=jax.ShapeDtypeStruct((n, tq, hc), F32),
        grid=(n // bn,),
        in_specs=in_specs,
        out_specs=pl.BlockSpec((bn, tq, hc), lambda b: (b, 0, 0)),
        compiler_params=_cparams(("parallel",)),
        name="attn",
    )(*args)


def _paged_slc_kernel(pt_ref, cache_ref, q_ref, new_ref, sel_ref, e_ref, o_ref, buf, sem, *, n_pages, hg, tq,
                      q_off):
    past = n_pages * PAGE_SIZE
    tk = buf.shape[2]
    slot = _paged_prefetch(
        pt_ref, cache_ref, lambda sl, p: buf.at[sl, :, pl.ds(pl.multiple_of(p * PAGE_SIZE, PAGE_SIZE), PAGE_SIZE)],
        sem, n_pages)
    buf[slot, :, past:] = new_ref[0]
    kv_t = buf[slot].astype(BF16)
    rows = hg * tq
    zero = jnp.zeros((rows, LANES), F32)
    q = q_ref[0]
    stack = lambda g: jnp.concatenate([q[:, (g * hg + h) * LANES:(g * hg + h + 1) * LANES] for h in range(hg)],
                                      axis=0)
    q2 = jnp.concatenate([jnp.concatenate([stack(0), zero], axis=1),
                          jnp.concatenate([zero, stack(1)], axis=1)], axis=0).astype(BF16)
    s = _dot(q2, kv_t)
    r = lax.broadcasted_iota(jnp.int32, (2 * rows, 1), 0)
    qpos = q_off + (r & (tq - 1))
    s = jnp.where(qpos - lax.broadcasted_iota(jnp.int32, (1, tk), 1) >= 0, s, NEG_INF)
    sel = jnp.concatenate([sel_ref[0, 0]] * hg + [sel_ref[0, 1]] * hg, axis=0).astype(BF16)
    s = jnp.where(_dot(sel, e_ref[...]) > 0.5, s, NEG_INF)
    m = jnp.max(s, axis=-1, keepdims=True)
    p = jnp.exp(s - m)
    den = jnp.sum(p, axis=-1, keepdims=True)
    o = jnp.where(m > 0.5 * NEG_INF, _dot_nt(p.astype(BF16), kv_t) / jnp.maximum(den, 1e-30), 0.0)
    lane = lax.broadcasted_iota(jnp.int32, (1, LANES), 1)
    for g in range(KV_GROUPS):
        for h in range(hg):
            blk = o[g * rows + h * tq:g * rows + (h + 1) * tq, g * LANES:(g + 1) * LANES]
            c0 = (g * hg + h) * LANES
            o_ref[0, :, c0:c0 + LANES] = jnp.where(lane >= HEAD_DIM, blk, 0.0)


def _paged_slc(page_table, cache_t, q, new_t, sel, expand, *, hg, q_off):
    ns, n_pages = page_table.shape
    _, tq, hc = q.shape
    tk = expand.shape[1]
    assert tk == (n_pages + 1) * PAGE_SIZE and new_t.shape == (ns, KV_COLS, PAGE_SIZE)
    grid_spec = pltpu.PrefetchScalarGridSpec(
        num_scalar_prefetch=1,
        grid=(ns,),
        in_specs=[
            pl.BlockSpec(memory_space=pl.ANY),
            pl.BlockSpec((1, tq, hc), lambda b, pt: (b, 0, 0)),
            pl.BlockSpec((1, KV_COLS, PAGE_SIZE), lambda b, pt: (b, 0, 0)),
            pl.BlockSpec((1, KV_GROUPS, tq, sel.shape[-1]), lambda b, pt: (b, 0, 0, 0)),
            pl.BlockSpec(expand.shape, lambda b, pt: (0, 0)),
        ],
        out_specs=pl.BlockSpec((1, tq, hc), lambda b, pt: (b, 0, 0)),
        scratch_shapes=[pltpu.VMEM((2, KV_COLS, tk), F32), pltpu.SemaphoreType.DMA((2,))],
    )
    return pl.pallas_call(
        functools.partial(_paged_slc_kernel, n_pages=n_pages, hg=hg, tq=tq, q_off=q_off),
        out_shape=jax.ShapeDtypeStruct((ns, tq, hc), F32),
        grid_spec=grid_spec,
        compiler_params=_cparams(("arbitrary",)),
        name="paged_slc",
    )(page_table.reshape(-1), cache_t, q, new_t, sel, expand)


def _out0_kernel(*refs, bounds):
    np_ = len(bounds)
    x, pool, oc, osl, ow = (_row_part(refs[k * np_:(k + 1) * np_], bounds) for k in range(5))
    gt_ref, eg_ref, wa_ref, wb_ref, o_ref = refs[5 * np_:]
    ghi, glo = _split_bf16(gt_ref[...])
    o = None
    for b, br in enumerate((oc, osl, ow)):
        gate = _dot(ghi, eg_ref[b]) + _dot(glo, eg_ref[b])
        term = gate * br.astype(F32)
        o = term if o is None else o + term
    mix = _dot(pool, wa_ref[...]) + _dot(o.astype(BF16), wb_ref[...])
    o_ref[...] = x + mix


def _out0(x_parts, pool_parts, cmp_parts, slc_parts, win_parts, gates, eg, wa, wb, tm=512):
    m = sum(a.shape[0] for a in x_parts)
    d = x_parts[0].shape[1]
    row = lambda w: pl.BlockSpec((tm, w), lambda i: (i, 0))
    specs, args = [], []
    for parts in (x_parts, pool_parts, cmp_parts, slc_parts, win_parts):
        sp, bounds = _row_part_specs(parts, tm)
        specs += sp
        args += list(parts)
    return pl.pallas_call(
        functools.partial(_out0_kernel, bounds=bounds),
        out_shape=jax.ShapeDtypeStruct((m, d), F32),
        grid=(m // tm,),
        in_specs=specs + [row(LANES),
                          pl.BlockSpec(eg.shape, lambda i: (0, 0, 0)),
                          pl.BlockSpec(wa.shape, lambda i: (0, 0)),
                          pl.BlockSpec(wb.shape, lambda i: (0, 0))],
        out_specs=row(d),
        compiler_params=_cparams(("parallel",)),
        name="out0",
    )(*args, gates, eg, wa, wb)


def _ffn_kernel(x_ref, g_ref, wg_ref, wu_ref, wd_ref, o_ref, h_sc, acc_sc):
    f = pl.program_id(1)

    @pl.when(f == 0)
    def _():
        x = x_ref[...]
        ms = jnp.mean(x * x, axis=-1, keepdims=True)
        h_sc[...] = (x * lax.rsqrt(ms + RMS_EPS) * g_ref[...]).astype(BF16)
        acc_sc[...] = x

    h = h_sc[...]
    a = _dot(h, wg_ref[...])
    u = _dot(h, wu_ref[...])
    act = (a * jax.nn.sigmoid(a) * u).astype(BF16)
    acc_sc[...] += _dot(act, wd_ref[...])

    @pl.when(f == pl.num_programs(1) - 1)
    def _():
        o_ref[...] = acc_sc[...]


def _ffn(x, g, wg, wu, wd, tm=512, nf=2):
    m, d = x.shape
    fdim = wg.shape[1]
    tf = fdim // nf
    return pl.pallas_call(
        _ffn_kernel,
        out_shape=jax.ShapeDtypeStruct((m, d), F32),
        grid=(m // tm, nf),
        in_specs=[
            pl.BlockSpec((tm, d), lambda i, f: (i, 0)),
            pl.BlockSpec((1, d), lambda i, f: (0, 0)),
            pl.BlockSpec((d, tf), lambda i, f: (0, f)),
            pl.BlockSpec((d, tf), lambda i, f: (0, f)),
            pl.BlockSpec((tf, d), lambda i, f: (f, 0)),
        ],
        out_specs=pl.BlockSpec((tm, d), lambda i, f: (i, 0)),
        scratch_shapes=[pltpu.VMEM((tm, d), BF16), pltpu.VMEM((tm, d), F32)],
        compiler_params=_cparams(("parallel", "arbitrary")),
        name="ffn",
    )(x, g, wg, wu, wd)


def _out1_kernel(*refs, bounds):
    np_ = len(bounds)
    x_ref = refs[0]
    attn = _row_part(refs[1:1 + np_], bounds)
    w_ref, g_ref, rwh_ref, rwl_ref, rb_ref, xo_ref, h_ref, r_ref = refs[1 + np_:]
    x = x_ref[...] + _dot(attn, w_ref[...])
    xo_ref[...] = x
    ms = jnp.mean(x * x, axis=-1, keepdims=True)
    h = x * lax.rsqrt(ms + RMS_EPS) * g_ref[...]
    h_ref[...] = h
    hi, lo = _split_bf16(h)
    logits = _dot(hi, rwh_ref[...]) + _dot(lo, rwh_ref[...]) + _dot(hi, rwl_ref[...]) + rb_ref[...]
    lane = lax.broadcasted_iota(jnp.int32, logits.shape, 1)
    logits = jnp.where(lane < N_EXPERTS, logits, -jnp.inf)
    m1 = jnp.max(logits, axis=-1, keepdims=True)
    i1 = jnp.min(jnp.where(logits == m1, lane, LANES), axis=-1, keepdims=True)
    rest = jnp.where(lane == i1, -jnp.inf, logits)
    m2 = jnp.max(rest, axis=-1, keepdims=True)
    i2 = jnp.min(jnp.where(rest == m2, lane, LANES), axis=-1, keepdims=True)
    e2 = jnp.exp(m2 - m1)
    g1 = 1.0 / (1.0 + e2)
    g2 = e2 / (1.0 + e2)
    r = jnp.where(lane == 0, i1.astype(F32), 0.0)
    r = jnp.where(lane == 1, i2.astype(F32), r)
    r = jnp.where(lane == 2, g1, r)
    r = jnp.where(lane == 3, g2, r)
    r_ref[...] = r


def _out1(x, o_parts, w, g, rwh, rwl, rb, tm=512):
    m, d = x.shape
    row = lambda wd: pl.BlockSpec((tm, wd), lambda i: (i, 0))
    full = lambda a: pl.BlockSpec(a.shape, lambda i: (0, 0))
    o_specs, bounds = _row_part_specs(o_parts, tm)
    return pl.pallas_call(
        functools.partial(_out1_kernel, bounds=bounds),
        out_shape=[jax.ShapeDtypeStruct((m, d), F32), jax.ShapeDtypeStruct((m, d), F32),
                   jax.ShapeDtypeStruct((m, LANES), F32)],
        grid=(m // tm,),
        in_specs=[row(d)] + o_specs + [full(w), full(g), full(rwh), full(rwl), full(rb)],
        out_specs=[row(d), row(d), row(LANES)],
        compiler_params=_cparams(("parallel",)),
        name="out1",
    )(x, *o_parts, w, g, rwh, rwl, rb)


def _moe_kernel(be_ref, nb_ref, x_ref, wg_ref, wu_ref, wd_ref, o_ref, acc_sc):
    b = pl.program_id(0)
    f = pl.program_id(1)
    used = b < nb_ref[0]

    @pl.when(f == 0)
    def _():
        acc_sc[...] = jnp.zeros_like(acc_sc)

    @pl.when(used)
    def _():
        x = x_ref[...]
        a = _dot(x, wg_ref[0])
        u = _dot(x, wu_ref[0])
        act = (a * jax.nn.sigmoid(a) * u).astype(BF16)
        acc_sc[...] += _dot(act, wd_ref[0])

    @pl.when(f == pl.num_programs(1) - 1)
    def _():
        o_ref[...] = acc_sc[...]


def _moe(xs, blk_e, n_used, wg, wu, wd, tm, tf):
    cap, d = xs.shape
    fdim = wg.shape[2]
    grid_spec = pltpu.PrefetchScalarGridSpec(
        num_scalar_prefetch=2,
        grid=(cap // tm, fdim // tf),
        in_specs=[
            pl.BlockSpec((tm, d), lambda b, f, be, nb: (b, 0)),
            pl.BlockSpec((1, d, tf), lambda b, f, be, nb: (be[b], 0, f)),
            pl.BlockSpec((1, d, tf), lambda b, f, be, nb: (be[b], 0, f)),
            pl.BlockSpec((1, tf, d), lambda b, f, be, nb: (be[b], f, 0)),
        ],
        out_specs=pl.BlockSpec((tm, d), lambda b, f, be, nb: (b, 0)),
        scratch_shapes=[pltpu.VMEM((tm, d), F32)],
    )
    return pl.pallas_call(
        _moe_kernel,
        out_shape=jax.ShapeDtypeStruct((cap, d), F32),
        grid_spec=grid_spec,
        compiler_params=_cparams(("parallel", "arbitrary")),
        name="moe",
    )(blk_e, n_used, xs, wg, wu, wd)


def _start_rows(idx_ref, src_ref, dst, sem, base, n_rows):
    def start(r, c):
        pltpu.make_async_copy(src_ref.at[pl.ds(idx_ref[base + r], 1)], dst.at[pl.ds(r, 1)], sem).start()
        return c

    lax.fori_loop(0, n_rows, start, 0, unroll=8)


def _wait_rows(src_ref, dst, sem, n_rows):
    def wait(r, c):
        pltpu.make_async_copy(src_ref.at[pl.ds(0, 1)], dst.at[pl.ds(r, 1)], sem).wait()
        return c

    lax.fori_loop(0, n_rows, wait, 0, unroll=8)


def _dispatch_kernel(idx_ref, h_ref, o_ref, rows_sc, sem, *, tm):
    _start_rows(idx_ref, h_ref, rows_sc, sem.at[0], pl.program_id(0) * tm, tm)
    _wait_rows(h_ref, rows_sc, sem.at[0], tm)
    o_ref[...] = rows_sc[...].astype(o_ref.dtype)


def _dispatch(row_tok, h, tm):
    cap = row_tok.shape[0]
    d = h.shape[1]
    grid_spec = pltpu.PrefetchScalarGridSpec(
        num_scalar_prefetch=1,
        grid=(cap // tm,),
        in_specs=[pl.BlockSpec(memory_space=pl.ANY)],
        out_specs=pl.BlockSpec((tm, d), lambda b, idx: (b, 0)),
        scratch_shapes=[pltpu.VMEM((tm, d), h.dtype), pltpu.SemaphoreType.DMA((1,))],
    )
    return pl.pallas_call(
        functools.partial(_dispatch_kernel, tm=tm),
        out_shape=jax.ShapeDtypeStruct((cap, d), BF16),
        grid_spec=grid_spec,
        compiler_params=_cparams(("arbitrary",)),
        name="moe_dispatch",
    )(row_tok, h)


def _combine_kernel(dest_ref, x_ref, r_ref, ys_ref, o_ref, rows_sc, sem, *, tm, n_tok):
    t0 = pl.program_id(0) * tm
    for k in range(2):
        _start_rows(dest_ref, ys_ref, rows_sc.at[k], sem.at[0], k * n_tok + t0, tm)
    for k in range(2):
        _wait_rows(ys_ref, rows_sc.at[k], sem.at[0], tm)
    r = r_ref[...]
    o_ref[...] = x_ref[...] + (r[:, 2:3] * rows_sc[0] + r[:, 3:4] * rows_sc[1])


def _combine(dest, x, r, ys, tm=512):
    m, d = x.shape
    grid_spec = pltpu.PrefetchScalarGridSpec(
        num_scalar_prefetch=1,
        grid=(m // tm,),
        in_specs=[
            pl.BlockSpec((tm, d), lambda b, idx: (b, 0)),
            pl.BlockSpec((tm, LANES), lambda b, idx: (b, 0)),
            pl.BlockSpec(memory_space=pl.ANY),
        ],
        out_specs=pl.BlockSpec((tm, d), lambda b, idx: (b, 0)),
        scratch_shapes=[pltpu.VMEM((2, tm, d), F32), pltpu.SemaphoreType.DMA((1,))],
    )
    return pl.pallas_call(
        functools.partial(_combine_kernel, tm=tm, n_tok=m),
        out_shape=jax.ShapeDtypeStruct((m, d), F32),
        grid_spec=grid_spec,
        compiler_params=_cparams(("arbitrary",)),
        name="moe_combine",
    )(dest, x, r, ys)


def _pad_heads_cols(w, heads):
    d = w.shape[0]
    w = w.reshape(d, heads, HEAD_DIM)
    return jnp.pad(w, ((0, 0), (0, 0), (0, LANES - HEAD_DIM))).reshape(d, heads * LANES)


def _pad_heads_rows(w, heads):
    d = w.shape[1]
    w = w.reshape(heads, HEAD_DIM, d)
    return jnp.pad(w, ((0, 0), (LANES - HEAD_DIM, 0), (0, 0))).reshape(heads * LANES, d)


def _head_gain(gain, heads, scale):
    g = jnp.pad(gain.astype(F32) * scale, (0, LANES - HEAD_DIM))
    return jnp.tile(g, heads)


def _key_gain(gain):
    return jnp.concatenate([gain.astype(F32), jnp.ones((HEAD_DIM,), F32)])


def _cover_t(n_cmp_pad, n_cmp, n_blocks, nb_pad, row_off):
    m = jnp.arange(n_cmp_pad)
    c_start = m * CMP_STRIDE
    c_end = c_start + CMP_BLOCK - 1
    jb = jnp.arange(nb_pad) - row_off
    b_start = jb * SLC_BLOCK
    b_end = b_start + SLC_BLOCK - 1
    cov = (c_start[None, :] <= b_end[:, None]) & (c_end[None, :] >= b_start[:, None])
    cov = cov & (m[None, :] < n_cmp) & (jb[:, None] >= 0) & (jb[:, None] < n_blocks)
    return cov.astype(BF16)


def _route(r, n_tok, tm):
    top_e = r[:, 0:2].astype(jnp.int32)
    n_asg = n_tok * 2
    flat_e = top_e.reshape(n_asg)
    onehot = (flat_e[:, None] == jnp.arange(N_EXPERTS)[None, :]).astype(jnp.int32)
    csum = jnp.cumsum(onehot, axis=0)
    counts = csum[-1]
    padded = (counts + tm - 1) // tm * tm
    start = jnp.cumsum(counts) - counts
    pend = jnp.cumsum(padded)
    pstart = pend - padded
    dest = jnp.sum(onehot * (pstart[None, :] + csum - onehot), axis=1)
    n_blk = n_asg // tm + N_EXPERTS
    cap = n_blk * tm
    blk_e = jnp.minimum(jnp.sum(jnp.arange(n_blk)[:, None] * tm >= pend[None, :], axis=1), N_EXPERTS - 1)
    order = jnp.argsort(flat_e, stable=True)
    k = jnp.arange(cap, dtype=jnp.int32) - jnp.repeat(pstart[blk_e], tm)
    src = jnp.clip(jnp.repeat(start[blk_e], tm) + k, 0, n_asg - 1)
    row_tok = jnp.where(k < jnp.repeat(counts[blk_e], tm), order[src] // 2, 0).astype(jnp.int32)
    n_used = (pend[-1] // tm).astype(jnp.int32).reshape(1)
    dest_t = dest.astype(jnp.int32).reshape(n_tok, 2).T.reshape(n_asg)
    return dest_t, row_tok, blk_e.astype(jnp.int32), n_used


def kernel(x_prompt, x_sample, state_pool, cache_nsa_cmp, cache_nsa_slc, cache_nsa_win, cache_swa, page_table, norm0_mix, w_in0, pool_w, pool_scale, nsa_q_gain, nsa_k_gain, cmp_k_w1, cmp_k_w2, cmp_k_pe, cmp_v_w1, cmp_v_w2, cmp_v_pe, w_out0, norm0_ffn, ffn_w_gate, ffn_w_up, ffn_w_down, norm1_mix, w_in1, swa_q_gain, swa_k_gain, swa_sinks, w_out1, norm1_ffn, router_w, router_b, moe_w_gate, moe_w_up, moe_w_down):
    nb, t, d = x_prompt.shape
    ns, s_len, _ = x_sample.shape
    mp = nb * t
    msz = ns * s_len
    past = page_table.shape[1] * PAGE_SIZE
    total = past + s_len
    g = KV_GROUPS

    x_parts = (x_prompt.reshape(mp, d), x_sample.reshape(msz, d))

    c_q = POOL_CH
    c_kv = c_q + NSA_HEADS * HEAD_DIM
    c_gate = c_kv + 3 * KV_COLS
    n_gate = 3 * NSA_HEADS
    w0 = jnp.concatenate([
        w_in0[:, :c_q],
        _pad_heads_cols(w_in0[:, c_q:c_kv], NSA_HEADS),
        w_in0[:, c_kv:c_gate],
        jnp.pad(w_in0[:, c_gate:], ((0, 0), (0, LANES - n_gate))),
    ], axis=1).astype(BF16)
    qw = NSA_HEADS * LANES
    aux0 = jnp.concatenate([
        jnp.ones((POOL_CH,), F32),
        _head_gain(nsa_q_gain, NSA_HEADS, ATTN_SCALE),
        jnp.ones((KV_COLS,), F32),
        jnp.tile(_key_gain(nsa_k_gain[1]), g),
        jnp.tile(_key_gain(nsa_k_gain[2]), g),
        jnp.ones((LANES,), F32),
    ]).reshape(1, -1)
    o0 = POOL_CH
    o1 = o0 + qw
    o2 = o1 + KV_COLS
    o3 = o2 + KV_COLS
    o4 = o3 + KV_COLS
    segs0 = (("raw", 0, POOL_CH), ("qnorm", o0, qw), ("raw", o1, KV_COLS), ("kvnorm", o2, KV_COLS),
             ("kvnorm", o3, KV_COLS), ("sigmoid", o4, LANES))
    outs0 = ((POOL_CH, F32), (qw, F32), (KV_COLS, F32), (KV_COLS, F32), (KV_COLS, BF16),
             (KV_COLS, F32), (KV_COLS, BF16), (LANES, F32))
    u_all, q_all, cmp_all, slc_all, slc_bf, win_all, win_bf, gates_all = _proj(
        x_parts, norm0_mix.reshape(1, d), w0, aux0, segs0, outs0, 512, "proj0")

    pool_w_bf = pool_w.astype(BF16)
    pool_scale2 = pool_scale.reshape(1, POOL_CH).astype(F32)

    u_p = u_all[:mp].reshape(nb, t, POOL_CH)
    u_s = u_all[mp:].reshape(ns, s_len, POOL_CH)
    pool_o_p = _pool_prompt(u_p, pool_w_bf, pool_scale2)
    u_ext = jnp.concatenate([state_pool.astype(F32), u_s], axis=1)
    x_ext = jnp.pad(u_ext, ((0, 0), (1, 0), (0, 0)))
    pool_o_s = _pool_sample(x_ext, pool_w_bf, pool_scale2, s_len)
    pool_parts = (pool_o_p.reshape(mp, POOL_CH), pool_o_s.reshape(msz, POOL_CH).astype(BF16))

    def w1_parts(w1):
        return w1.reshape(2, CMP_STRIDE, HEAD_DIM, CMP_HIDDEN)

    w_sel = jnp.stack([w1_parts(cmp_k_w1), w1_parts(cmp_v_w1)] * g, axis=0)
    w1_bd = jnp.einsum("cpjdh,ce->jcdpeh", w_sel, jnp.eye(2 * g, dtype=F32))
    w1_bd = w1_bd.reshape(CMP_STRIDE, KV_COLS, 2 * 2 * g * CMP_HIDDEN).astype(BF16)
    w2_sel = jnp.stack([cmp_k_w2, cmp_v_w2] * g, axis=0)
    w2_bd = jnp.einsum("chd,ce->ched", w2_sel, jnp.eye(2 * g, dtype=F32))
    w2_bd = w2_bd.reshape(2 * g * CMP_HIDDEN, KV_COLS).astype(BF16)
    pek = jnp.pad(cmp_k_pe.reshape(1, -1), ((0, 7), (0, 0))).astype(BF16)
    pev = jnp.pad(cmp_v_pe.reshape(1, -1), ((0, 7), (0, 0))).astype(BF16)
    w1k = cmp_k_w1.astype(BF16)
    w1v = cmp_v_w1.astype(BF16)
    kg0 = _key_gain(nsa_k_gain[0]).reshape(1, LANES)
    seg_cols = CMP_STRIDE * KV_COLS
    half = 2 * g * CMP_HIDDEN

    n_seg_p = t // CMP_STRIDE
    pa_p, pb_p = _compress1(cmp_all[:mp].reshape(nb * n_seg_p, seg_cols), w1_bd, 256)
    kcv_p = _compress2(pa_p.reshape(nb, n_seg_p, half), pb_p.reshape(nb, n_seg_p, half),
                       jnp.zeros((nb, 8, half), F32), pek, pev, w1k, w1v, w2_bd, kg0)
    n_seg_s = past // CMP_STRIDE
    pad_new = -(-total // CMP_STRIDE) * CMP_STRIDE - past
    assert pad_new == CMP_STRIDE, "the new rows must fit one segment"
    new_seg = jnp.pad(cmp_all[mp:].reshape(ns, s_len * KV_COLS), ((0, 0), (0, (pad_new - s_len) * KV_COLS)))
    _, pb_new = _compress1(new_seg, w1_bd, ns)
    pb_new = jnp.pad(pb_new.reshape(ns, 1, half), ((0, 0), (0, 7), (0, 0)))
    w_kv = jnp.stack([w1_parts(cmp_k_w1), w1_parts(cmp_v_w1)], axis=0)
    w1_gs = jnp.einsum("kpjdh,ke->jkdpeh", w_kv, jnp.eye(2, dtype=F32))
    w1_gs = w1_gs.reshape(CMP_STRIDE * 2 * HEAD_DIM, 2 * 2 * CMP_HIDDEN).astype(BF16)
    kcv_s = _paged_compress(page_table, _pages_t(cache_nsa_cmp), pb_new, w1_gs, pek, pev, w1k, w1v, w2_bd, kg0)

    hg0 = NSA_HEADS // g
    q_p = q_all[:mp].reshape(nb, t, qw)
    q_s = q_all[mp:].reshape(ns, s_len, qw)
    nblk_p = t // SLC_BLOCK
    cov_p = _cover_t(n_seg_p, n_seg_p - 1, nblk_p, LANES, HEAD_DIM)
    o_cmp_p, selb_p = _cmp_attn(q_p, kcv_p, cov_p, hg=hg0, tq=256, bn=1, q_off=0, n_blocks=nblk_p,
                                out_dtype=BF16, row_off=HEAD_DIM, as_bias=True)
    nblk_s = -(-total // SLC_BLOCK)
    nbp_s = -(-nblk_s // LANES) * LANES
    n_cmp_s = -(-total // CMP_STRIDE) - 1
    cov_s = _cover_t(n_seg_s, n_cmp_s, nblk_s, nbp_s, 0)
    o_cmp_s, sel_s = _cmp_attn(q_s, kcv_s, cov_s, hg=hg0, tq=s_len, bn=LANES // s_len, q_off=past,
                               n_blocks=nblk_s, out_dtype=F32, row_off=0, as_bias=False)

    o_slc_p = _flash(q_all, slc_bf, n=nb, t=t, hg=hg0, tq=512, window=None, selb=selb_p)
    tk_s = -(-total // PAGE_SIZE) * PAGE_SIZE
    expand_s = (jnp.arange(tk_s)[None, :] // SLC_BLOCK == jnp.arange(nbp_s)[:, None]).astype(BF16)
    new_slc_t = jnp.pad(jnp.transpose(slc_all[mp:].reshape(ns, s_len, KV_COLS), (0, 2, 1)),
                        ((0, 0), (0, 0), (0, PAGE_SIZE - s_len)))
    o_slc_s = _paged_slc(page_table, _pages_t(cache_nsa_slc), q_s, new_slc_t, sel_s, expand_s, hg=hg0, q_off=past)

    o_win_p = _flash(q_all, win_bf, n=nb, t=t, hg=hg0, tq=256, window=NSA_WINDOW)
    wb_len = cache_nsa_win.shape[1]
    tk_w = -(-(wb_len + s_len) // 16) * 16
    kv_win_s = jnp.concatenate([
        cache_nsa_win.astype(BF16).reshape(ns, wb_len, KV_COLS), win_bf[mp:].reshape(ns, s_len, KV_COLS),
        jnp.zeros((ns, tk_w - wb_len - s_len, KV_COLS), BF16)], axis=1)
    o_win_s = _attn(q_s, kv_win_s, hg=hg0, window=NSA_WINDOW, q_off=past, k_off=past - wb_len)

    parts = lambda a, b: (a.reshape(mp, -1), b.reshape(msz, -1).astype(BF16))
    eg = (jnp.arange(LANES)[None, :, None] ==
          (jnp.arange(3)[:, None, None] * NSA_HEADS + jnp.arange(qw)[None, None, :] // LANES)).astype(BF16)
    wa0 = w_out0[:POOL_CH].astype(BF16)
    wb0 = _pad_heads_rows(w_out0[POOL_CH:], NSA_HEADS).astype(BF16)
    x1 = _out0(x_parts, pool_parts, parts(o_cmp_p, o_cmp_s), parts(o_slc_p, o_slc_s), parts(o_win_p, o_win_s),
               gates_all, eg, wa0, wb0)
    x2 = _ffn(x1, norm0_ffn.reshape(1, d), ffn_w_gate.astype(BF16), ffn_w_up.astype(BF16),
              ffn_w_down.astype(BF16))

    c_q1 = SWA_HEADS * HEAD_DIM
    qw1 = SWA_HEADS * LANES
    w1p = jnp.concatenate([_pad_heads_cols(w_in1[:, :c_q1], SWA_HEADS), w_in1[:, c_q1:]], axis=1).astype(BF16)
    aux1 = jnp.concatenate([_head_gain(swa_q_gain, SWA_HEADS, ATTN_SCALE),
                            jnp.tile(_key_gain(swa_k_gain), g)]).reshape(1, -1)
    segs1 = (("qnorm", 0, qw1), ("kvnorm", qw1, KV_COLS))
    outs1 = ((qw1, F32), (KV_COLS, F32), (KV_COLS, BF16))
    q1_all, swa_all, swa_bf = _proj((x2,), norm1_mix.reshape(1, d), w1p, aux1, segs1, outs1, 512, "proj1")
    hg1 = SWA_HEADS // g
    sinks = swa_sinks.astype(F32)
    q1_s = q1_all[mp:].reshape(ns, s_len, qw1)
    o1_p = _flash(q1_all, swa_bf, n=nb, t=t, hg=hg1, tq=128, window=SWA_WINDOW, sinks=sinks)
    sb_len = cache_swa.shape[1]
    tk_1 = -(-(sb_len + s_len) // 16) * 16
    kv_swa_s = jnp.concatenate([
        cache_swa.astype(BF16).reshape(ns, sb_len, KV_COLS), swa_bf[mp:].reshape(ns, s_len, KV_COLS),
        jnp.zeros((ns, tk_1 - sb_len - s_len, KV_COLS), BF16)], axis=1)
    o1_s = _attn(q1_s, kv_swa_s, hg=hg1, window=SWA_WINDOW, q_off=past, k_off=past - sb_len, sinks=sinks)

    rw = jnp.pad(router_w.astype(F32), ((0, 0), (0, LANES - N_EXPERTS)))
    rwh, rwl = _split_bf16(rw)
    rb = jnp.pad(router_b.astype(F32), (0, LANES - N_EXPERTS)).reshape(1, LANES)
    x3, h3, r = _out1(x2, parts(o1_p, o1_s), _pad_heads_rows(w_out1, SWA_HEADS).astype(BF16),
                      norm1_ffn.reshape(1, d), rwh, rwl, rb)
    m_all = mp + msz
    tm_moe = 512
    dest_t, row_tok, blk_e, n_used = _route(r, m_all, tm_moe)
    n_blk = row_tok.shape[0] // tm_moe
    xs = _dispatch(row_tok, h3, tm_moe * max(k for k in (4, 2, 1) if n_blk % k == 0))
    ys = _moe(xs, blk_e, n_used, moe_w_gate.astype(BF16), moe_w_up.astype(BF16), moe_w_down.astype(BF16),
              tm_moe, moe_w_gate.shape[2] // 2)
    x4 = _combine(dest_t, x3, r, ys)

    y_p = x4[:mp].reshape(nb, t, d)
    y_s = x4[mp:].reshape(ns, s_len, d)
    rows5 = lambda a, n, l: a.reshape(n, l, g, 2, HEAD_DIM)
    keep_w = min(NSA_WINDOW, t)
    keep_s = min(SWA_WINDOW, t)
    pool_p = u_p[:, t - POOL_STATE:]
    cmp_p = rows5(cmp_all[:mp], nb, t)
    slc_p_out = rows5(slc_all[:mp], nb, t)
    win_p_out = rows5(win_all[:mp], nb, t)[:, t - keep_w:]
    swa_p_out = rows5(swa_all[:mp], nb, t)[:, t - keep_s:]
    pool_s = u_ext[:, -POOL_STATE:]
    cmp_s = rows5(cmp_all[mp:], ns, s_len)
    slc_s_out = rows5(slc_all[mp:], ns, s_len)
    win_s_out = jnp.concatenate([cache_nsa_win.astype(F32), rows5(win_all[mp:], ns, s_len)], axis=1)[:, -wb_len:]
    swa_s_out = jnp.concatenate([cache_swa.astype(F32), rows5(swa_all[mp:], ns, s_len)], axis=1)[:, -sb_len:]
    return (y_p, y_s, pool_p, cmp_p, slc_p_out, win_p_out, swa_p_out, pool_s, cmp_s, slc_s_out, win_s_out,
            swa_s_out)
```

```python
import functools

import jax
import jax.numpy as jnp
from jax import lax
from jax.experimental import pallas as pl
from jax.experimental.pallas import tpu as pltpu

F32 = jnp.float32
BF16 = jnp.bfloat16

LANES = 128
MXU_COLS = 256
HEAD_DIM = 64
D_MODEL = 1024
POOL_WINDOWS = (2, 4, 8, 16)
POOL_CH = 512
POOL_GROUP = 128
POOL_STATE = 15
POOL_HALO = 16
NSA_HEADS = 8
SWA_HEADS = 16
KV_GROUPS = 2
KV_COLS = KV_GROUPS * 2 * HEAD_DIM
CMP_BLOCK = 32
CMP_STRIDE = 16
CMP_HIDDEN = 128
SLC_BLOCK = 64
SLC_SHIFT = SLC_BLOCK.bit_length() - 1
SLC_TOPN = 16
NSA_WINDOW = 512
SWA_WINDOW = 128
PAGE_SIZE = 128
N_EXPERTS = 8
RMS_EPS = 1e-6
NEG_INF = -1e30
FORCE_SCORE = 1e9
PAD_SCORE = -3e38
ATTN_SCALE = HEAD_DIM ** -0.5
VMEM_LIMIT = 56 * 1024 * 1024


def _cparams(sem):
    return pltpu.CompilerParams(dimension_semantics=sem, vmem_limit_bytes=VMEM_LIMIT)


def _split_bf16(x):
    hi = x.astype(BF16)
    lo = (x - hi.astype(F32)).astype(BF16)
    return hi, lo


def _dot(a, b):
    return jnp.dot(a, b, preferred_element_type=F32)


def _dot_nt(a, b):
    return lax.dot_general(a, b, (((1,), (1,)), ((), ())), preferred_element_type=F32)


def _row_part_specs(parts, tm):
    specs, bounds, t0 = [], [], 0
    for a in parts:
        nt = a.shape[0] // tm
        assert nt * tm == a.shape[0]
        specs.append(pl.BlockSpec((tm, a.shape[1]), lambda i, t0=t0, nt=nt: (jnp.clip(i - t0, 0, nt - 1), 0)))
        t0 += nt
        bounds.append(t0)
    return specs, tuple(bounds)


def _row_part(refs, bounds):
    i = pl.program_id(0)
    v = refs[-1][...]
    for k in range(len(refs) - 2, -1, -1):
        v = jnp.where(i < bounds[k], refs[k][...], v)
    return v


def _proj_kernel(*refs, segs, bounds):
    np_ = len(bounds)
    x = _row_part(refs[:np_], bounds)
    g_ref, w_ref, aux_ref = refs[np_:np_ + 3]
    out_refs = refs[np_ + 3:]
    ms = jnp.mean(x * x, axis=-1, keepdims=True)
    h = (x * lax.rsqrt(ms + RMS_EPS) * g_ref[...]).astype(BF16)
    lane = lax.broadcasted_iota(jnp.int32, (1, LANES), 1)
    is_key = lane < HEAD_DIM
    oi = 0
    for kind, c0, width in segs:
        if kind == "qnorm":
            for pair in range(width // MXU_COLS):
                cp = c0 + pair * MXU_COLS
                z = _dot(h, w_ref[:, cp:cp + MXU_COLS])
                for hh in range(MXU_COLS // LANES):
                    zh = z[:, hh * LANES:(hh + 1) * LANES]
                    msq = jnp.sum(zh * zh, axis=-1, keepdims=True) * (1.0 / HEAD_DIM)
                    lo = cp - c0 + hh * LANES
                    gain = aux_ref[:, cp + hh * LANES:cp + (hh + 1) * LANES]
                    out_refs[oi][:, lo:lo + LANES] = zh * lax.rsqrt(msq + RMS_EPS) * gain
            oi += 1
            continue
        z = _dot(h, w_ref[:, c0:c0 + width])
        aux = aux_ref[:, c0:c0 + width]
        if kind == "raw":
            out_refs[oi][...] = z
            oi += 1
        elif kind == "sigmoid":
            out_refs[oi][...] = jax.nn.sigmoid(z)
            oi += 1
        elif kind == "kvnorm":
            for gg in range(width // LANES):
                sl = slice(gg * LANES, (gg + 1) * LANES)
                zb = z[:, sl]
                msq = jnp.sum(jnp.where(is_key, zb * zb, 0.0), axis=-1, keepdims=True) * (1.0 / HEAD_DIM)
                y = jnp.where(is_key, zb * lax.rsqrt(msq + RMS_EPS) * aux[:, sl], zb)
                out_refs[oi][:, sl] = y
                out_refs[oi + 1][:, sl] = y.astype(BF16)
            oi += 2
        else:
            raise ValueError(kind)


def _proj(x_parts, g, w, aux, segs, out_defs, tm, name):
    m = sum(a.shape[0] for a in x_parts)
    d = x_parts[0].shape[1]
    c = w.shape[1]
    out_shape = [jax.ShapeDtypeStruct((m, wd), dt) for wd, dt in out_defs]
    out_specs = [pl.BlockSpec((tm, wd), lambda i: (i, 0)) for wd, _ in out_defs]
    x_specs, bounds = _row_part_specs(x_parts, tm)
    return pl.pallas_call(
        functools.partial(_proj_kernel, segs=segs, bounds=bounds),
        out_shape=out_shape,
        grid=(m // tm,),
        in_specs=x_specs + [
            pl.BlockSpec((1, d), lambda i: (0, 0)),
            pl.BlockSpec((d, c), lambda i: (0, 0)),
            pl.BlockSpec((1, c), lambda i: (0, 0)),
        ],
        out_specs=out_specs,
        compiler_params=_cparams(("parallel",)),
        name=name,
    )(*x_parts, g, w, aux)


def _pool_prompt_kernel(u_ref, halo_ref, pw_ref, ps_ref, o_ref, e_ref, *, tp):
    i = pl.program_id(1)
    e_ref[0:POOL_HALO, :] = jnp.where(i > 0, halo_ref[0], 0.0)
    e_ref[POOL_HALO:, :] = u_ref[0]
    t = i * tp + lax.broadcasted_iota(jnp.int32, (tp, 1), 0)
    for gi, w in enumerate(POOL_WINDOWS):
        sl = slice(gi * POOL_GROUP, (gi + 1) * POOL_GROUP)
        tok = e_ref[POOL_HALO:POOL_HALO + tp, sl]
        s = tok
        for k in range(1, w):
            s = s + e_ref[POOL_HALO - k:POOL_HALO - k + tp, sl]
        cnt = jnp.minimum(t + 1, w).astype(F32)
        dg = s / cnt - tok
        y = _dot(dg.astype(BF16), pw_ref[gi]) * ps_ref[:, sl]
        o_ref[0, :, sl] = y.astype(o_ref.dtype)


def _pool_prompt(u, pool_w, pool_scale, tp=512):
    n, t, c = u.shape
    hb = tp // POOL_HALO
    return pl.pallas_call(
        functools.partial(_pool_prompt_kernel, tp=tp),
        out_shape=jax.ShapeDtypeStruct((n, t, c), BF16),
        grid=(n, t // tp),
        in_specs=[
            pl.BlockSpec((1, tp, c), lambda b, i: (b, i, 0)),
            pl.BlockSpec((1, POOL_HALO, c), lambda b, i: (b, jnp.maximum(i * hb - 1, 0), 0)),
            pl.BlockSpec((len(POOL_WINDOWS), POOL_GROUP, POOL_GROUP), lambda b, i: (0, 0, 0)),
            pl.BlockSpec((1, c), lambda b, i: (0, 0)),
        ],
        out_specs=pl.BlockSpec((1, tp, c), lambda b, i: (b, i, 0)),
        scratch_shapes=[pltpu.VMEM((tp + POOL_HALO, c), F32)],
        compiler_params=_cparams(("parallel", "parallel")),
        name="pool_prompt",
    )(u, u, pool_w, pool_scale)


def _pool_sample_kernel(x_ref, pw_ref, ps_ref, o_ref, *, bn, s_len):
    base = 1 + POOL_STATE
    for gi, w in enumerate(POOL_WINDOWS):
        sl = slice(gi * POOL_GROUP, (gi + 1) * POOL_GROUP)
        tok = x_ref[:, base:base + s_len, sl]
        s = tok
        for k in range(1, w):
            s = s + x_ref[:, base - k:base - k + s_len, sl]
        dg = (s / float(w) - tok).reshape(bn * s_len, POOL_GROUP)
        y = _dot(dg.astype(BF16), pw_ref[gi]) * ps_ref[:, sl]
        o_ref[:, :, sl] = y.reshape(bn, s_len, POOL_GROUP).astype(o_ref.dtype)


def _pool_sample(x_ext, pool_w, pool_scale, s_len, bn=16):
    n, l, c = x_ext.shape
    return pl.pallas_call(
        functools.partial(_pool_sample_kernel, bn=bn, s_len=s_len),
        out_shape=jax.ShapeDtypeStruct((n, s_len, c), F32),
        grid=(n // bn,),
        in_specs=[
            pl.BlockSpec((bn, l, c), lambda b: (b, 0, 0)),
            pl.BlockSpec((len(POOL_WINDOWS), POOL_GROUP, POOL_GROUP), lambda b: (0, 0, 0)),
            pl.BlockSpec((1, c), lambda b: (0, 0)),
        ],
        out_specs=pl.BlockSpec((bn, s_len, c), lambda b: (b, 0, 0)),
        compiler_params=_cparams(("parallel",)),
        name="pool_sample",
    )(x_ext, pool_w, pool_scale)


def _compress1_kernel(x_ref, w_ref, pa_ref, pb_ref):
    acc = None
    for j in range(CMP_STRIDE):
        xj = x_ref[:, j * KV_COLS:(j + 1) * KV_COLS].astype(BF16)
        d = _dot(xj, w_ref[j])
        acc = d if acc is None else acc + d
    half = 4 * CMP_HIDDEN
    pa_ref[...] = acc[:, :half]
    pb_ref[...] = acc[:, half:]


def _compress1(x, w_bd, ts):
    r, c = x.shape
    ts = min(ts, r)
    half = 4 * CMP_HIDDEN
    return pl.pallas_call(
        _compress1_kernel,
        out_shape=[jax.ShapeDtypeStruct((r, half), F32), jax.ShapeDtypeStruct((r, half), F32)],
        grid=(r // ts,),
        in_specs=[
            pl.BlockSpec((ts, c), lambda i: (i, 0)),
            pl.BlockSpec(w_bd.shape, lambda i: (0, 0, 0)),
        ],
        out_specs=[pl.BlockSpec((ts, half), lambda i: (i, 0)), pl.BlockSpec((ts, half), lambda i: (i, 0))],
        compiler_params=_cparams(("parallel",)),
        name="compress1",
    )(x, w_bd)


def _gelu_tanh(x):
    return 0.5 * x * (1.0 + jnp.tanh(0.7978845608028654 * (x + 0.044715 * (x * x * x))))


def _compress_tail(pa, pb, pb_new, pek_ref, pev_ref, w1k_ref, w1v_ref, w2_ref, kg_ref, o_ref):
    s = pa.shape[0]
    row = lax.broadcasted_iota(jnp.int32, (s, 1), 0)
    pb_next = jnp.where(row == s - 1, pb_new, pltpu.roll(pb, s - 1, 0))
    hk = _dot(pek_ref[...], w1k_ref[...])[0:1, :]
    hv = _dot(pev_ref[...], w1v_ref[...])[0:1, :]
    pe_h = jnp.concatenate([hk, hv, hk, hv], axis=1)
    a = _gelu_tanh(pa + pb_next + pe_h)
    o = _dot(a.astype(BF16), w2_ref[...])
    lane = lax.broadcasted_iota(jnp.int32, (1, LANES), 1)
    is_key = lane < HEAD_DIM
    for gg in range(KV_GROUPS):
        sl = slice(gg * LANES, (gg + 1) * LANES)
        ob = o[:, sl]
        msq = jnp.sum(jnp.where(is_key, ob * ob, 0.0), axis=-1, keepdims=True) * (1.0 / HEAD_DIM)
        y = jnp.where(is_key, ob * lax.rsqrt(msq + RMS_EPS) * kg_ref[...], ob)
        o_ref[0, :, sl] = y.astype(o_ref.dtype)


def _compress2_kernel(pa_ref, pb_ref, pbn_ref, pek_ref, pev_ref, w1k_ref, w1v_ref, w2_ref, kg_ref, o_ref):
    _compress_tail(pa_ref[0], pb_ref[0], pbn_ref[0][0:1, :], pek_ref, pev_ref, w1k_ref, w1v_ref, w2_ref,
                   kg_ref, o_ref)


def _pages_t(cache):
    return jnp.transpose(cache, (0, 2, 3, 4, 1)).reshape(cache.shape[0], KV_COLS, cache.shape[1])


def _paged_prefetch(pt_ref, cache_ref, dst, sem, n_pages):
    b = pl.program_id(0)
    slot = b % 2

    def fetch(bb, sl):
        def body(p, c):
            pltpu.make_async_copy(cache_ref.at[pt_ref[bb * n_pages + p]], dst(sl, p), sem.at[sl]).start()
            return c

        lax.fori_loop(0, n_pages, body, 0)

    @pl.when(b == 0)
    def _():
        fetch(0, 0)

    @pl.when(b + 1 < pl.num_programs(0))
    def _():
        fetch(b + 1, 1 - slot)

    def wait(p, c):
        pltpu.make_async_copy(cache_ref.at[0], dst(slot, p), sem.at[slot]).wait()
        return c

    lax.fori_loop(0, n_pages, wait, 0)
    return slot


def _paged_compress_kernel(pt_ref, cache_ref, pbn_ref, w1_ref, pek_ref, pev_ref, w1k_ref, w1v_ref, w2_ref, kg_ref,
                           o_ref, buf, rows_sc, sem, *, n_pages):
    slot = _paged_prefetch(
        pt_ref, cache_ref, lambda sl, p: buf.at[sl, pl.ds(pl.multiple_of(p * KV_COLS, KV_COLS), KV_COLS)],
        sem, n_pages)
    for p in range(n_pages):
        for gg in range(KV_GROUPS):
            f0 = p * KV_COLS + gg * LANES
            rows_sc[gg, p * PAGE_SIZE:(p + 1) * PAGE_SIZE, :] = buf[slot, f0:f0 + LANES, :].T
    n_seg = n_pages * (PAGE_SIZE // CMP_STRIDE)
    hid2 = 2 * CMP_HIDDEN
    pas, pbs = [], []
    for gg in range(KV_GROUPS):
        xg = jnp.concatenate(
            [rows_sc[gg, pl.ds(j, n_seg, stride=CMP_STRIDE), :].astype(BF16)
             for j in range(CMP_STRIDE)], axis=1)
        pg = _dot(xg, w1_ref[...])
        pas.append(pg[:, :hid2])
        pbs.append(pg[:, hid2:])
    _compress_tail(jnp.concatenate(pas, axis=1), jnp.concatenate(pbs, axis=1), pbn_ref[0][0:1, :],
                   pek_ref, pev_ref, w1k_ref, w1v_ref, w2_ref, kg_ref, o_ref)


def _paged_compress(page_table, cache_t, pb_new, w1_gs, pek, pev, w1k, w1v, w2_bd, kg):
    ns, n_pages = page_table.shape
    s = n_pages * (PAGE_SIZE // CMP_STRIDE)
    c = pb_new.shape[-1]
    full2 = lambda b, pt: (0, 0)
    grid_spec = pltpu.PrefetchScalarGridSpec(
        num_scalar_prefetch=1,
        grid=(ns,),
        in_specs=[
            pl.BlockSpec(memory_space=pl.ANY),
            pl.BlockSpec((1, 8, c), lambda b, pt: (b, 0, 0)),
            pl.BlockSpec(w1_gs.shape, full2),
            pl.BlockSpec(pek.shape, full2),
            pl.BlockSpec(pev.shape, full2),
            pl.BlockSpec(w1k.shape, full2),
            pl.BlockSpec(w1v.shape, full2),
            pl.BlockSpec(w2_bd.shape, full2),
            pl.BlockSpec(kg.shape, full2),
        ],
        out_specs=pl.BlockSpec((1, s, KV_COLS), lambda b, pt: (b, 0, 0)),
        scratch_shapes=[pltpu.VMEM((2, n_pages * KV_COLS, PAGE_SIZE), F32),
                        pltpu.VMEM((KV_GROUPS, n_pages * PAGE_SIZE, LANES), F32), pltpu.SemaphoreType.DMA((2,))],
    )
    return pl.pallas_call(
        functools.partial(_paged_compress_kernel, n_pages=n_pages),
        out_shape=jax.ShapeDtypeStruct((ns, s, KV_COLS), BF16),
        grid_spec=grid_spec,
        compiler_params=_cparams(("arbitrary",)),
        name="paged_compress",
    )(page_table.reshape(-1), cache_t, pb_new, w1_gs, pek, pev, w1k, w1v, w2_bd, kg)


def _compress2(pa, pb, pb_new, pek, pev, w1k, w1v, w2_bd, kg):
    n, s, c = pa.shape
    full2 = lambda b: (0, 0)
    return pl.pallas_call(
        _compress2_kernel,
        out_shape=jax.ShapeDtypeStruct((n, s, KV_COLS), BF16),
        grid=(n,),
        in_specs=[
            pl.BlockSpec((1, s, c), lambda b: (b, 0, 0)),
            pl.BlockSpec((1, s, c), lambda b: (b, 0, 0)),
            pl.BlockSpec((1, 8, c), lambda b: (b, 0, 0)),
            pl.BlockSpec(pek.shape, full2),
            pl.BlockSpec(pev.shape, full2),
            pl.BlockSpec(w1k.shape, full2),
            pl.BlockSpec(w1v.shape, full2),
            pl.BlockSpec(w2_bd.shape, full2),
            pl.BlockSpec(kg.shape, full2),
        ],
        out_specs=pl.BlockSpec((1, s, KV_COLS), lambda b: (b, 0, 0)),
        compiler_params=_cparams(("parallel",)),
        name="compress2",
    )(pa, pb, pb_new, pek, pev, w1k, w1v, w2_bd, kg)


def _stack_heads(q, hg):
    return jnp.concatenate([q[:, h * LANES:(h + 1) * LANES] for h in range(hg)], axis=0).astype(BF16)


def _row_qpos(hg, tq, q0):
    r = lax.broadcasted_iota(jnp.int32, (hg * tq, 1), 0)
    return q0 + (r & (tq - 1))


def _store_heads(o_ref, lead, o, hg, tq, head0=0):
    lane = lax.broadcasted_iota(jnp.int32, (1, LANES), 1)
    for h in range(hg):
        blk = jnp.where(lane >= HEAD_DIM, o[h * tq:(h + 1) * tq], 0.0)
        c0 = (head0 + h) * LANES
        o_ref[lead + (slice(None), slice(c0, c0 + LANES))] = blk.astype(o_ref.dtype)


def _cmp_attn_kernel(q_ref, kc_ref, cov_ref, o_ref, sel_ref, *, bn, hg, tq, q_off, n_blocks, row_off, as_bias):
    i = pl.program_id(2)
    s_len = kc_ref.shape[1]
    q0 = q_off + i * tq
    qpos = _row_qpos(hg, tq, q0)
    blk_end = lax.broadcasted_iota(jnp.int32, (1, s_len), 1) * CMP_STRIDE + (CMP_BLOCK - 1)
    cond = blk_end <= qpos
    psums = []
    for b in range(bn):
        qs = _stack_heads(q_ref[b], hg)
        kc = kc_ref[b]
        s = jnp.where(cond, _dot_nt(qs, kc), NEG_INF)
        m = jnp.max(s, axis=-1, keepdims=True)
        p = jnp.where(cond, jnp.exp(s - m), 0.0)
        den = jnp.sum(p, axis=-1, keepdims=True)
        p = p / jnp.maximum(den, 1e-30)
        o = _dot(p.astype(BF16), kc)
        _store_heads(o_ref, (b,), o, hg, tq)
        ps = p[0:tq]
        for h in range(1, hg):
            ps = ps + p[h * tq:(h + 1) * tq]
        psums.append(ps)
    ps = psums[0] if bn == 1 else jnp.concatenate(psums, axis=0)
    hi, lo = _split_bf16(ps)
    cov = cov_ref[...]
    score = _dot_nt(cov, hi) + _dot_nt(cov, lo)
    nb_pad, nl = score.shape
    j = lax.broadcasted_iota(jnp.int32, (nb_pad, nl), 0) - row_off
    t = q0 + (lax.broadcasted_iota(jnp.int32, (1, nl), 1) & (tq - 1))
    cur = t >> SLC_SHIFT
    forced = (j == 0) | (j == cur) | (j == cur - 1)
    valid = j * SLC_BLOCK <= t
    score = jnp.where(valid, jnp.where(forced, FORCE_SCORE, score), NEG_INF)
    is_block = lax.bitcast_convert_type(j, jnp.uint32) < jnp.uint32(n_blocks)
    score = jnp.where(is_block, score, PAD_SCORE)
    hi = min(nb_pad, row_off + -(-n_blocks // 8) * 8)
    sc = score[row_off:hi]
    jj = lax.broadcasted_iota(jnp.int32, sc.shape, 0)
    picked = jnp.zeros(sc.shape, F32)
    for _ in range(min(SLC_TOPN, n_blocks)):
        mx = jnp.max(sc, axis=0, keepdims=True)
        jm = jnp.min(jnp.where(sc == mx, jj, nb_pad), axis=0, keepdims=True)
        pick = jj == jm
        picked = jnp.where(pick, 1.0, picked)
        sc = jnp.where(pick, -jnp.inf, sc)
    sel = jnp.concatenate([jnp.zeros((n, nl), F32) for n in (row_off,) if n] + [picked] +
                          [jnp.zeros((n, nl), F32) for n in (nb_pad - hi,) if n], axis=0)
    if as_bias:
        sel = jnp.where(is_block, (sel - 1.0) * (-NEG_INF), 0.0)
    sel_t = sel.T
    for b in range(bn):
        sel_ref[b, 0] = sel_t[b * tq:(b + 1) * tq]


def _cmp_attn(q, kc, cov_t, *, hg, tq, bn, q_off, n_blocks, out_dtype, row_off, as_bias):
    n, t, hc = q.shape
    s_len = kc.shape[1]
    nb_pad = cov_t.shape[0]
    g = KV_GROUPS
    return pl.pallas_call(
        functools.partial(_cmp_attn_kernel, bn=bn, hg=hg, tq=tq, q_off=q_off, n_blocks=n_blocks,
                          row_off=row_off, as_bias=as_bias),
        out_shape=[jax.ShapeDtypeStruct((n, t, hc), out_dtype), jax.ShapeDtypeStruct((n, g, t, nb_pad), F32)],
        grid=(n // bn, g, t // tq),
        in_specs=[
            pl.BlockSpec((bn, tq, hg * LANES), lambda b, gg, i: (b, i, gg)),
            pl.BlockSpec((bn, s_len, LANES), lambda b, gg, i: (b, 0, gg)),
            pl.BlockSpec((nb_pad, s_len), lambda b, gg, i: (0, 0)),
        ],
        out_specs=[
            pl.BlockSpec((bn, tq, hg * LANES), lambda b, gg, i: (b, i, gg)),
            pl.BlockSpec((bn, 1, tq, nb_pad), lambda b, gg, i: (b, gg, i, 0)),
        ],
        compiler_params=_cparams(("parallel", "parallel", "parallel")),
        name="cmp_attn",
    )(q, kc, cov_t)


def _flash_kernel(*refs, hg, tq, tk, window, use_sel, use_sink):
    refs = list(refs)
    sink_ref = refs.pop(0) if use_sink else None
    q_ref = refs.pop(0)
    kv_ref = refs.pop(0)
    selb_ref = refs.pop(0) if use_sel else None
    o_ref, qs_sc, m_sc, acc_sc = refs
    g = pl.program_id(1)
    i = pl.program_id(2)
    rows = hg * tq
    q0 = i * tq
    for h in range(hg):
        qh = q_ref[:, h * LANES:(h + 1) * LANES]
        if use_sel:
            qh = qh + selb_ref[0, 0]
        qs_sc[h * tq:(h + 1) * tq, :] = qh.astype(BF16)
    m_sc[...] = jnp.full((rows, LANES), NEG_INF, F32)
    acc_sc[...] = jnp.zeros((rows, LANES), F32)

    def tile(k0, masked):
        kv = kv_ref[pl.ds(k0, tk), :]
        klane = lax.broadcasted_iota(jnp.int32, (tk, LANES), 1)
        if use_sel:
            krow = lax.broadcasted_iota(jnp.int32, (tk, LANES), 0)
            onehot = jnp.where(klane - HEAD_DIM == ((k0 + krow) >> SLC_SHIFT), 1.0, 0.0).astype(BF16)
            kaug = jnp.where(klane < HEAD_DIM, kv, onehot)
        else:
            kaug = kv
        vaug = jnp.where(klane < HEAD_DIM, jnp.ones_like(kv), kv)
        s = _dot_nt(qs_sc[...], kaug)
        if masked:
            qpos = q0 + (lax.broadcasted_iota(jnp.int32, (rows, 1), 0) & (tq - 1))
            d = qpos - (k0 + lax.broadcasted_iota(jnp.int32, (1, tk), 1))
            if window is None:
                s = jnp.where(d >= 0, s, NEG_INF)
            else:
                s = jnp.where(lax.bitcast_convert_type(d, jnp.uint32) <= jnp.uint32(window), s, NEG_INF)
        m_prev = m_sc[...]
        m_new = jnp.maximum(m_prev, jnp.max(s, axis=-1, keepdims=True))
        alpha = jnp.exp(m_prev - m_new)
        p = jnp.exp(s - jnp.concatenate([m_new] * (tk // LANES), axis=1))
        acc_sc[...] = alpha * acc_sc[...] + _dot(p.astype(BF16), vaug)
        m_sc[...] = m_new

    if window is None:
        def body(jt, c):
            tile(pl.multiple_of(jt * tk, tk), False)
            return c

        lax.fori_loop(0, i, body, 0)
        tile(pl.multiple_of(i * tk, tk), True)
    else:
        tile(pl.multiple_of(jnp.maximum(q0 - window, 0), LANES), True)

    acc = acc_sc[...]
    den = pltpu.roll(acc, HEAD_DIM, 1)
    m = m_sc[...]
    if use_sink:
        sink = jnp.concatenate([jnp.full((tq, LANES), sink_ref[g * hg + h], F32) for h in range(hg)], axis=0)
        m_fin = jnp.maximum(m, sink)
        w = jnp.exp(m - m_fin)
        o = acc * w / jnp.maximum(den * w + jnp.exp(sink - m_fin), 1e-30)
    else:
        o = jnp.where(m > 0.5 * NEG_INF, acc / jnp.maximum(den, 1e-30), 0.0)
    lane = lax.broadcasted_iota(jnp.int32, (1, LANES), 1)
    o = jnp.where(lane >= HEAD_DIM, o, 0.0)
    for h in range(hg):
        o_ref[:, h * LANES:(h + 1) * LANES] = o[h * tq:(h + 1) * tq].astype(o_ref.dtype)


def _flash(q, kv, *, n, t, hg, tq, window, selb=None, sinks=None):
    hc = q.shape[1]
    nq = t // tq
    g = KV_GROUPS
    use_sel = selb is not None
    use_sink = sinks is not None
    rows = hg * tq
    tk = tq if window is None else tq + window
    if use_sel:
        assert t // SLC_BLOCK <= LANES - HEAD_DIM, "selection bias needs one upper query lane per block"
    in_specs = []
    args = []
    if use_sink:
        in_specs.append(pl.BlockSpec(memory_space=pltpu.SMEM))
        args.append(sinks)
    in_specs += [
        pl.BlockSpec((tq, hg * LANES), lambda b, gg, i: (b * nq + i, gg)),
        pl.BlockSpec((t, LANES), lambda b, gg, i: (b, gg)),
    ]
    args += [q, kv]
    if use_sel:
        in_specs.append(pl.BlockSpec((1, 1, tq, LANES), lambda b, gg, i: (b, gg, i, 0)))
        args.append(selb)
    return pl.pallas_call(
        functools.partial(_flash_kernel, hg=hg, tq=tq, tk=tk, window=window, use_sel=use_sel,
                          use_sink=use_sink),
        out_shape=jax.ShapeDtypeStruct((n * t, hc), BF16),
        grid=(n, g, nq),
        in_specs=in_specs,
        out_specs=pl.BlockSpec((tq, hg * LANES), lambda b, gg, i: (b * nq + i, gg)),
        scratch_shapes=[pltpu.VMEM((rows, LANES), BF16), pltpu.VMEM((rows, LANES), F32),
                        pltpu.VMEM((rows, LANES), F32)],
        compiler_params=_cparams(("parallel", "parallel", "parallel")),
        name="flash",
    )(*args)


def _attn_kernel(*refs, hg, tq, bn, window, q_off, k_off, use_sink):
    refs = list(refs)
    sink_ref = refs.pop(0) if use_sink else None
    q_ref = refs.pop(0)
    kv_ref = refs.pop(0)
    (o_ref,) = refs
    tk = kv_ref.shape[1]
    qpos = _row_qpos(hg, tq, q_off)
    d = qpos - (k_off + lax.broadcasted_iota(jnp.int32, (1, tk), 1))
    if window is None:
        valid = d >= 0
    else:
        valid = lax.bitcast_convert_type(d, jnp.uint32) <= jnp.uint32(window)
    for g in range(KV_GROUPS):
        if use_sink:
            sink = jnp.concatenate([jnp.full((tq, 1), sink_ref[g * hg + h], F32) for h in range(hg)], axis=0)
        for b in range(bn):
            qs = _stack_heads(q_ref[b, :, g * hg * LANES:(g + 1) * hg * LANES], hg)
            kv = kv_ref[b, :, g * LANES:(g + 1) * LANES]
            s = jnp.where(valid, _dot_nt(qs, kv), NEG_INF)
            m = jnp.max(s, axis=-1, keepdims=True)
            if use_sink:
                m = jnp.maximum(m, sink)
            p = jnp.exp(s - m)
            den = jnp.sum(p, axis=-1, keepdims=True)
            if use_sink:
                den = den + jnp.exp(sink - m)
            o = jnp.where(m > 0.5 * NEG_INF, _dot(p.astype(BF16), kv) / jnp.maximum(den, 1e-30), 0.0)
            _store_heads(o_ref, (b,), o, hg, tq, head0=g * hg)


def _attn(q, kv, *, hg, window, q_off, k_off, sinks=None, bn=8):
    n, tq, hc = q.shape
    tkv = kv.shape[1]
    use_sink = sinks is not None
    in_specs = []
    args = []
    if use_sink:
        in_specs.append(pl.BlockSpec(memory_space=pltpu.SMEM))
        args.append(sinks)
    in_specs += [
        pl.BlockSpec((bn, tq, hc), lambda b: (b, 0, 0)),
        pl.BlockSpec((bn, tkv, KV_COLS), lambda b: (b, 0, 0)),
    ]
    args += [q, kv]
    return pl.pallas_call(
        functools.partial(_attn_kernel, hg=hg, tq=tq, bn=bn, window=window, q_off=q_off, k_off=k_off,
                          use_sink=use_sink),
        out_shape=jax.ShapeDtypeStruct((n, tq, hc), F32),
        grid=(n // bn,),
        in_specs=in_specs,
        out_specs=pl.BlockSpec((bn, tq, hc), lambda b: (b, 0, 0)),
        compiler_params=_cparams(("parallel",)),
        name="attn",
    )(*args)


def _paged_slc_kernel(pt_ref, cache_ref, q_ref, new_ref, selb_ref, oh_ref, o_ref, buf, sem, *, n_pages, hg, tq,
                      q_off):
    past = n_pages * PAGE_SIZE
    tk = buf.shape[2]
    rng_keys = (LANES - HEAD_DIM) * SLC_BLOCK
    slot = _paged_prefetch(
        pt_ref, cache_ref, lambda sl, p: buf.at[sl, :, pl.ds(pl.multiple_of(p * PAGE_SIZE, PAGE_SIZE), PAGE_SIZE)],
        sem, n_pages)
    buf[slot, :, past:] = new_ref[0]
    kv_t = buf[slot].astype(BF16)
    rows = hg * tq
    zero = jnp.zeros((rows, LANES), F32)
    q = q_ref[0]
    stack = lambda g: jnp.concatenate([q[:, (g * hg + h) * LANES:(g * hg + h + 1) * LANES] for h in range(hg)],
                                      axis=0)
    oh = oh_ref[...]
    kaug_t = jnp.concatenate([kv_t[0:HEAD_DIM], oh, kv_t[LANES:LANES + HEAD_DIM], oh], axis=0)
    qg = [stack(g) for g in range(KV_GROUPS)]
    s_parts = []
    for rr in range(-(-tk // rng_keys)):
        qa = [qg[g] + jnp.concatenate([selb_ref[0, g, rr]] * hg, axis=0) for g in range(KV_GROUPS)]
        q2 = jnp.concatenate([jnp.concatenate([qa[0], zero], axis=1),
                              jnp.concatenate([zero, qa[1]], axis=1)], axis=0).astype(BF16)
        s_parts.append(_dot(q2, kaug_t[:, rr * rng_keys:min(tk, (rr + 1) * rng_keys)]))
    s = jnp.concatenate(s_parts, axis=1)
    r = lax.broadcasted_iota(jnp.int32, (2 * rows, 1), 0)
    qpos = q_off + (r & (tq - 1))
    s = jnp.where(qpos - lax.broadcasted_iota(jnp.int32, (1, tk), 1) >= 0, s, NEG_INF)
    m = jnp.max(s, axis=-1, keepdims=True)
    p = jnp.exp(s - m)
    den = jnp.sum(p, axis=-1, keepdims=True)
    o = jnp.where(m > 0.5 * NEG_INF, _dot_nt(p.astype(BF16), kv_t) / jnp.maximum(den, 1e-30), 0.0)
    lane = lax.broadcasted_iota(jnp.int32, (1, LANES), 1)
    for g in range(KV_GROUPS):
        for h in range(hg):
            blk = o[g * rows + h * tq:g * rows + (h + 1) * tq, g * LANES:(g + 1) * LANES]
            c0 = (g * hg + h) * LANES
            o_ref[0, :, c0:c0 + LANES] = jnp.where(lane >= HEAD_DIM, blk, 0.0)


def _paged_slc(page_table, cache_t, q, new_t, sel, *, hg, q_off):
    ns, n_pages = page_table.shape
    _, tq, hc = q.shape
    tk = (n_pages + 1) * PAGE_SIZE
    assert new_t.shape == (ns, KV_COLS, PAGE_SIZE)
    n_slot = LANES - HEAD_DIM
    n_rng = -(-tk // (n_slot * SLC_BLOCK))
    assert n_rng * n_slot <= sel.shape[-1]
    bias = (sel - 1.0) * (-NEG_INF)
    selb = jnp.stack([jnp.pad(bias[..., rr * n_slot:(rr + 1) * n_slot], ((0, 0), (0, 0), (0, 0), (HEAD_DIM, 0)))
                      for rr in range(n_rng)], axis=2)
    onehot = (((jnp.arange(tk)[None, :] >> SLC_SHIFT) & (n_slot - 1)) == jnp.arange(n_slot)[:, None]).astype(BF16)
    grid_spec = pltpu.PrefetchScalarGridSpec(
        num_scalar_prefetch=1,
        grid=(ns,),
        in_specs=[
            pl.BlockSpec(memory_space=pl.ANY),
            pl.BlockSpec((1, tq, hc), lambda b, pt: (b, 0, 0)),
            pl.BlockSpec((1, KV_COLS, PAGE_SIZE), lambda b, pt: (b, 0, 0)),
            pl.BlockSpec((1, KV_GROUPS, n_rng, tq, LANES), lambda b, pt: (b, 0, 0, 0, 0)),
            pl.BlockSpec(onehot.shape, lambda b, pt: (0, 0)),
        ],
        out_specs=pl.BlockSpec((1, tq, hc), lambda b, pt: (b, 0, 0)),
        scratch_shapes=[pltpu.VMEM((2, KV_COLS, tk), F32), pltpu.SemaphoreType.DMA((2,))],
    )
    return pl.pallas_call(
        functools.partial(_paged_slc_kernel, n_pages=n_pages, hg=hg, tq=tq, q_off=q_off),
        out_shape=jax.ShapeDtypeStruct((ns, tq, hc), F32),
        grid_spec=grid_spec,
        compiler_params=_cparams(("arbitrary",)),
        name="paged_slc",
    )(page_table.reshape(-1), cache_t, q, new_t, selb, onehot)


def _out0_kernel(*refs, bounds):
    np_ = len(bounds)
    x, pool, oc, osl, ow = (_row_part(refs[k * np_:(k + 1) * np_], bounds) for k in range(5))
    gt_ref, eg_ref, wa_ref, wb_ref, o_ref = refs[5 * np_:]
    ghi, glo = _split_bf16(gt_ref[...])
    o = None
    for b, br in enumerate((oc, osl, ow)):
        gate = _dot(ghi, eg_ref[b]) + _dot(glo, eg_ref[b])
        term = gate * br.astype(F32)
        o = term if o is None else o + term
    mix = _dot(pool, wa_ref[...]) + _dot(o.astype(BF16), wb_ref[...])
    o_ref[...] = x + mix


def _out0(x_parts, pool_parts, cmp_parts, slc_parts, win_parts, gates, eg, wa, wb, tm=512):
    m = sum(a.shape[0] for a in x_parts)
    d = x_parts[0].shape[1]
    row = lambda w: pl.BlockSpec((tm, w), lambda i: (i, 0))
    specs, args = [], []
    for parts in (x_parts, pool_parts, cmp_parts, slc_parts, win_parts):
        sp, bounds = _row_part_specs(parts, tm)
        specs += sp
        args += list(parts)
    return pl.pallas_call(
        functools.partial(_out0_kernel, bounds=bounds),
        out_shape=jax.ShapeDtypeStruct((m, d), F32),
        grid=(m // tm,),
        in_specs=specs + [row(LANES),
                          pl.BlockSpec(eg.shape, lambda i: (0, 0, 0)),
                          pl.BlockSpec(wa.shape, lambda i: (0, 0)),
                          pl.BlockSpec(wb.shape, lambda i: (0, 0))],
        out_specs=row(d),
        compiler_params=_cparams(("parallel",)),
        name="out0",
    )(*args, gates, eg, wa, wb)


def _ffn_kernel(x_ref, g_ref, wg_ref, wu_ref, wd_ref, o_ref, h_sc, acc_sc):
    f = pl.program_id(1)

    @pl.when(f == 0)
    def _():
        x = x_ref[...]
        ms = jnp.mean(x * x, axis=-1, keepdims=True)
        h_sc[...] = (x * lax.rsqrt(ms + RMS_EPS) * g_ref[...]).astype(BF16)
        acc_sc[...] = x

    h = h_sc[...]
    a = _dot(h, wg_ref[...])
    u = _dot(h, wu_ref[...])
    act = (a * jax.nn.sigmoid(a) * u).astype(BF16)
    acc_sc[...] += _dot(act, wd_ref[...])

    @pl.when(f == pl.num_programs(1) - 1)
    def _():
        o_ref[...] = acc_sc[...]


def _ffn(x, g, wg, wu, wd, tm=512, nf=2):
    m, d = x.shape
    fdim = wg.shape[1]
    tf = fdim // nf
    return pl.pallas_call(
        _ffn_kernel,
        out_shape=jax.ShapeDtypeStruct((m, d), F32),
        grid=(m // tm, nf),
        in_specs=[
            pl.BlockSpec((tm, d), lambda i, f: (i, 0)),
            pl.BlockSpec((1, d), lambda i, f: (0, 0)),
            pl.BlockSpec((d, tf), lambda i, f: (0, f)),
            pl.BlockSpec((d, tf), lambda i, f: (0, f)),
            pl.BlockSpec((tf, d), lambda i, f: (f, 0)),
        ],
        out_specs=pl.BlockSpec((tm, d), lambda i, f: (i, 0)),
        scratch_shapes=[pltpu.VMEM((tm, d), BF16), pltpu.VMEM((tm, d), F32)],
        compiler_params=_cparams(("parallel", "arbitrary")),
        name="ffn",
    )(x, g, wg, wu, wd)


def _out1_kernel(*refs, bounds):
    np_ = len(bounds)
    x_ref = refs[0]
    attn = _row_part(refs[1:1 + np_], bounds)
    w_ref, g_ref, rwh_ref, rwl_ref, rb_ref, xo_ref, h_ref, r_ref = refs[1 + np_:]
    x = x_ref[...] + _dot(attn, w_ref[...])
    xo_ref[...] = x
    ms = jnp.mean(x * x, axis=-1, keepdims=True)
    h = x * lax.rsqrt(ms + RMS_EPS) * g_ref[...]
    h_ref[...] = h
    hi, lo = _split_bf16(h)
    logits = _dot(hi, rwh_ref[...]) + _dot(lo, rwh_ref[...]) + _dot(hi, rwl_ref[...]) + rb_ref[...]
    lane = lax.broadcasted_iota(jnp.int32, logits.shape, 1)
    logits = jnp.where(lane < N_EXPERTS, logits, -jnp.inf)
    m1 = jnp.max(logits, axis=-1, keepdims=True)
    i1 = jnp.min(jnp.where(logits == m1, lane, LANES), axis=-1, keepdims=True)
    rest = jnp.where(lane == i1, -jnp.inf, logits)
    m2 = jnp.max(rest, axis=-1, keepdims=True)
    i2 = jnp.min(jnp.where(rest == m2, lane, LANES), axis=-1, keepdims=True)
    e2 = jnp.exp(m2 - m1)
    g1 = 1.0 / (1.0 + e2)
    g2 = e2 / (1.0 + e2)
    r = jnp.where(lane == 0, i1.astype(F32), 0.0)
    r = jnp.where(lane == 1, i2.astype(F32), r)
    r = jnp.where(lane == 2, g1, r)
    r = jnp.where(lane == 3, g2, r)
    r_ref[...] = r


def _out1(x, o_parts, w, g, rwh, rwl, rb, tm=512):
    m, d = x.shape
    row = lambda wd: pl.BlockSpec((tm, wd), lambda i: (i, 0))
    full = lambda a: pl.BlockSpec(a.shape, lambda i: (0, 0))
    o_specs, bounds = _row_part_specs(o_parts, tm)
    return pl.pallas_call(
        functools.partial(_out1_kernel, bounds=bounds),
        out_shape=[jax.ShapeDtypeStruct((m, d), F32), jax.ShapeDtypeStruct((m, d), F32),
                   jax.ShapeDtypeStruct((m, LANES), F32)],
        grid=(m // tm,),
        in_specs=[row(d)] + o_specs + [full(w), full(g), full(rwh), full(rwl), full(rb)],
        out_specs=[row(d), row(d), row(LANES)],
        compiler_params=_cparams(("parallel",)),
        name="out1",
    )(x, *o_parts, w, g, rwh, rwl, rb)


def _moe_kernel(be_ref, nb_ref, x_ref, wg_ref, wu_ref, wd_ref, o_ref, acc_sc):
    b = pl.program_id(0)
    f = pl.program_id(1)
    used = b < nb_ref[0]

    @pl.when(f == 0)
    def _():
        acc_sc[...] = jnp.zeros_like(acc_sc)

    @pl.when(used)
    def _():
        x = x_ref[...]
        a = _dot(x, wg_ref[0])
        u = _dot(x, wu_ref[0])
        act = (a * jax.nn.sigmoid(a) * u).astype(BF16)
        acc_sc[...] += _dot(act, wd_ref[0])

    @pl.when(f == pl.num_programs(1) - 1)
    def _():
        o_ref[...] = acc_sc[...]


def _moe(xs, blk_e, n_used, wg, wu, wd, tm, tf):
    cap, d = xs.shape
    fdim = wg.shape[2]
    grid_spec = pltpu.PrefetchScalarGridSpec(
        num_scalar_prefetch=2,
        grid=(cap // tm, fdim // tf),
        in_specs=[
            pl.BlockSpec((tm, d), lambda b, f, be, nb: (b, 0)),
            pl.BlockSpec((1, d, tf), lambda b, f, be, nb: (be[b], 0, f)),
            pl.BlockSpec((1, d, tf), lambda b, f, be, nb: (be[b], 0, f)),
            pl.BlockSpec((1, tf, d), lambda b, f, be, nb: (be[b], f, 0)),
        ],
        out_specs=pl.BlockSpec((tm, d), lambda b, f, be, nb: (b, 0)),
        scratch_shapes=[pltpu.VMEM((tm, d), F32)],
    )
    return pl.pallas_call(
        _moe_kernel,
        out_shape=jax.ShapeDtypeStruct((cap, d), F32),
        grid_spec=grid_spec,
        compiler_params=_cparams(("parallel", "arbitrary")),
        name="moe",
    )(blk_e, n_used, xs, wg, wu, wd)


def _start_rows(idx_ref, src_ref, dst, sem, base, n_rows):
    def start(r, c):
        pltpu.make_async_copy(src_ref.at[pl.ds(idx_ref[base + r], 1)], dst.at[pl.ds(r, 1)], sem).start()
        return c

    lax.fori_loop(0, n_rows, start, 0, unroll=8)


def _wait_rows(src_ref, dst, sem, n_rows):
    def wait(r, c):
        pltpu.make_async_copy(src_ref.at[pl.ds(0, 1)], dst.at[pl.ds(r, 1)], sem).wait()
        return c

    lax.fori_loop(0, n_rows, wait, 0, unroll=8)


def _dispatch_kernel(idx_ref, h_ref, o_ref, rows_sc, sem, *, tm):
    _start_rows(idx_ref, h_ref, rows_sc, sem.at[0], pl.program_id(0) * tm, tm)
    _wait_rows(h_ref, rows_sc, sem.at[0], tm)
    o_ref[...] = rows_sc[...].astype(o_ref.dtype)


def _dispatch(row_tok, h, tm):
    cap = row_tok.shape[0]
    d = h.shape[1]
    grid_spec = pltpu.PrefetchScalarGridSpec(
        num_scalar_prefetch=1,
        grid=(cap // tm,),
        in_specs=[pl.BlockSpec(memory_space=pl.ANY)],
        out_specs=pl.BlockSpec((tm, d), lambda b, idx: (b, 0)),
        scratch_shapes=[pltpu.VMEM((tm, d), h.dtype), pltpu.SemaphoreType.DMA((1,))],
    )
    return pl.pallas_call(
        functools.partial(_dispatch_kernel, tm=tm),
        out_shape=jax.ShapeDtypeStruct((cap, d), BF16),
        grid_spec=grid_spec,
        compiler_params=_cparams(("arbitrary",)),
        name="moe_dispatch",
    )(row_tok, h)


def _combine_kernel(dest_ref, x_ref, r_ref, ys_ref, o_ref, rows_sc, sem, *, tm, n_tok):
    t0 = pl.program_id(0) * tm
    for k in range(2):
        _start_rows(dest_ref, ys_ref, rows_sc.at[k], sem.at[0], k * n_tok + t0, tm)
    for k in range(2):
        _wait_rows(ys_ref, rows_sc.at[k], sem.at[0], tm)
    r = r_ref[...]
    o_ref[...] = x_ref[...] + (r[:, 2:3] * rows_sc[0] + r[:, 3:4] * rows_sc[1])


def _combine(dest, x, r, ys, tm=512):
    m, d = x.shape
    grid_spec = pltpu.PrefetchScalarGridSpec(
        num_scalar_prefetch=1,
        grid=(m // tm,),
        in_specs=[
            pl.BlockSpec((tm, d), lambda b, idx: (b, 0)),
            pl.BlockSpec((tm, LANES), lambda b, idx: (b, 0)),
            pl.BlockSpec(memory_space=pl.ANY),
        ],
        out_specs=pl.BlockSpec((tm, d), lambda b, idx: (b, 0)),
        scratch_shapes=[pltpu.VMEM((2, tm, d), F32), pltpu.SemaphoreType.DMA((1,))],
    )
    return pl.pallas_call(
        functools.partial(_combine_kernel, tm=tm, n_tok=m),
        out_shape=jax.ShapeDtypeStruct((m, d), F32),
        grid_spec=grid_spec,
        compiler_params=_cparams(("arbitrary",)),
        name="moe_combine",
    )(dest, x, r, ys)


def _pad_heads_cols(w, heads):
    d = w.shape[0]
    w = w.reshape(d, heads, HEAD_DIM)
    return jnp.pad(w, ((0, 0), (0, 0), (0, LANES - HEAD_DIM))).reshape(d, heads * LANES)


def _pad_heads_rows(w, heads):
    d = w.shape[1]
    w = w.reshape(heads, HEAD_DIM, d)
    return jnp.pad(w, ((0, 0), (LANES - HEAD_DIM, 0), (0, 0))).reshape(heads * LANES, d)


def _head_gain(gain, heads, scale):
    g = jnp.pad(gain.astype(F32) * scale, (0, LANES - HEAD_DIM))
    return jnp.tile(g, heads)


def _key_gain(gain):
    return jnp.concatenate([gain.astype(F32), jnp.ones((HEAD_DIM,), F32)])


def _cover_t(n_cmp_pad, n_cmp, n_blocks, nb_pad, row_off):
    m = jnp.arange(n_cmp_pad)
    c_start = m * CMP_STRIDE
    c_end = c_start + CMP_BLOCK - 1
    jb = jnp.arange(nb_pad) - row_off
    b_start = jb * SLC_BLOCK
    b_end = b_start + SLC_BLOCK - 1
    cov = (c_start[None, :] <= b_end[:, None]) & (c_end[None, :] >= b_start[:, None])
    cov = cov & (m[None, :] < n_cmp) & (jb[:, None] >= 0) & (jb[:, None] < n_blocks)
    return cov.astype(BF16)


def _route(r, n_tok, tm):
    top_e = r[:, 0:2].astype(jnp.int32)
    n_asg = n_tok * 2
    flat_e = top_e.reshape(n_asg)
    onehot = (flat_e[:, None] == jnp.arange(N_EXPERTS)[None, :]).astype(jnp.int32)
    csum = jnp.cumsum(onehot, axis=0)
    counts = csum[-1]
    padded = (counts + tm - 1) // tm * tm
    start = jnp.cumsum(counts) - counts
    pend = jnp.cumsum(padded)
    pstart = pend - padded
    dest = jnp.sum(onehot * (pstart[None, :] + csum - onehot), axis=1)
    n_blk = n_asg // tm + N_EXPERTS
    cap = n_blk * tm
    blk_e = jnp.minimum(jnp.sum(jnp.arange(n_blk)[:, None] * tm >= pend[None, :], axis=1), N_EXPERTS - 1)
    order = jnp.argsort(flat_e, stable=True)
    k = jnp.arange(cap, dtype=jnp.int32) - jnp.repeat(pstart[blk_e], tm)
    src = jnp.clip(jnp.repeat(start[blk_e], tm) + k, 0, n_asg - 1)
    row_tok = jnp.where(k < jnp.repeat(counts[blk_e], tm), order[src] // 2, 0).astype(jnp.int32)
    n_used = (pend[-1] // tm).astype(jnp.int32).reshape(1)
    dest_t = dest.astype(jnp.int32).reshape(n_tok, 2).T.reshape(n_asg)
    return dest_t, row_tok, blk_e.astype(jnp.int32), n_used


def kernel(x_prompt, x_sample, state_pool, cache_nsa_cmp, cache_nsa_slc, cache_nsa_win, cache_swa, page_table, norm0_mix, w_in0, pool_w, pool_scale, nsa_q_gain, nsa_k_gain, cmp_k_w1, cmp_k_w2, cmp_k_pe, cmp_v_w1, cmp_v_w2, cmp_v_pe, w_out0, norm0_ffn, ffn_w_gate, ffn_w_up, ffn_w_down, norm1_mix, w_in1, swa_q_gain, swa_k_gain, swa_sinks, w_out1, norm1_ffn, router_w, router_b, moe_w_gate, moe_w_up, moe_w_down):
    nb, t, d = x_prompt.shape
    ns, s_len, _ = x_sample.shape
    mp = nb * t
    msz = ns * s_len
    past = page_table.shape[1] * PAGE_SIZE
    total = past + s_len
    g = KV_GROUPS

    x_parts = (x_prompt.reshape(mp, d), x_sample.reshape(msz, d))

    c_q = POOL_CH
    c_kv = c_q + NSA_HEADS * HEAD_DIM
    c_gate = c_kv + 3 * KV_COLS
    n_gate = 3 * NSA_HEADS
    w0 = jnp.concatenate([
        w_in0[:, :c_q],
        _pad_heads_cols(w_in0[:, c_q:c_kv], NSA_HEADS),
        w_in0[:, c_kv:c_gate],
        jnp.pad(w_in0[:, c_gate:], ((0, 0), (0, LANES - n_gate))),
    ], axis=1).astype(BF16)
    qw = NSA_HEADS * LANES
    aux0 = jnp.concatenate([
        jnp.ones((POOL_CH,), F32),
        _head_gain(nsa_q_gain, NSA_HEADS, ATTN_SCALE),
        jnp.ones((KV_COLS,), F32),
        jnp.tile(_key_gain(nsa_k_gain[1]), g),
        jnp.tile(_key_gain(nsa_k_gain[2]), g),
        jnp.ones((LANES,), F32),
    ]).reshape(1, -1)
    o0 = POOL_CH
    o1 = o0 + qw
    o2 = o1 + KV_COLS
    o3 = o2 + KV_COLS
    o4 = o3 + KV_COLS
    segs0 = (("raw", 0, POOL_CH), ("qnorm", o0, qw), ("raw", o1, KV_COLS), ("kvnorm", o2, KV_COLS),
             ("kvnorm", o3, KV_COLS), ("sigmoid", o4, LANES))
    outs0 = ((POOL_CH, F32), (qw, F32), (KV_COLS, F32), (KV_COLS, F32), (KV_COLS, BF16),
             (KV_COLS, F32), (KV_COLS, BF16), (LANES, F32))
    u_all, q_all, cmp_all, slc_all, slc_bf, win_all, win_bf, gates_all = _proj(
        x_parts, norm0_mix.reshape(1, d), w0, aux0, segs0, outs0, 512, "proj0")

    pool_w_bf = pool_w.astype(BF16)
    pool_scale2 = pool_scale.reshape(1, POOL_CH).astype(F32)

    u_p = u_all[:mp].reshape(nb, t, POOL_CH)
    u_s = u_all[mp:].reshape(ns, s_len, POOL_CH)
    pool_o_p = _pool_prompt(u_p, pool_w_bf, pool_scale2)
    u_ext = jnp.concatenate([state_pool.astype(F32), u_s], axis=1)
    x_ext = jnp.pad(u_ext, ((0, 0), (1, 0), (0, 0)))
    pool_o_s = _pool_sample(x_ext, pool_w_bf, pool_scale2, s_len)
    pool_parts = (pool_o_p.reshape(mp, POOL_CH), pool_o_s.reshape(msz, POOL_CH).astype(BF16))

    def w1_parts(w1):
        return w1.reshape(2, CMP_STRIDE, HEAD_DIM, CMP_HIDDEN)

    w_sel = jnp.stack([w1_parts(cmp_k_w1), w1_parts(cmp_v_w1)] * g, axis=0)
    w1_bd = jnp.einsum("cpjdh,ce->jcdpeh", w_sel, jnp.eye(2 * g, dtype=F32))
    w1_bd = w1_bd.reshape(CMP_STRIDE, KV_COLS, 2 * 2 * g * CMP_HIDDEN).astype(BF16)
    w2_sel = jnp.stack([cmp_k_w2, cmp_v_w2] * g, axis=0)
    w2_bd = jnp.einsum("chd,ce->ched", w2_sel, jnp.eye(2 * g, dtype=F32))
    w2_bd = w2_bd.reshape(2 * g * CMP_HIDDEN, KV_COLS).astype(BF16)
    pek = jnp.pad(cmp_k_pe.reshape(1, -1), ((0, 7), (0, 0))).astype(BF16)
    pev = jnp.pad(cmp_v_pe.reshape(1, -1), ((0, 7), (0, 0))).astype(BF16)
    w1k = cmp_k_w1.astype(BF16)
    w1v = cmp_v_w1.astype(BF16)
    kg0 = _key_gain(nsa_k_gain[0]).reshape(1, LANES)
    seg_cols = CMP_STRIDE * KV_COLS
    half = 2 * g * CMP_HIDDEN

    n_seg_p = t // CMP_STRIDE
    pa_p, pb_p = _compress1(cmp_all[:mp].reshape(nb * n_seg_p, seg_cols), w1_bd, 256)
    kcv_p = _compress2(pa_p.reshape(nb, n_seg_p, half), pb_p.reshape(nb, n_seg_p, half),
                       jnp.zeros((nb, 8, half), F32), pek, pev, w1k, w1v, w2_bd, kg0)
    n_seg_s = past // CMP_STRIDE
    pad_new = -(-total // CMP_STRIDE) * CMP_STRIDE - past
    assert pad_new == CMP_STRIDE, "the new rows must fit one segment"
    new_seg = jnp.pad(cmp_all[mp:].reshape(ns, s_len * KV_COLS), ((0, 0), (0, (pad_new - s_len) * KV_COLS)))
    _, pb_new = _compress1(new_seg, w1_bd, ns)
    pb_new = jnp.pad(pb_new.reshape(ns, 1, half), ((0, 0), (0, 7), (0, 0)))
    w_kv = jnp.stack([w1_parts(cmp_k_w1), w1_parts(cmp_v_w1)], axis=0)
    w1_gs = jnp.einsum("kpjdh,ke->jkdpeh", w_kv, jnp.eye(2, dtype=F32))
    w1_gs = w1_gs.reshape(CMP_STRIDE * 2 * HEAD_DIM, 2 * 2 * CMP_HIDDEN).astype(BF16)
    kcv_s = _paged_compress(page_table, _pages_t(cache_nsa_cmp), pb_new, w1_gs, pek, pev, w1k, w1v, w2_bd, kg0)

    hg0 = NSA_HEADS // g
    q_p = q_all[:mp].reshape(nb, t, qw)
    q_s = q_all[mp:].reshape(ns, s_len, qw)
    nblk_p = t // SLC_BLOCK
    cov_p = _cover_t(n_seg_p, n_seg_p - 1, nblk_p, LANES, HEAD_DIM)
    o_cmp_p, selb_p = _cmp_attn(q_p, kcv_p, cov_p, hg=hg0, tq=256, bn=1, q_off=0, n_blocks=nblk_p,
                                out_dtype=BF16, row_off=HEAD_DIM, as_bias=True)
    nblk_s = -(-total // SLC_BLOCK)
    nbp_s = -(-nblk_s // LANES) * LANES
    n_cmp_s = -(-total // CMP_STRIDE) - 1
    cov_s = _cover_t(n_seg_s, n_cmp_s, nblk_s, nbp_s, 0)
    o_cmp_s, sel_s = _cmp_attn(q_s, kcv_s, cov_s, hg=hg0, tq=s_len, bn=LANES // s_len, q_off=past,
                               n_blocks=nblk_s, out_dtype=F32, row_off=0, as_bias=False)

    o_slc_p = _flash(q_all, slc_bf, n=nb, t=t, hg=hg0, tq=512, window=None, selb=selb_p)
    new_slc_t = jnp.pad(jnp.transpose(slc_all[mp:].reshape(ns, s_len, KV_COLS), (0, 2, 1)),
                        ((0, 0), (0, 0), (0, PAGE_SIZE - s_len)))
    o_slc_s = _paged_slc(page_table, _pages_t(cache_nsa_slc), q_s, new_slc_t, sel_s, hg=hg0, q_off=past)

    o_win_p = _flash(q_all, win_bf, n=nb, t=t, hg=hg0, tq=256, window=NSA_WINDOW)
    wb_len = cache_nsa_win.shape[1]
    tk_w = -(-(wb_len + s_len) // 16) * 16
    kv_win_s = jnp.concatenate([
        cache_nsa_win.astype(BF16).reshape(ns, wb_len, KV_COLS), win_bf[mp:].reshape(ns, s_len, KV_COLS),
        jnp.zeros((ns, tk_w - wb_len - s_len, KV_COLS), BF16)], axis=1)
    o_win_s = _attn(q_s, kv_win_s, hg=hg0, window=NSA_WINDOW, q_off=past, k_off=past - wb_len)

    parts = lambda a, b: (a.reshape(mp, -1), b.reshape(msz, -1).astype(BF16))
    eg = (jnp.arange(LANES)[None, :, None] ==
          (jnp.arange(3)[:, None, None] * NSA_HEADS + jnp.arange(qw)[None, None, :] // LANES)).astype(BF16)
    wa0 = w_out0[:POOL_CH].astype(BF16)
    wb0 = _pad_heads_rows(w_out0[POOL_CH:], NSA_HEADS).astype(BF16)
    x1 = _out0(x_parts, pool_parts, parts(o_cmp_p, o_cmp_s), parts(o_slc_p, o_slc_s), parts(o_win_p, o_win_s),
               gates_all, eg, wa0, wb0)
    x2 = _ffn(x1, norm0_ffn.reshape(1, d), ffn_w_gate.astype(BF16), ffn_w_up.astype(BF16),
              ffn_w_down.astype(BF16))

    c_q1 = SWA_HEADS * HEAD_DIM
    qw1 = SWA_HEADS * LANES
    w1p = jnp.concatenate([_pad_heads_cols(w_in1[:, :c_q1], SWA_HEADS), w_in1[:, c_q1:]], axis=1).astype(BF16)
    aux1 = jnp.concatenate([_head_gain(swa_q_gain, SWA_HEADS, ATTN_SCALE),
                            jnp.tile(_key_gain(swa_k_gain), g)]).reshape(1, -1)
    segs1 = (("qnorm", 0, qw1), ("kvnorm", qw1, KV_COLS))
    outs1 = ((qw1, F32), (KV_COLS, F32), (KV_COLS, BF16))
    q1_all, swa_all, swa_bf = _proj((x2,), norm1_mix.reshape(1, d), w1p, aux1, segs1, outs1, 512, "proj1")
    hg1 = SWA_HEADS // g
    sinks = swa_sinks.astype(F32)
    q1_s = q1_all[mp:].reshape(ns, s_len, qw1)
    o1_p = _flash(q1_all, swa_bf, n=nb, t=t, hg=hg1, tq=128, window=SWA_WINDOW, sinks=sinks)
    sb_len = cache_swa.shape[1]
    tk_1 = -(-(sb_len + s_len) // 16) * 16
    kv_swa_s = jnp.concatenate([
        cache_swa.astype(BF16).reshape(ns, sb_len, KV_COLS), swa_bf[mp:].reshape(ns, s_len, KV_COLS),
        jnp.zeros((ns, tk_1 - sb_len - s_len, KV_COLS), BF16)], axis=1)
    o1_s = _attn(q1_s, kv_swa_s, hg=hg1, window=SWA_WINDOW, q_off=past, k_off=past - sb_len, sinks=sinks)

    rw = jnp.pad(router_w.astype(F32), ((0, 0), (0, LANES - N_EXPERTS)))
    rwh, rwl = _split_bf16(rw)
    rb = jnp.pad(router_b.astype(F32), (0, LANES - N_EXPERTS)).reshape(1, LANES)
    x3, h3, r = _out1(x2, parts(o1_p, o1_s), _pad_heads_rows(w_out1, SWA_HEADS).astype(BF16),
                      norm1_ffn.reshape(1, d), rwh, rwl, rb)
    m_all = mp + msz
    tm_moe = 512
    dest_t, row_tok, blk_e, n_used = _route(r, m_all, tm_moe)
    n_blk = row_tok.shape[0] // tm_moe
    xs = _dispatch(row_tok, h3, tm_moe * max(k for k in (4, 2, 1) if n_blk % k == 0))
    ys = _moe(xs, blk_e, n_used, moe_w_gate.astype(BF16), moe_w_up.astype(BF16), moe_w_down.astype(BF16),
              tm_moe, moe_w_gate.shape[2] // 2)
    x4 = _combine(dest_t, x3, r, ys)

    y_p = x4[:mp].reshape(nb, t, d)
    y_s = x4[mp:].reshape(ns, s_len, d)
    rows5 = lambda a, n, l: a.reshape(n, l, g, 2, HEAD_DIM)
    keep_w = min(NSA_WINDOW, t)
    keep_s = min(SWA_WINDOW, t)
    pool_p = u_p[:, t - POOL_STATE:]
    cmp_p = rows5(cmp_all[:mp], nb, t)
    slc_p_out = rows5(slc_all[:mp], nb, t)
    win_p_out = rows5(win_all[:mp], nb, t)[:, t - keep_w:]
    swa_p_out = rows5(swa_all[:mp], nb, t)[:, t - keep_s:]
    pool_s = u_ext[:, -POOL_STATE:]
    cmp_s = rows5(cmp_all[mp:], ns, s_len)
    slc_s_out = rows5(slc_all[mp:], ns, s_len)
    win_s_out = jnp.concatenate([cache_nsa_win.astype(F32), rows5(win_all[mp:], ns, s_len)], axis=1)[:, -wb_len:]
    swa_s_out = jnp.concatenate([cache_swa.astype(F32), rows5(swa_all[mp:], ns, s_len)], axis=1)[:, -sb_len:]
    return (y_p, y_s, pool_p, cmp_p, slc_p_out, win_p_out, swa_p_out, pool_s, cmp_s, slc_s_out, win_s_out,
            swa_s_out)
```

```python
import functools

import jax
import jax.numpy as jnp
from jax import lax
from jax.experimental import pallas as pl
from jax.experimental.pallas import tpu as pltpu

F32 = jnp.float32
BF16 = jnp.bfloat16

LANES = 128
MXU_COLS = 256
HEAD_DIM = 64
D_MODEL = 1024
POOL_WINDOWS = (2, 4, 8, 16)
POOL_CH = 512
POOL_GROUP = 128
POOL_STATE = 15
POOL_HALO = 16
NSA_HEADS = 8
SWA_HEADS = 16
KV_GROUPS = 2
KV_COLS = KV_GROUPS * 2 * HEAD_DIM
CMP_BLOCK = 32
CMP_STRIDE = 16
CMP_HIDDEN = 128
SLC_BLOCK = 64
SLC_SHIFT = SLC_BLOCK.bit_length() - 1
SLC_TOPN = 16
NSA_WINDOW = 512
SWA_WINDOW = 128
PAGE_SIZE = 128
N_EXPERTS = 8
RMS_EPS = 1e-6
NEG_INF = -1e30
FORCE_SCORE = 1e9
PAD_SCORE = -3e38
ATTN_SCALE = HEAD_DIM ** -0.5
VMEM_LIMIT = 56 * 1024 * 1024


def _cparams(sem):
    return pltpu.CompilerParams(dimension_semantics=sem, vmem_limit_bytes=VMEM_LIMIT)


def _split_bf16(x):
    hi = x.astype(BF16)
    lo = (x - hi.astype(F32)).astype(BF16)
    return hi, lo


def _dot(a, b):
    return jnp.dot(a, b, preferred_element_type=F32)


def _dot_nt(a, b):
    return lax.dot_general(a, b, (((1,), (1,)), ((), ())), preferred_element_type=F32)


def _row_part_specs(parts, tm):
    specs, bounds, t0 = [], [], 0
    for a in parts:
        nt = a.shape[0] // tm
        assert nt * tm == a.shape[0]
        specs.append(pl.BlockSpec((tm, a.shape[1]), lambda i, t0=t0, nt=nt: (jnp.clip(i - t0, 0, nt - 1), 0)))
        t0 += nt
        bounds.append(t0)
    return specs, tuple(bounds)


def _row_part(refs, bounds):
    i = pl.program_id(0)
    v = refs[-1][...]
    for k in range(len(refs) - 2, -1, -1):
        v = jnp.where(i < bounds[k], refs[k][...], v)
    return v


def _proj_kernel(*refs, segs, bounds):
    np_ = len(bounds)
    x = _row_part(refs[:np_], bounds)
    g_ref, w_ref, aux_ref = refs[np_:np_ + 3]
    out_refs = refs[np_ + 3:]
    ms = jnp.mean(x * x, axis=-1, keepdims=True)
    h = (x * lax.rsqrt(ms + RMS_EPS) * g_ref[...]).astype(BF16)
    lane = lax.broadcasted_iota(jnp.int32, (1, LANES), 1)
    is_key = lane < HEAD_DIM
    oi = 0
    for kind, c0, width in segs:
        if kind == "qnorm":
            for pair in range(width // MXU_COLS):
                cp = c0 + pair * MXU_COLS
                z = _dot(h, w_ref[:, cp:cp + MXU_COLS])
                for hh in range(MXU_COLS // LANES):
                    zh = z[:, hh * LANES:(hh + 1) * LANES]
                    msq = jnp.sum(zh * zh, axis=-1, keepdims=True) * (1.0 / HEAD_DIM)
                    lo = cp - c0 + hh * LANES
                    gain = aux_ref[:, cp + hh * LANES:cp + (hh + 1) * LANES]
                    out_refs[oi][:, lo:lo + LANES] = zh * lax.rsqrt(msq + RMS_EPS) * gain
            oi += 1
            continue
        z = _dot(h, w_ref[:, c0:c0 + width])
        aux = aux_ref[:, c0:c0 + width]
        if kind == "raw":
            out_refs[oi][...] = z
            oi += 1
        elif kind == "sigmoid":
            out_refs[oi][...] = jax.nn.sigmoid(z)
            oi += 1
        elif kind == "kvnorm":
            for gg in range(width // LANES):
                sl = slice(gg * LANES, (gg + 1) * LANES)
                zb = z[:, sl]
                msq = jnp.sum(jnp.where(is_key, zb * zb, 0.0), axis=-1, keepdims=True) * (1.0 / HEAD_DIM)
                y = jnp.where(is_key, zb * lax.rsqrt(msq + RMS_EPS) * aux[:, sl], zb)
                out_refs[oi][:, sl] = y
                out_refs[oi + 1][:, sl] = y.astype(BF16)
            oi += 2
        else:
            raise ValueError(kind)


def _proj(x_parts, g, w, aux, segs, out_defs, tm, name):
    m = sum(a.shape[0] for a in x_parts)
    d = x_parts[0].shape[1]
    c = w.shape[1]
    out_shape = [jax.ShapeDtypeStruct((m, wd), dt) for wd, dt in out_defs]
    out_specs = [pl.BlockSpec((tm, wd), lambda i: (i, 0)) for wd, _ in out_defs]
    x_specs, bounds = _row_part_specs(x_parts, tm)
    return pl.pallas_call(
        functools.partial(_proj_kernel, segs=segs, bounds=bounds),
        out_shape=out_shape,
        grid=(m // tm,),
        in_specs=x_specs + [
            pl.BlockSpec((1, d), lambda i: (0, 0)),
            pl.BlockSpec((d, c), lambda i: (0, 0)),
            pl.BlockSpec((1, c), lambda i: (0, 0)),
        ],
        out_specs=out_specs,
        compiler_params=_cparams(("parallel",)),
        name=name,
    )(*x_parts, g, w, aux)


def _pool_prompt_kernel(u_ref, halo_ref, pw_ref, ps_ref, o_ref, e_ref, *, tp):
    i = pl.program_id(1)
    e_ref[0:POOL_HALO, :] = jnp.where(i > 0, halo_ref[0], 0.0)
    e_ref[POOL_HALO:, :] = u_ref[0]
    t = i * tp + lax.broadcasted_iota(jnp.int32, (tp, 1), 0)
    for gi, w in enumerate(POOL_WINDOWS):
        sl = slice(gi * POOL_GROUP, (gi + 1) * POOL_GROUP)
        tok = e_ref[POOL_HALO:POOL_HALO + tp, sl]
        s = tok
        for k in range(1, w):
            s = s + e_ref[POOL_HALO - k:POOL_HALO - k + tp, sl]
        cnt = jnp.minimum(t + 1, w).astype(F32)
        dg = s / cnt - tok
        y = _dot(dg.astype(BF16), pw_ref[gi]) * ps_ref[:, sl]
        o_ref[0, :, sl] = y.astype(o_ref.dtype)


def _pool_prompt(u, pool_w, pool_scale, tp=512):
    n, t, c = u.shape
    hb = tp // POOL_HALO
    return pl.pallas_call(
        functools.partial(_pool_prompt_kernel, tp=tp),
        out_shape=jax.ShapeDtypeStruct((n, t, c), BF16),
        grid=(n, t // tp),
        in_specs=[
            pl.BlockSpec((1, tp, c), lambda b, i: (b, i, 0)),
            pl.BlockSpec((1, POOL_HALO, c), lambda b, i: (b, jnp.maximum(i * hb - 1, 0), 0)),
            pl.BlockSpec((len(POOL_WINDOWS), POOL_GROUP, POOL_GROUP), lambda b, i: (0, 0, 0)),
            pl.BlockSpec((1, c), lambda b, i: (0, 0)),
        ],
        out_specs=pl.BlockSpec((1, tp, c), lambda b, i: (b, i, 0)),
        scratch_shapes=[pltpu.VMEM((tp + POOL_HALO, c), F32)],
        compiler_params=_cparams(("parallel", "parallel")),
        name="pool_prompt",
    )(u, u, pool_w, pool_scale)


def _pool_sample_kernel(x_ref, pw_ref, ps_ref, o_ref, *, bn, s_len):
    base = 1 + POOL_STATE
    for gi, w in enumerate(POOL_WINDOWS):
        sl = slice(gi * POOL_GROUP, (gi + 1) * POOL_GROUP)
        tok = x_ref[:, base:base + s_len, sl]
        s = tok
        for k in range(1, w):
            s = s + x_ref[:, base - k:base - k + s_len, sl]
        dg = (s / float(w) - tok).reshape(bn * s_len, POOL_GROUP)
        y = _dot(dg.astype(BF16), pw_ref[gi]) * ps_ref[:, sl]
        o_ref[:, :, sl] = y.reshape(bn, s_len, POOL_GROUP).astype(o_ref.dtype)


def _pool_sample(x_ext, pool_w, pool_scale, s_len, bn=16):
    n, l, c = x_ext.shape
    return pl.pallas_call(
        functools.partial(_pool_sample_kernel, bn=bn, s_len=s_len),
        out_shape=jax.ShapeDtypeStruct((n, s_len, c), F32),
        grid=(n // bn,),
        in_specs=[
            pl.BlockSpec((bn, l, c), lambda b: (b, 0, 0)),
            pl.BlockSpec((len(POOL_WINDOWS), POOL_GROUP, POOL_GROUP), lambda b: (0, 0, 0)),
            pl.BlockSpec((1, c), lambda b: (0, 0)),
        ],
        out_specs=pl.BlockSpec((bn, s_len, c), lambda b: (b, 0, 0)),
        compiler_params=_cparams(("parallel",)),
        name="pool_sample",
    )(x_ext, pool_w, pool_scale)


def _compress1_kernel(x_ref, w_ref, pa_ref, pb_ref):
    acc = None
    for j in range(CMP_STRIDE):
        xj = x_ref[:, j * KV_COLS:(j + 1) * KV_COLS].astype(BF16)
        d = _dot(xj, w_ref[j])
        acc = d if acc is None else acc + d
    half = 4 * CMP_HIDDEN
    pa_ref[...] = acc[:, :half]
    pb_ref[...] = acc[:, half:]


def _compress1(x, w_bd, ts):
    r, c = x.shape
    ts = min(ts, r)
    half = 4 * CMP_HIDDEN
    return pl.pallas_call(
        _compress1_kernel,
        out_shape=[jax.ShapeDtypeStruct((r, half), F32), jax.ShapeDtypeStruct((r, half), F32)],
        grid=(r // ts,),
        in_specs=[
            pl.BlockSpec((ts, c), lambda i: (i, 0)),
            pl.BlockSpec(w_bd.shape, lambda i: (0, 0, 0)),
        ],
        out_specs=[pl.BlockSpec((ts, half), lambda i: (i, 0)), pl.BlockSpec((ts, half), lambda i: (i, 0))],
        compiler_params=_cparams(("parallel",)),
        name="compress1",
    )(x, w_bd)


def _gelu_tanh(x):
    return 0.5 * x * (1.0 + jnp.tanh(0.7978845608028654 * (x + 0.044715 * (x * x * x))))


def _compress_tail(pa, pb, pb_new, pek_ref, pev_ref, w1k_ref, w1v_ref, w2_ref, kg_ref, o_ref):
    s = pa.shape[0]
    row = lax.broadcasted_iota(jnp.int32, (s, 1), 0)
    pb_next = jnp.where(row == s - 1, pb_new, pltpu.roll(pb, s - 1, 0))
    hk = _dot(pek_ref[...], w1k_ref[...])[0:1, :]
    hv = _dot(pev_ref[...], w1v_ref[...])[0:1, :]
    pe_h = jnp.concatenate([hk, hv, hk, hv], axis=1)
    a = _gelu_tanh(pa + pb_next + pe_h)
    o = _dot(a.astype(BF16), w2_ref[...])
    lane = lax.broadcasted_iota(jnp.int32, (1, LANES), 1)
    is_key = lane < HEAD_DIM
    for gg in range(KV_GROUPS):
        sl = slice(gg * LANES, (gg + 1) * LANES)
        ob = o[:, sl]
        msq = jnp.sum(jnp.where(is_key, ob * ob, 0.0), axis=-1, keepdims=True) * (1.0 / HEAD_DIM)
        y = jnp.where(is_key, ob * lax.rsqrt(msq + RMS_EPS) * kg_ref[...], ob)
        o_ref[0, :, sl] = y.astype(o_ref.dtype)


def _compress2_kernel(pa_ref, pb_ref, pbn_ref, pek_ref, pev_ref, w1k_ref, w1v_ref, w2_ref, kg_ref, o_ref):
    _compress_tail(pa_ref[0], pb_ref[0], pbn_ref[0][0:1, :], pek_ref, pev_ref, w1k_ref, w1v_ref, w2_ref,
                   kg_ref, o_ref)


def _pages_t(cache):
    return jnp.transpose(cache, (0, 2, 3, 4, 1)).reshape(cache.shape[0], KV_COLS, cache.shape[1])


def _paged_prefetch(pt_ref, cache_ref, dst, sem, n_pages):
    b = pl.program_id(0)
    slot = b % 2

    def fetch(bb, sl):
        def body(p, c):
            pltpu.make_async_copy(cache_ref.at[pt_ref[bb * n_pages + p]], dst(sl, p), sem.at[sl]).start()
            return c

        lax.fori_loop(0, n_pages, body, 0)

    @pl.when(b == 0)
    def _():
        fetch(0, 0)

    @pl.when(b + 1 < pl.num_programs(0))
    def _():
        fetch(b + 1, 1 - slot)

    def wait(p, c):
        pltpu.make_async_copy(cache_ref.at[0], dst(slot, p), sem.at[slot]).wait()
        return c

    lax.fori_loop(0, n_pages, wait, 0)
    return slot


def _paged_compress_kernel(pt_ref, cache_ref, pbn_ref, w1_ref, pek_ref, pev_ref, w1k_ref, w1v_ref, w2_ref, kg_ref,
                           o_ref, buf, rows_sc, sem, *, n_pages):
    slot = _paged_prefetch(
        pt_ref, cache_ref, lambda sl, p: buf.at[sl, pl.ds(pl.multiple_of(p * KV_COLS, KV_COLS), KV_COLS)],
        sem, n_pages)
    for p in range(n_pages):
        for gg in range(KV_GROUPS):
            f0 = p * KV_COLS + gg * LANES
            rows_sc[gg, p * PAGE_SIZE:(p + 1) * PAGE_SIZE, :] = buf[slot, f0:f0 + LANES, :].T
    n_seg = n_pages * (PAGE_SIZE // CMP_STRIDE)
    hid2 = 2 * CMP_HIDDEN
    pas, pbs = [], []
    for gg in range(KV_GROUPS):
        xg = jnp.concatenate(
            [rows_sc[gg, pl.ds(j, n_seg, stride=CMP_STRIDE), :].astype(BF16)
             for j in range(CMP_STRIDE)], axis=1)
        pg = _dot(xg, w1_ref[...])
        pas.append(pg[:, :hid2])
        pbs.append(pg[:, hid2:])
    _compress_tail(jnp.concatenate(pas, axis=1), jnp.concatenate(pbs, axis=1), pbn_ref[0][0:1, :],
                   pek_ref, pev_ref, w1k_ref, w1v_ref, w2_ref, kg_ref, o_ref)


def _paged_compress(page_table, cache_t, pb_new, w1_gs, pek, pev, w1k, w1v, w2_bd, kg):
    ns, n_pages = page_table.shape
    s = n_pages * (PAGE_SIZE // CMP_STRIDE)
    c = pb_new.shape[-1]
    full2 = lambda b, pt: (0, 0)
    grid_spec = pltpu.PrefetchScalarGridSpec(
        num_scalar_prefetch=1,
        grid=(ns,),
        in_specs=[
            pl.BlockSpec(memory_space=pl.ANY),
            pl.BlockSpec((1, 8, c), lambda b, pt: (b, 0, 0)),
            pl.BlockSpec(w1_gs.shape, full2),
            pl.BlockSpec(pek.shape, full2),
            pl.BlockSpec(pev.shape, full2),
            pl.BlockSpec(w1k.shape, full2),
            pl.BlockSpec(w1v.shape, full2),
            pl.BlockSpec(w2_bd.shape, full2),
            pl.BlockSpec(kg.shape, full2),
        ],
        out_specs=pl.BlockSpec((1, s, KV_COLS), lambda b, pt: (b, 0, 0)),
        scratch_shapes=[pltpu.VMEM((2, n_pages * KV_COLS, PAGE_SIZE), F32),
                        pltpu.VMEM((KV_GROUPS, n_pages * PAGE_SIZE, LANES), F32), pltpu.SemaphoreType.DMA((2,))],
    )
    return pl.pallas_call(
        functools.partial(_paged_compress_kernel, n_pages=n_pages),
        out_shape=jax.ShapeDtypeStruct((ns, s, KV_COLS), BF16),
        grid_spec=grid_spec,
        compiler_params=_cparams(("arbitrary",)),
        name="paged_compress",
    )(page_table.reshape(-1), cache_t, pb_new, w1_gs, pek, pev, w1k, w1v, w2_bd, kg)


def _compress2(pa, pb, pb_new, pek, pev, w1k, w1v, w2_bd, kg):
    n, s, c = pa.shape
    full2 = lambda b: (0, 0)
    return pl.pallas_call(
        _compress2_kernel,
        out_shape=jax.ShapeDtypeStruct((n, s, KV_COLS), BF16),
        grid=(n,),
        in_specs=[
            pl.BlockSpec((1, s, c), lambda b: (b, 0, 0)),
            pl.BlockSpec((1, s, c), lambda b: (b, 0, 0)),
            pl.BlockSpec((1, 8, c), lambda b: (b, 0, 0)),
            pl.BlockSpec(pek.shape, full2),
            pl.BlockSpec(pev.shape, full2),
            pl.BlockSpec(w1k.shape, full2),
            pl.BlockSpec(w1v.shape, full2),
            pl.BlockSpec(w2_bd.shape, full2),
            pl.BlockSpec(kg.shape, full2),
        ],
        out_specs=pl.BlockSpec((1, s, KV_COLS), lambda b: (b, 0, 0)),
        compiler_params=_cparams(("parallel",)),
        name="compress2",
    )(pa, pb, pb_new, pek, pev, w1k, w1v, w2_bd, kg)


def _stack_heads(q, hg):
    return jnp.concatenate([q[:, h * LANES:(h + 1) * LANES] for h in range(hg)], axis=0).astype(BF16)


def _row_qpos(hg, tq, q0):
    r = lax.broadcasted_iota(jnp.int32, (hg * tq, 1), 0)
    return q0 + (r & (tq - 1))


def _store_heads(o_ref, lead, o, hg, tq, head0=0):
    lane = lax.broadcasted_iota(jnp.int32, (1, LANES), 1)
    for h in range(hg):
        blk = jnp.where(lane >= HEAD_DIM, o[h * tq:(h + 1) * tq], 0.0)
        c0 = (head0 + h) * LANES
        o_ref[lead + (slice(None), slice(c0, c0 + LANES))] = blk.astype(o_ref.dtype)


def _cmp_attn_kernel(q_ref, kc_ref, cov_ref, o_ref, sel_ref, *, bn, hg, tq, q_off, n_blocks, row_off, as_bias):
    i = pl.program_id(2)
    s_len = kc_ref.shape[1]
    q0 = q_off + i * tq
    qpos = _row_qpos(hg, tq, q0)
    blk_end = lax.broadcasted_iota(jnp.int32, (1, s_len), 1) * CMP_STRIDE + (CMP_BLOCK - 1)
    cond = blk_end <= qpos
    psums = []
    for b in range(bn):
        qs = _stack_heads(q_ref[b], hg)
        kc = kc_ref[b]
        s = jnp.where(cond, _dot_nt(qs, kc), NEG_INF)
        m = jnp.max(s, axis=-1, keepdims=True)
        p = jnp.where(cond, jnp.exp(s - m), 0.0)
        den = jnp.sum(p, axis=-1, keepdims=True)
        p = p / jnp.maximum(den, 1e-30)
        o = _dot(p.astype(BF16), kc)
        _store_heads(o_ref, (b,), o, hg, tq)
        ps = p[0:tq]
        for h in range(1, hg):
            ps = ps + p[h * tq:(h + 1) * tq]
        psums.append(ps)
    ps = psums[0] if bn == 1 else jnp.concatenate(psums, axis=0)
    hi, lo = _split_bf16(ps)
    cov = cov_ref[...]
    score = _dot_nt(cov, hi) + _dot_nt(cov, lo)
    nb_pad, nl = score.shape
    j = lax.broadcasted_iota(jnp.int32, (nb_pad, nl), 0) - row_off
    t = q0 + (lax.broadcasted_iota(jnp.int32, (1, nl), 1) & (tq - 1))
    cur = t >> SLC_SHIFT
    forced = (j == 0) | (j == cur) | (j == cur - 1)
    valid = j * SLC_BLOCK <= t
    score = jnp.where(valid, jnp.where(forced, FORCE_SCORE, score), NEG_INF)
    is_block = lax.bitcast_convert_type(j, jnp.uint32) < jnp.uint32(n_blocks)
    score = jnp.where(is_block, score, PAD_SCORE)
    hi = min(nb_pad, row_off + -(-n_blocks // 8) * 8)
    sc = score[row_off:hi]
    jj = lax.broadcasted_iota(jnp.int32, sc.shape, 0)
    picked = jnp.zeros(sc.shape, F32)
    for _ in range(min(SLC_TOPN, n_blocks)):
        mx = jnp.max(sc, axis=0, keepdims=True)
        jm = jnp.min(jnp.where(sc == mx, jj, nb_pad), axis=0, keepdims=True)
        pick = jj == jm
        picked = jnp.where(pick, 1.0, picked)
        sc = jnp.where(pick, -jnp.inf, sc)
    sel = jnp.concatenate([jnp.zeros((n, nl), F32) for n in (row_off,) if n] + [picked] +
                          [jnp.zeros((n, nl), F32) for n in (nb_pad - hi,) if n], axis=0)
    if as_bias:
        sel = jnp.where(is_block, (sel - 1.0) * (-NEG_INF), 0.0)
    sel_t = sel.T
    for b in range(bn):
        sel_ref[b, 0] = sel_t[b * tq:(b + 1) * tq]


def _cmp_attn(q, kc, cov_t, *, hg, tq, bn, q_off, n_blocks, out_dtype, row_off, as_bias):
    n, t, hc = q.shape
    s_len = kc.shape[1]
    nb_pad = cov_t.shape[0]
    g = KV_GROUPS
    return pl.pallas_call(
        functools.partial(_cmp_attn_kernel, bn=bn, hg=hg, tq=tq, q_off=q_off, n_blocks=n_blocks,
                          row_off=row_off, as_bias=as_bias),
        out_shape=[jax.ShapeDtypeStruct((n, t, hc), out_dtype), jax.ShapeDtypeStruct((n, g, t, nb_pad), F32)],
        grid=(n // bn, g, t // tq),
        in_specs=[
            pl.BlockSpec((bn, tq, hg * LANES), lambda b, gg, i: (b, i, gg)),
            pl.BlockSpec((bn, s_len, LANES), lambda b, gg, i: (b, 0, gg)),
            pl.BlockSpec((nb_pad, s_len), lambda b, gg, i: (0, 0)),
        ],
        out_specs=[
            pl.BlockSpec((bn, tq, hg * LANES), lambda b, gg, i: (b, i, gg)),
            pl.BlockSpec((bn, 1, tq, nb_pad), lambda b, gg, i: (b, gg, i, 0)),
        ],
        compiler_params=_cparams(("parallel", "parallel", "parallel")),
        name="cmp_attn",
    )(q, kc, cov_t)


def _flash_kernel(*refs, hg, tq, tk, window, use_sel, use_sink):
    refs = list(refs)
    sink_ref = refs.pop(0) if use_sink else None
    q_ref = refs.pop(0)
    kv_ref = refs.pop(0)
    selb_ref = refs.pop(0) if use_sel else None
    o_ref, qs_sc, m_sc, acc_sc = refs
    g = pl.program_id(1)
    i = pl.program_id(2)
    rows = hg * tq
    q0 = i * tq
    for h in range(hg):
        qh = q_ref[:, h * LANES:(h + 1) * LANES]
        if use_sel:
            qh = qh + selb_ref[0, 0]
        qs_sc[h * tq:(h + 1) * tq, :] = qh.astype(BF16)
    m_sc[...] = jnp.full((rows, LANES), NEG_INF, F32)
    acc_sc[...] = jnp.zeros((rows, LANES), F32)

    def tile(k0, masked):
        kv = kv_ref[pl.ds(k0, tk), :]
        klane = lax.broadcasted_iota(jnp.int32, (tk, LANES), 1)
        if use_sel:
            krow = lax.broadcasted_iota(jnp.int32, (tk, LANES), 0)
            onehot = jnp.where(klane - HEAD_DIM == ((k0 + krow) >> SLC_SHIFT), 1.0, 0.0).astype(BF16)
            kaug = jnp.where(klane < HEAD_DIM, kv, onehot)
        else:
            kaug = kv
        vaug = jnp.where(klane < HEAD_DIM, jnp.ones_like(kv), kv)
        s = _dot_nt(qs_sc[...], kaug)
        if masked:
            qpos = q0 + (lax.broadcasted_iota(jnp.int32, (rows, 1), 0) & (tq - 1))
            d = qpos - (k0 + lax.broadcasted_iota(jnp.int32, (1, tk), 1))
            if window is None:
                s = jnp.where(d >= 0, s, NEG_INF)
            else:
                s = jnp.where(lax.bitcast_convert_type(d, jnp.uint32) <= jnp.uint32(window), s, NEG_INF)
        m_prev = m_sc[...]
        m_new = jnp.maximum(m_prev, jnp.max(s, axis=-1, keepdims=True))
        alpha = jnp.exp(m_prev - m_new)
        p = jnp.exp(s - jnp.concatenate([m_new] * (tk // LANES), axis=1))
        acc_sc[...] = alpha * acc_sc[...] + _dot(p.astype(BF16), vaug)
        m_sc[...] = m_new

    if window is None:
        def body(jt, c):
            tile(pl.multiple_of(jt * tk, tk), False)
            return c

        lax.fori_loop(0, i, body, 0)
        tile(pl.multiple_of(i * tk, tk), True)
    else:
        tile(pl.multiple_of(jnp.maximum(q0 - window, 0), LANES), True)

    acc = acc_sc[...]
    den = pltpu.roll(acc, HEAD_DIM, 1)
    m = m_sc[...]
    if use_sink:
        sink = jnp.concatenate([jnp.full((tq, LANES), sink_ref[g * hg + h], F32) for h in range(hg)], axis=0)
        m_fin = jnp.maximum(m, sink)
        w = jnp.exp(m - m_fin)
        o = acc * w / jnp.maximum(den * w + jnp.exp(sink - m_fin), 1e-30)
    else:
        o = jnp.where(m > 0.5 * NEG_INF, acc / jnp.maximum(den, 1e-30), 0.0)
    lane = lax.broadcasted_iota(jnp.int32, (1, LANES), 1)
    o = jnp.where(lane >= HEAD_DIM, o, 0.0)
    for h in range(hg):
        o_ref[:, h * LANES:(h + 1) * LANES] = o[h * tq:(h + 1) * tq].astype(o_ref.dtype)


def _flash(q, kv, *, n, t, hg, tq, window, selb=None, sinks=None):
    hc = q.shape[1]
    nq = t // tq
    g = KV_GROUPS
    use_sel = selb is not None
    use_sink = sinks is not None
    rows = hg * tq
    tk = tq if window is None else tq + window
    if use_sel:
        assert t // SLC_BLOCK <= LANES - HEAD_DIM, "selection bias needs one upper query lane per block"
    in_specs = []
    args = []
    if use_sink:
        in_specs.append(pl.BlockSpec(memory_space=pltpu.SMEM))
        args.append(sinks)
    in_specs += [
        pl.BlockSpec((tq, hg * LANES), lambda b, gg, i: (b * nq + i, gg)),
        pl.BlockSpec((t, LANES), lambda b, gg, i: (b, gg)),
    ]
    args += [q, kv]
    if use_sel:
        in_specs.append(pl.BlockSpec((1, 1, tq, LANES), lambda b, gg, i: (b, gg, i, 0)))
        args.append(selb)
    return pl.pallas_call(
        functools.partial(_flash_kernel, hg=hg, tq=tq, tk=tk, window=window, use_sel=use_sel,
                          use_sink=use_sink),
        out_shape=jax.ShapeDtypeStruct((n * t, hc), BF16),
        grid=(n, g, nq),
        in_specs=in_specs,
        out_specs=pl.BlockSpec((tq, hg * LANES), lambda b, gg, i: (b * nq + i, gg)),
        scratch_shapes=[pltpu.VMEM((rows, LANES), BF16), pltpu.VMEM((rows, LANES), F32),
                        pltpu.VMEM((rows, LANES), F32)],
        compiler_params=_cparams(("parallel", "parallel", "parallel")),
        name="flash",
    )(*args)


def _attn_kernel(*refs, hg, tq, bn, window, q_off, k_off, use_sink):
    refs = list(refs)
    sink_ref = refs.pop(0) if use_sink else None
    q_ref, buf_ref, new_ref, o_ref = refs
    qpos = _row_qpos(hg, tq, q_off)

    def valid(k0, width):
        d = qpos - (k0 + lax.broadcasted_iota(jnp.int32, (1, width), 1))
        return lax.bitcast_convert_type(d, jnp.uint32) <= jnp.uint32(window)

    valid_buf = valid(k_off, buf_ref.shape[2])
    valid_new = valid(q_off, new_ref.shape[2])
    for g in range(KV_GROUPS):
        if use_sink:
            sink = jnp.concatenate([jnp.full((tq, 1), sink_ref[g * hg + h], F32) for h in range(hg)], axis=0)
        for b in range(bn):
            qs = _stack_heads(q_ref[b, :, g * hg * LANES:(g + 1) * hg * LANES], hg)
            kb = buf_ref[b, g * LANES:(g + 1) * LANES, :].astype(BF16)
            kn = new_ref[b, g * LANES:(g + 1) * LANES, :].astype(BF16)
            sb = jnp.where(valid_buf, _dot(qs, kb), NEG_INF)
            sn = jnp.where(valid_new, _dot(qs, kn), NEG_INF)
            m = jnp.maximum(jnp.max(sb, axis=-1, keepdims=True), jnp.max(sn, axis=-1, keepdims=True))
            if use_sink:
                m = jnp.maximum(m, sink)
            pb = jnp.exp(sb - m)
            pn = jnp.exp(sn - m)
            den = jnp.sum(pb, axis=-1, keepdims=True) + jnp.sum(pn, axis=-1, keepdims=True)
            if use_sink:
                den = den + jnp.exp(sink - m)
            pv = _dot_nt(pb.astype(BF16), kb) + _dot_nt(pn.astype(BF16), kn)
            o = jnp.where(m > 0.5 * NEG_INF, pv / jnp.maximum(den, 1e-30), 0.0)
            _store_heads(o_ref, (b,), o, hg, tq, head0=g * hg)


def _attn(q, buf_t, new_t, *, hg, window, q_off, k_off, sinks=None, bn=8):
    n, tq, hc = q.shape
    use_sink = sinks is not None
    in_specs = []
    args = []
    if use_sink:
        in_specs.append(pl.BlockSpec(memory_space=pltpu.SMEM))
        args.append(sinks)
    in_specs += [
        pl.BlockSpec((bn, tq, hc), lambda b: (b, 0, 0)),
        pl.BlockSpec((bn,) + buf_t.shape[1:], lambda b: (b, 0, 0)),
        pl.BlockSpec((bn,) + new_t.shape[1:], lambda b: (b, 0, 0)),
    ]
    args += [q, buf_t, new_t]
    return pl.pallas_call(
        functools.partial(_attn_kernel, hg=hg, tq=tq, bn=bn, window=window, q_off=q_off, k_off=k_off,
                          use_sink=use_sink),
        out_shape=jax.ShapeDtypeStruct((n, tq, hc), F32),
        grid=(n // bn,),
        in_specs=in_specs,
        out_specs=pl.BlockSpec((bn, tq, hc), lambda b: (b, 0, 0)),
        compiler_params=_cparams(("parallel",)),
        name="attn",
    )(*args)


def _paged_slc_kernel(pt_ref, cache_ref, q_ref, new_ref, selb_ref, oh_ref, o_ref, buf, sem, *, n_pages, hg, tq,
                      q_off):
    past = n_pages * PAGE_SIZE
    tk = buf.shape[2]
    rng_keys = (LANES - HEAD_DIM) * SLC_BLOCK
    slot = _paged_prefetch(
        pt_ref, cache_ref, lambda sl, p: buf.at[sl, :, pl.ds(pl.multiple_of(p * PAGE_SIZE, PAGE_SIZE), PAGE_SIZE)],
        sem, n_pages)
    buf[slot, :, past:] = new_ref[0]
    kv_t = buf[slot].astype(BF16)
    rows = hg * tq
    zero = jnp.zeros((rows, LANES), F32)
    q = q_ref[0]
    stack = lambda g: jnp.concatenate([q[:, (g * hg + h) * LANES:(g * hg + h + 1) * LANES] for h in range(hg)],
                                      axis=0)
    oh = oh_ref[...]
    kaug_t = jnp.concatenate([kv_t[0:HEAD_DIM], oh, kv_t[LANES:LANES + HEAD_DIM], oh], axis=0)
    qg = [stack(g) for g in range(KV_GROUPS)]
    s_parts = []
    for rr in range(-(-tk // rng_keys)):
        qa = [qg[g] + jnp.concatenate([selb_ref[0, g, rr]] * hg, axis=0) for g in range(KV_GROUPS)]
        q2 = jnp.concatenate([jnp.concatenate([qa[0], zero], axis=1),
                              jnp.concatenate([zero, qa[1]], axis=1)], axis=0).astype(BF16)
        s_parts.append(_dot(q2, kaug_t[:, rr * rng_keys:min(tk, (rr + 1) * rng_keys)]))
    s = jnp.concatenate(s_parts, axis=1)
    r = lax.broadcasted_iota(jnp.int32, (2 * rows, 1), 0)
    qpos = q_off + (r & (tq - 1))
    s = jnp.where(qpos - lax.broadcasted_iota(jnp.int32, (1, tk), 1) >= 0, s, NEG_INF)
    m = jnp.max(s, axis=-1, keepdims=True)
    p = jnp.exp(s - m)
    den = jnp.sum(p, axis=-1, keepdims=True)
    o = jnp.where(m > 0.5 * NEG_INF, _dot_nt(p.astype(BF16), kv_t) / jnp.maximum(den, 1e-30), 0.0)
    lane = lax.broadcasted_iota(jnp.int32, (1, LANES), 1)
    for g in range(KV_GROUPS):
        for h in range(hg):
            blk = o[g * rows + h * tq:g * rows + (h + 1) * tq, g * LANES:(g + 1) * LANES]
            c0 = (g * hg + h) * LANES
            o_ref[0, :, c0:c0 + LANES] = jnp.where(lane >= HEAD_DIM, blk, 0.0)


def _paged_slc(page_table, cache_t, q, new_t, sel, *, hg, q_off):
    ns, n_pages = page_table.shape
    _, tq, hc = q.shape
    tk = (n_pages + 1) * PAGE_SIZE
    assert new_t.shape == (ns, KV_COLS, PAGE_SIZE)
    n_slot = LANES - HEAD_DIM
    n_rng = -(-tk // (n_slot * SLC_BLOCK))
    assert n_rng * n_slot <= sel.shape[-1]
    bias = (sel - 1.0) * (-NEG_INF)
    selb = jnp.stack([jnp.pad(bias[..., rr * n_slot:(rr + 1) * n_slot], ((0, 0), (0, 0), (0, 0), (HEAD_DIM, 0)))
                      for rr in range(n_rng)], axis=2)
    onehot = (((jnp.arange(tk)[None, :] >> SLC_SHIFT) & (n_slot - 1)) == jnp.arange(n_slot)[:, None]).astype(BF16)
    grid_spec = pltpu.PrefetchScalarGridSpec(
        num_scalar_prefetch=1,
        grid=(ns,),
        in_specs=[
            pl.BlockSpec(memory_space=pl.ANY),
            pl.BlockSpec((1, tq, hc), lambda b, pt: (b, 0, 0)),
            pl.BlockSpec((1, KV_COLS, PAGE_SIZE), lambda b, pt: (b, 0, 0)),
            pl.BlockSpec((1, KV_GROUPS, n_rng, tq, LANES), lambda b, pt: (b, 0, 0, 0, 0)),
            pl.BlockSpec(onehot.shape, lambda b, pt: (0, 0)),
        ],
        out_specs=pl.BlockSpec((1, tq, hc), lambda b, pt: (b, 0, 0)),
        scratch_shapes=[pltpu.VMEM((2, KV_COLS, tk), F32), pltpu.SemaphoreType.DMA((2,))],
    )
    return pl.pallas_call(
        functools.partial(_paged_slc_kernel, n_pages=n_pages, hg=hg, tq=tq, q_off=q_off),
        out_shape=jax.ShapeDtypeStruct((ns, tq, hc), F32),
        grid_spec=grid_spec,
        compiler_params=_cparams(("arbitrary",)),
        name="paged_slc",
    )(page_table.reshape(-1), cache_t, q, new_t, selb, onehot)


def _out0_kernel(*refs, bounds):
    np_ = len(bounds)
    x, pool, oc, osl, ow = (_row_part(refs[k * np_:(k + 1) * np_], bounds) for k in range(5))
    gt_ref, eg_ref, wa_ref, wb_ref, o_ref = refs[5 * np_:]
    ghi, glo = _split_bf16(gt_ref[...])
    o = None
    for b, br in enumerate((oc, osl, ow)):
        gate = _dot(ghi, eg_ref[b]) + _dot(glo, eg_ref[b])
        term = gate * br.astype(F32)
        o = term if o is None else o + term
    mix = _dot(pool, wa_ref[...]) + _dot(o.astype(BF16), wb_ref[...])
    o_ref[...] = x + mix


def _out0(x_parts, pool_parts, cmp_parts, slc_parts, win_parts, gates, eg, wa, wb, tm=512):
    m = sum(a.shape[0] for a in x_parts)
    d = x_parts[0].shape[1]
    row = lambda w: pl.BlockSpec((tm, w), lambda i: (i, 0))
    specs, args = [], []
    for parts in (x_parts, pool_parts, cmp_parts, slc_parts, win_parts):
        sp, bounds = _row_part_specs(parts, tm)
        specs += sp
        args += list(parts)
    return pl.pallas_call(
        functools.partial(_out0_kernel, bounds=bounds),
        out_shape=jax.ShapeDtypeStruct((m, d), F32),
        grid=(m // tm,),
        in_specs=specs + [row(LANES),
                          pl.BlockSpec(eg.shape, lambda i: (0, 0, 0)),
                          pl.BlockSpec(wa.shape, lambda i: (0, 0)),
                          pl.BlockSpec(wb.shape, lambda i: (0, 0))],
        out_specs=row(d),
        compiler_params=_cparams(("parallel",)),
        name="out0",
    )(*args, gates, eg, wa, wb)


def _ffn_kernel(x_ref, g_ref, wg_ref, wu_ref, wd_ref, o_ref, h_sc, acc_sc):
    f = pl.program_id(1)

    @pl.when(f == 0)
    def _():
        x = x_ref[...]
        ms = jnp.mean(x * x, axis=-1, keepdims=True)
        h_sc[...] = (x * lax.rsqrt(ms + RMS_EPS) * g_ref[...]).astype(BF16)
        acc_sc[...] = x

    h = h_sc[...]
    a = _dot(h, wg_ref[...])
    u = _dot(h, wu_ref[...])
    act = (a * jax.nn.sigmoid(a) * u).astype(BF16)
    acc_sc[...] += _dot(act, wd_ref[...])

    @pl.when(f == pl.num_programs(1) - 1)
    def _():
        o_ref[...] = acc_sc[...]


def _ffn(x, g, wg, wu, wd, tm=512, nf=2):
    m, d = x.shape
    fdim = wg.shape[1]
    tf = fdim // nf
    return pl.pallas_call(
        _ffn_kernel,
        out_shape=jax.ShapeDtypeStruct((m, d), F32),
        grid=(m // tm, nf),
        in_specs=[
            pl.BlockSpec((tm, d), lambda i, f: (i, 0)),
            pl.BlockSpec((1, d), lambda i, f: (0, 0)),
            pl.BlockSpec((d, tf), lambda i, f: (0, f)),
            pl.BlockSpec((d, tf), lambda i, f: (0, f)),
            pl.BlockSpec((tf, d), lambda i, f: (f, 0)),
        ],
        out_specs=pl.BlockSpec((tm, d), lambda i, f: (i, 0)),
        scratch_shapes=[pltpu.VMEM((tm, d), BF16), pltpu.VMEM((tm, d), F32)],
        compiler_params=_cparams(("parallel", "arbitrary")),
        name="ffn",
    )(x, g, wg, wu, wd)


def _out1_kernel(*refs, bounds):
    np_ = len(bounds)
    x_ref = refs[0]
    attn = _row_part(refs[1:1 + np_], bounds)
    w_ref, g_ref, rwh_ref, rwl_ref, rb_ref, xo_ref, h_ref, r_ref = refs[1 + np_:]
    x = x_ref[...] + _dot(attn, w_ref[...])
    xo_ref[...] = x
    ms = jnp.mean(x * x, axis=-1, keepdims=True)
    h = x * lax.rsqrt(ms + RMS_EPS) * g_ref[...]
    h_ref[...] = h
    hi, lo = _split_bf16(h)
    logits = _dot(hi, rwh_ref[...]) + _dot(lo, rwh_ref[...]) + _dot(hi, rwl_ref[...]) + rb_ref[...]
    lane = lax.broadcasted_iota(jnp.int32, logits.shape, 1)
    logits = jnp.where(lane < N_EXPERTS, logits, -jnp.inf)
    m1 = jnp.max(logits, axis=-1, keepdims=True)
    i1 = jnp.min(jnp.where(logits == m1, lane, LANES), axis=-1, keepdims=True)
    rest = jnp.where(lane == i1, -jnp.inf, logits)
    m2 = jnp.max(rest, axis=-1, keepdims=True)
    i2 = jnp.min(jnp.where(rest == m2, lane, LANES), axis=-1, keepdims=True)
    e2 = jnp.exp(m2 - m1)
    g1 = 1.0 / (1.0 + e2)
    g2 = e2 / (1.0 + e2)
    r = jnp.where(lane == 0, i1.astype(F32), 0.0)
    r = jnp.where(lane == 1, i2.astype(F32), r)
    r = jnp.where(lane == 2, g1, r)
    r = jnp.where(lane == 3, g2, r)
    r_ref[...] = r


def _out1(x, o_parts, w, g, rwh, rwl, rb, tm=512):
    m, d = x.shape
    row = lambda wd: pl.BlockSpec((tm, wd), lambda i: (i, 0))
    full = lambda a: pl.BlockSpec(a.shape, lambda i: (0, 0))
    o_specs, bounds = _row_part_specs(o_parts, tm)
    return pl.pallas_call(
        functools.partial(_out1_kernel, bounds=bounds),
        out_shape=[jax.ShapeDtypeStruct((m, d), F32), jax.ShapeDtypeStruct((m, d), F32),
                   jax.ShapeDtypeStruct((m, LANES), F32)],
        grid=(m // tm,),
        in_specs=[row(d)] + o_specs + [full(w), full(g), full(rwh), full(rwl), full(rb)],
        out_specs=[row(d), row(d), row(LANES)],
        compiler_params=_cparams(("parallel",)),
        name="out1",
    )(x, *o_parts, w, g, rwh, rwl, rb)


def _moe_kernel(be_ref, nb_ref, x_ref, wg_ref, wu_ref, wd_ref, o_ref, acc_sc):
    b = pl.program_id(0)
    f = pl.program_id(1)
    used = b < nb_ref[0]

    @pl.when(f == 0)
    def _():
        acc_sc[...] = jnp.zeros_like(acc_sc)

    @pl.when(used)
    def _():
        x = x_ref[...]
        a = _dot(x, wg_ref[0])
        u = _dot(x, wu_ref[0])
        act = (a * jax.nn.sigmoid(a) * u).astype(BF16)
        acc_sc[...] += _dot(act, wd_ref[0])

    @pl.when(f == pl.num_programs(1) - 1)
    def _():
        o_ref[...] = acc_sc[...]


def _moe(xs, blk_e, n_used, wg, wu, wd, tm, tf):
    cap, d = xs.shape
    fdim = wg.shape[2]
    grid_spec = pltpu.PrefetchScalarGridSpec(
        num_scalar_prefetch=2,
        grid=(cap // tm, fdim // tf),
        in_specs=[
            pl.BlockSpec((tm, d), lambda b, f, be, nb: (b, 0)),
            pl.BlockSpec((1, d, tf), lambda b, f, be, nb: (be[b], 0, f)),
            pl.BlockSpec((1, d, tf), lambda b, f, be, nb: (be[b], 0, f)),
            pl.BlockSpec((1, tf, d), lambda b, f, be, nb: (be[b], f, 0)),
        ],
        out_specs=pl.BlockSpec((tm, d), lambda b, f, be, nb: (b, 0)),
        scratch_shapes=[pltpu.VMEM((tm, d), F32)],
    )
    return pl.pallas_call(
        _moe_kernel,
        out_shape=jax.ShapeDtypeStruct((cap, d), F32),
        grid_spec=grid_spec,
        compiler_params=_cparams(("parallel", "arbitrary")),
        name="moe",
    )(blk_e, n_used, xs, wg, wu, wd)


def _start_rows(idx_ref, src_ref, dst, sem, base, n_rows):
    def start(r, c):
        pltpu.make_async_copy(src_ref.at[pl.ds(idx_ref[base + r], 1)], dst.at[pl.ds(r, 1)], sem).start()
        return c

    lax.fori_loop(0, n_rows, start, 0, unroll=8)


def _wait_rows(src_ref, dst, sem, n_rows):
    def wait(r, c):
        pltpu.make_async_copy(src_ref.at[pl.ds(0, 1)], dst.at[pl.ds(r, 1)], sem).wait()
        return c

    lax.fori_loop(0, n_rows, wait, 0, unroll=8)


def _dispatch_kernel(idx_ref, h_ref, o_ref, rows_sc, sem, *, tm):
    _start_rows(idx_ref, h_ref, rows_sc, sem.at[0], pl.program_id(0) * tm, tm)
    _wait_rows(h_ref, rows_sc, sem.at[0], tm)
    o_ref[...] = rows_sc[...].astype(o_ref.dtype)


def _dispatch(row_tok, h, tm):
    cap = row_tok.shape[0]
    d = h.shape[1]
    grid_spec = pltpu.PrefetchScalarGridSpec(
        num_scalar_prefetch=1,
        grid=(cap // tm,),
        in_specs=[pl.BlockSpec(memory_space=pl.ANY)],
        out_specs=pl.BlockSpec((tm, d), lambda b, idx: (b, 0)),
        scratch_shapes=[pltpu.VMEM((tm, d), h.dtype), pltpu.SemaphoreType.DMA((1,))],
    )
    return pl.pallas_call(
        functools.partial(_dispatch_kernel, tm=tm),
        out_shape=jax.ShapeDtypeStruct((cap, d), BF16),
        grid_spec=grid_spec,
        compiler_params=_cparams(("arbitrary",)),
        name="moe_dispatch",
    )(row_tok, h)


def _combine_kernel(dest_ref, x_ref, r_ref, ys_ref, o_ref, rows_sc, sem, *, tm, n_tok):
    t0 = pl.program_id(0) * tm
    for k in range(2):
        _start_rows(dest_ref, ys_ref, rows_sc.at[k], sem.at[0], k * n_tok + t0, tm)
    for k in range(2):
        _wait_rows(ys_ref, rows_sc.at[k], sem.at[0], tm)
    r = r_ref[...]
    o_ref[...] = x_ref[...] + (r[:, 2:3] * rows_sc[0] + r[:, 3:4] * rows_sc[1])


def _combine(dest, x, r, ys, tm=512):
    m, d = x.shape
    grid_spec = pltpu.PrefetchScalarGridSpec(
        num_scalar_prefetch=1,
        grid=(m // tm,),
        in_specs=[
            pl.BlockSpec((tm, d), lambda b, idx: (b, 0)),
            pl.BlockSpec((tm, LANES), lambda b, idx: (b, 0)),
            pl.BlockSpec(memory_space=pl.ANY),
        ],
        out_specs=pl.BlockSpec((tm, d), lambda b, idx: (b, 0)),
        scratch_shapes=[pltpu.VMEM((2, tm, d), F32), pltpu.SemaphoreType.DMA((1,))],
    )
    return pl.pallas_call(
        functools.partial(_combine_kernel, tm=tm, n_tok=m),
        out_shape=jax.ShapeDtypeStruct((m, d), F32),
        grid_spec=grid_spec,
        compiler_params=_cparams(("arbitrary",)),
        name="moe_combine",
    )(dest, x, r, ys)


def _pad_heads_cols(w, heads):
    d = w.shape[0]
    w = w.reshape(d, heads, HEAD_DIM)
    return jnp.pad(w, ((0, 0), (0, 0), (0, LANES - HEAD_DIM))).reshape(d, heads * LANES)


def _pad_heads_rows(w, heads):
    d = w.shape[1]
    w = w.reshape(heads, HEAD_DIM, d)
    return jnp.pad(w, ((0, 0), (LANES - HEAD_DIM, 0), (0, 0))).reshape(heads * LANES, d)


def _head_gain(gain, heads, scale):
    g = jnp.pad(gain.astype(F32) * scale, (0, LANES - HEAD_DIM))
    return jnp.tile(g, heads)


def _key_gain(gain):
    return jnp.concatenate([gain.astype(F32), jnp.ones((HEAD_DIM,), F32)])


def _cover_t(n_cmp_pad, n_cmp, n_blocks, nb_pad, row_off):
    m = jnp.arange(n_cmp_pad)
    c_start = m * CMP_STRIDE
    c_end = c_start + CMP_BLOCK - 1
    jb = jnp.arange(nb_pad) - row_off
    b_start = jb * SLC_BLOCK
    b_end = b_start + SLC_BLOCK - 1
    cov = (c_start[None, :] <= b_end[:, None]) & (c_end[None, :] >= b_start[:, None])
    cov = cov & (m[None, :] < n_cmp) & (jb[:, None] >= 0) & (jb[:, None] < n_blocks)
    return cov.astype(BF16)


def _route(r, n_tok, tm):
    top_e = r[:, 0:2].astype(jnp.int32)
    n_asg = n_tok * 2
    flat_e = top_e.reshape(n_asg)
    onehot = (flat_e[:, None] == jnp.arange(N_EXPERTS)[None, :]).astype(jnp.int32)
    csum = jnp.cumsum(onehot, axis=0)
    counts = csum[-1]
    padded = (counts + tm - 1) // tm * tm
    start = jnp.cumsum(counts) - counts
    pend = jnp.cumsum(padded)
    pstart = pend - padded
    dest = jnp.sum(onehot * (pstart[None, :] + csum - onehot), axis=1)
    n_blk = n_asg // tm + N_EXPERTS
    cap = n_blk * tm
    blk_e = jnp.minimum(jnp.sum(jnp.arange(n_blk)[:, None] * tm >= pend[None, :], axis=1), N_EXPERTS - 1)
    order = jnp.argsort(flat_e, stable=True)
    k = jnp.arange(cap, dtype=jnp.int32) - jnp.repeat(pstart[blk_e], tm)
    src = jnp.clip(jnp.repeat(start[blk_e], tm) + k, 0, n_asg - 1)
    row_tok = jnp.where(k < jnp.repeat(counts[blk_e], tm), order[src] // 2, 0).astype(jnp.int32)
    n_used = (pend[-1] // tm).astype(jnp.int32).reshape(1)
    dest_t = dest.astype(jnp.int32).reshape(n_tok, 2).T.reshape(n_asg)
    return dest_t, row_tok, blk_e.astype(jnp.int32), n_used


def kernel(x_prompt, x_sample, state_pool, cache_nsa_cmp, cache_nsa_slc, cache_nsa_win, cache_swa, page_table, norm0_mix, w_in0, pool_w, pool_scale, nsa_q_gain, nsa_k_gain, cmp_k_w1, cmp_k_w2, cmp_k_pe, cmp_v_w1, cmp_v_w2, cmp_v_pe, w_out0, norm0_ffn, ffn_w_gate, ffn_w_up, ffn_w_down, norm1_mix, w_in1, swa_q_gain, swa_k_gain, swa_sinks, w_out1, norm1_ffn, router_w, router_b, moe_w_gate, moe_w_up, moe_w_down):
    nb, t, d = x_prompt.shape
    ns, s_len, _ = x_sample.shape
    mp = nb * t
    msz = ns * s_len
    past = page_table.shape[1] * PAGE_SIZE
    total = past + s_len
    g = KV_GROUPS

    x_parts = (x_prompt.reshape(mp, d), x_sample.reshape(msz, d))

    c_q = POOL_CH
    c_kv = c_q + NSA_HEADS * HEAD_DIM
    c_gate = c_kv + 3 * KV_COLS
    n_gate = 3 * NSA_HEADS
    w0 = jnp.concatenate([
        w_in0[:, :c_q],
        _pad_heads_cols(w_in0[:, c_q:c_kv], NSA_HEADS),
        w_in0[:, c_kv:c_gate],
        jnp.pad(w_in0[:, c_gate:], ((0, 0), (0, LANES - n_gate))),
    ], axis=1).astype(BF16)
    qw = NSA_HEADS * LANES
    aux0 = jnp.concatenate([
        jnp.ones((POOL_CH,), F32),
        _head_gain(nsa_q_gain, NSA_HEADS, ATTN_SCALE),
        jnp.ones((KV_COLS,), F32),
        jnp.tile(_key_gain(nsa_k_gain[1]), g),
        jnp.tile(_key_gain(nsa_k_gain[2]), g),
        jnp.ones((LANES,), F32),
    ]).reshape(1, -1)
    o0 = POOL_CH
    o1 = o0 + qw
    o2 = o1 + KV_COLS
    o3 = o2 + KV_COLS
    o4 = o3 + KV_COLS
    segs0 = (("raw", 0, POOL_CH), ("qnorm", o0, qw), ("raw", o1, KV_COLS), ("kvnorm", o2, KV_COLS),
             ("kvnorm", o3, KV_COLS), ("sigmoid", o4, LANES))
    outs0 = ((POOL_CH, F32), (qw, F32), (KV_COLS, F32), (KV_COLS, F32), (KV_COLS, BF16),
             (KV_COLS, F32), (KV_COLS, BF16), (LANES, F32))
    u_all, q_all, cmp_all, slc_all, slc_bf, win_all, win_bf, gates_all = _proj(
        x_parts, norm0_mix.reshape(1, d), w0, aux0, segs0, outs0, 512, "proj0")

    pool_w_bf = pool_w.astype(BF16)
    pool_scale2 = pool_scale.reshape(1, POOL_CH).astype(F32)

    u_p = u_all[:mp].reshape(nb, t, POOL_CH)
    u_s = u_all[mp:].reshape(ns, s_len, POOL_CH)
    pool_o_p = _pool_prompt(u_p, pool_w_bf, pool_scale2)
    u_ext = jnp.concatenate([state_pool.astype(F32), u_s], axis=1)
    x_ext = jnp.pad(u_ext, ((0, 0), (1, 0), (0, 0)))
    pool_o_s = _pool_sample(x_ext, pool_w_bf, pool_scale2, s_len)
    pool_parts = (pool_o_p.reshape(mp, POOL_CH), pool_o_s.reshape(msz, POOL_CH).astype(BF16))

    def w1_parts(w1):
        return w1.reshape(2, CMP_STRIDE, HEAD_DIM, CMP_HIDDEN)

    w_sel = jnp.stack([w1_parts(cmp_k_w1), w1_parts(cmp_v_w1)] * g, axis=0)
    w1_bd = jnp.einsum("cpjdh,ce->jcdpeh", w_sel, jnp.eye(2 * g, dtype=F32))
    w1_bd = w1_bd.reshape(CMP_STRIDE, KV_COLS, 2 * 2 * g * CMP_HIDDEN).astype(BF16)
    w2_sel = jnp.stack([cmp_k_w2, cmp_v_w2] * g, axis=0)
    w2_bd = jnp.einsum("chd,ce->ched", w2_sel, jnp.eye(2 * g, dtype=F32))
    w2_bd = w2_bd.reshape(2 * g * CMP_HIDDEN, KV_COLS).astype(BF16)
    pek = jnp.pad(cmp_k_pe.reshape(1, -1), ((0, 7), (0, 0))).astype(BF16)
    pev = jnp.pad(cmp_v_pe.reshape(1, -1), ((0, 7), (0, 0))).astype(BF16)
    w1k = cmp_k_w1.astype(BF16)
    w1v = cmp_v_w1.astype(BF16)
    kg0 = _key_gain(nsa_k_gain[0]).reshape(1, LANES)
    seg_cols = CMP_STRIDE * KV_COLS
    half = 2 * g * CMP_HIDDEN

    n_seg_p = t // CMP_STRIDE
    pa_p, pb_p = _compress1(cmp_all[:mp].reshape(nb * n_seg_p, seg_cols), w1_bd, 256)
    kcv_p = _compress2(pa_p.reshape(nb, n_seg_p, half), pb_p.reshape(nb, n_seg_p, half),
                       jnp.zeros((nb, 8, half), F32), pek, pev, w1k, w1v, w2_bd, kg0)
    n_seg_s = past // CMP_STRIDE
    pad_new = -(-total // CMP_STRIDE) * CMP_STRIDE - past
    assert pad_new == CMP_STRIDE, "the new rows must fit one segment"
    new_seg = jnp.pad(cmp_all[mp:].reshape(ns, s_len * KV_COLS), ((0, 0), (0, (pad_new - s_len) * KV_COLS)))
    _, pb_new = _compress1(new_seg, w1_bd, ns)
    pb_new = jnp.pad(pb_new.reshape(ns, 1, half), ((0, 0), (0, 7), (0, 0)))
    w_kv = jnp.stack([w1_parts(cmp_k_w1), w1_parts(cmp_v_w1)], axis=0)
    w1_gs = jnp.einsum("kpjdh,ke->jkdpeh", w_kv, jnp.eye(2, dtype=F32))
    w1_gs = w1_gs.reshape(CMP_STRIDE * 2 * HEAD_DIM, 2 * 2 * CMP_HIDDEN).astype(BF16)
    kcv_s = _paged_compress(page_table, _pages_t(cache_nsa_cmp), pb_new, w1_gs, pek, pev, w1k, w1v, w2_bd, kg0)

    hg0 = NSA_HEADS // g
    q_p = q_all[:mp].reshape(nb, t, qw)
    q_s = q_all[mp:].reshape(ns, s_len, qw)
    nblk_p = t // SLC_BLOCK
    cov_p = _cover_t(n_seg_p, n_seg_p - 1, nblk_p, LANES, HEAD_DIM)
    o_cmp_p, selb_p = _cmp_attn(q_p, kcv_p, cov_p, hg=hg0, tq=256, bn=1, q_off=0, n_blocks=nblk_p,
                                out_dtype=BF16, row_off=HEAD_DIM, as_bias=True)
    nblk_s = -(-total // SLC_BLOCK)
    nbp_s = -(-nblk_s // LANES) * LANES
    n_cmp_s = -(-total // CMP_STRIDE) - 1
    cov_s = _cover_t(n_seg_s, n_cmp_s, nblk_s, nbp_s, 0)
    o_cmp_s, sel_s = _cmp_attn(q_s, kcv_s, cov_s, hg=hg0, tq=s_len, bn=LANES // s_len, q_off=past,
                               n_blocks=nblk_s, out_dtype=F32, row_off=0, as_bias=False)

    o_slc_p = _flash(q_all, slc_bf, n=nb, t=t, hg=hg0, tq=512, window=None, selb=selb_p)
    new_t = lambda rows: jnp.pad(jnp.transpose(rows.reshape(ns, s_len, KV_COLS), (0, 2, 1)),
                                 ((0, 0), (0, 0), (0, PAGE_SIZE - s_len)))
    o_slc_s = _paged_slc(page_table, _pages_t(cache_nsa_slc), q_s, new_t(slc_all[mp:]), sel_s, hg=hg0, q_off=past)

    o_win_p = _flash(q_all, win_bf, n=nb, t=t, hg=hg0, tq=256, window=NSA_WINDOW)
    wb_len = cache_nsa_win.shape[1]
    win_buf_t = _pages_t(cache_nsa_win)
    win_new_t = new_t(win_all[mp:])
    o_win_s = _attn(q_s, win_buf_t, win_new_t, hg=hg0, window=NSA_WINDOW, q_off=past, k_off=past - wb_len)

    parts = lambda a, b: (a.reshape(mp, -1), b.reshape(msz, -1).astype(BF16))
    eg = (jnp.arange(LANES)[None, :, None] ==
          (jnp.arange(3)[:, None, None] * NSA_HEADS + jnp.arange(qw)[None, None, :] // LANES)).astype(BF16)
    wa0 = w_out0[:POOL_CH].astype(BF16)
    wb0 = _pad_heads_rows(w_out0[POOL_CH:], NSA_HEADS).astype(BF16)
    x1 = _out0(x_parts, pool_parts, parts(o_cmp_p, o_cmp_s), parts(o_slc_p, o_slc_s), parts(o_win_p, o_win_s),
               gates_all, eg, wa0, wb0)
    x2 = _ffn(x1, norm0_ffn.reshape(1, d), ffn_w_gate.astype(BF16), ffn_w_up.astype(BF16),
              ffn_w_down.astype(BF16))

    c_q1 = SWA_HEADS * HEAD_DIM
    qw1 = SWA_HEADS * LANES
    w1p = jnp.concatenate([_pad_heads_cols(w_in1[:, :c_q1], SWA_HEADS), w_in1[:, c_q1:]], axis=1).astype(BF16)
    aux1 = jnp.concatenate([_head_gain(swa_q_gain, SWA_HEADS, ATTN_SCALE),
                            jnp.tile(_key_gain(swa_k_gain), g)]).reshape(1, -1)
    segs1 = (("qnorm", 0, qw1), ("kvnorm", qw1, KV_COLS))
    outs1 = ((qw1, F32), (KV_COLS, F32), (KV_COLS, BF16))
    q1_all, swa_all, swa_bf = _proj((x2,), norm1_mix.reshape(1, d), w1p, aux1, segs1, outs1, 512, "proj1")
    hg1 = SWA_HEADS // g
    sinks = swa_sinks.astype(F32)
    q1_s = q1_all[mp:].reshape(ns, s_len, qw1)
    o1_p = _flash(q1_all, swa_bf, n=nb, t=t, hg=hg1, tq=128, window=SWA_WINDOW, sinks=sinks)
    sb_len = cache_swa.shape[1]
    swa_buf_t = _pages_t(cache_swa)
    swa_new_t = new_t(swa_all[mp:])
    o1_s = _attn(q1_s, swa_buf_t, swa_new_t, hg=hg1, window=SWA_WINDOW, q_off=past, k_off=past - sb_len,
                 sinks=sinks)

    rw = jnp.pad(router_w.astype(F32), ((0, 0), (0, LANES - N_EXPERTS)))
    rwh, rwl = _split_bf16(rw)
    rb = jnp.pad(router_b.astype(F32), (0, LANES - N_EXPERTS)).reshape(1, LANES)
    x3, h3, r = _out1(x2, parts(o1_p, o1_s), _pad_heads_rows(w_out1, SWA_HEADS).astype(BF16),
                      norm1_ffn.reshape(1, d), rwh, rwl, rb)
    m_all = mp + msz
    tm_moe = 512
    dest_t, row_tok, blk_e, n_used = _route(r, m_all, tm_moe)
    n_blk = row_tok.shape[0] // tm_moe
    xs = _dispatch(row_tok, h3, tm_moe * max(k for k in (4, 2, 1) if n_blk % k == 0))
    ys = _moe(xs, blk_e, n_used, moe_w_gate.astype(BF16), moe_w_up.astype(BF16), moe_w_down.astype(BF16),
              tm_moe, moe_w_gate.shape[2] // 2)
    x4 = _combine(dest_t, x3, r, ys)

    y_p = x4[:mp].reshape(nb, t, d)
    y_s = x4[mp:].reshape(ns, s_len, d)
    rows5 = lambda a, n, l: a.reshape(n, l, g, 2, HEAD_DIM)
    keep_w = min(NSA_WINDOW, t)
    keep_s = min(SWA_WINDOW, t)
    pool_p = u_p[:, t - POOL_STATE:]
    cmp_p = rows5(cmp_all[:mp], nb, t)
    slc_p_out = rows5(slc_all[:mp], nb, t)
    win_p_out = rows5(win_all[:mp], nb, t)[:, t - keep_w:]
    swa_p_out = rows5(swa_all[:mp], nb, t)[:, t - keep_s:]
    pool_s = u_ext[:, -POOL_STATE:]
    cmp_s = rows5(cmp_all[mp:], ns, s_len)
    slc_s_out = rows5(slc_all[mp:], ns, s_len)
    def shifted(buf_t, add_t, length):
        out_t = jnp.concatenate([buf_t[:, :, s_len:], add_t[:, :, :s_len]], axis=2)
        return jnp.transpose(out_t.reshape(ns, g, 2, HEAD_DIM, length), (0, 4, 1, 2, 3))

    win_s_out = shifted(win_buf_t, win_new_t, wb_len)
    swa_s_out = shifted(swa_buf_t, swa_new_t, sb_len)
    return (y_p, y_s, pool_p, cmp_p, slc_p_out, win_p_out, swa_p_out, pool_s, cmp_s, slc_s_out, win_s_out,
            swa_s_out)
```

```python
import functools

import jax
import jax.numpy as jnp
from jax import lax
from jax.experimental import pallas as pl
from jax.experimental.pallas import tpu as pltpu

F32 = jnp.float32
BF16 = jnp.bfloat16

LANES = 128
MXU_COLS = 256
HEAD_DIM = 64
D_MODEL = 1024
POOL_WINDOWS = (2, 4, 8, 16)
POOL_CH = 512
POOL_GROUP = 128
POOL_STATE = 15
POOL_HALO = 16
NSA_HEADS = 8
SWA_HEADS = 16
KV_GROUPS = 2
KV_COLS = KV_GROUPS * 2 * HEAD_DIM
CMP_BLOCK = 32
CMP_STRIDE = 16
CMP_HIDDEN = 128
SLC_BLOCK = 64
SLC_SHIFT = SLC_BLOCK.bit_length() - 1
SLC_TOPN = 16
NSA_WINDOW = 512
SWA_WINDOW = 128
PAGE_SIZE = 128
N_EXPERTS = 8
RMS_EPS = 1e-6
NEG_INF = -1e30
FORCE_SCORE = 1e9
PAD_SCORE = -3e38
ATTN_SCALE = HEAD_DIM ** -0.5
VMEM_LIMIT = 56 * 1024 * 1024


def _cparams(sem):
    return pltpu.CompilerParams(dimension_semantics=sem, vmem_limit_bytes=VMEM_LIMIT)


def _split_bf16(x):
    hi = x.astype(BF16)
    lo = (x - hi.astype(F32)).astype(BF16)
    return hi, lo


def _dot(a, b):
    return jnp.dot(a, b, preferred_element_type=F32)


def _dot_nt(a, b):
    return lax.dot_general(a, b, (((1,), (1,)), ((), ())), preferred_element_type=F32)


def _row_part_specs(parts, tm):
    specs, bounds, t0 = [], [], 0
    for a in parts:
        nt = a.shape[0] // tm
        assert nt * tm == a.shape[0]
        specs.append(pl.BlockSpec((tm, a.shape[1]), lambda i, t0=t0, nt=nt: (jnp.clip(i - t0, 0, nt - 1), 0)))
        t0 += nt
        bounds.append(t0)
    return specs, tuple(bounds)


def _row_part(refs, bounds):
    i = pl.program_id(0)
    v = refs[-1][...]
    for k in range(len(refs) - 2, -1, -1):
        v = jnp.where(i < bounds[k], refs[k][...], v)
    return v


def _proj_kernel(*refs, segs, bounds):
    np_ = len(bounds)
    x = _row_part(refs[:np_], bounds)
    g_ref, w_ref, aux_ref = refs[np_:np_ + 3]
    out_refs = refs[np_ + 3:]
    ms = jnp.mean(x * x, axis=-1, keepdims=True)
    h = (x * lax.rsqrt(ms + RMS_EPS) * g_ref[...]).astype(BF16)
    lane = lax.broadcasted_iota(jnp.int32, (1, LANES), 1)
    is_key = lane < HEAD_DIM
    oi = 0
    for kind, c0, width in segs:
        if kind == "qnorm":
            for pair in range(width // MXU_COLS):
                cp = c0 + pair * MXU_COLS
                z = _dot(h, w_ref[:, cp:cp + MXU_COLS])
                for hh in range(MXU_COLS // LANES):
                    zh = z[:, hh * LANES:(hh + 1) * LANES]
                    msq = jnp.sum(zh * zh, axis=-1, keepdims=True) * (1.0 / HEAD_DIM)
                    lo = cp - c0 + hh * LANES
                    gain = aux_ref[:, cp + hh * LANES:cp + (hh + 1) * LANES]
                    out_refs[oi][:, lo:lo + LANES] = zh * lax.rsqrt(msq + RMS_EPS) * gain
            oi += 1
            continue
        z = _dot(h, w_ref[:, c0:c0 + width])
        aux = aux_ref[:, c0:c0 + width]
        if kind == "raw":
            out_refs[oi][...] = z
            oi += 1
        elif kind == "sigmoid":
            out_refs[oi][...] = jax.nn.sigmoid(z)
            oi += 1
        elif kind == "kvnorm":
            for gg in range(width // LANES):
                sl = slice(gg * LANES, (gg + 1) * LANES)
                zb = z[:, sl]
                msq = jnp.sum(jnp.where(is_key, zb * zb, 0.0), axis=-1, keepdims=True) * (1.0 / HEAD_DIM)
                y = jnp.where(is_key, zb * lax.rsqrt(msq + RMS_EPS) * aux[:, sl], zb)
                out_refs[oi][:, sl] = y
                out_refs[oi + 1][:, sl] = y.astype(BF16)
            oi += 2
        else:
            raise ValueError(kind)


def _proj(x_parts, g, w, aux, segs, out_defs, tm, name):
    m = sum(a.shape[0] for a in x_parts)
    d = x_parts[0].shape[1]
    c = w.shape[1]
    out_shape = [jax.ShapeDtypeStruct((m, wd), dt) for wd, dt in out_defs]
    out_specs = [pl.BlockSpec((tm, wd), lambda i: (i, 0)) for wd, _ in out_defs]
    x_specs, bounds = _row_part_specs(x_parts, tm)
    return pl.pallas_call(
        functools.partial(_proj_kernel, segs=segs, bounds=bounds),
        out_shape=out_shape,
        grid=(m // tm,),
        in_specs=x_specs + [
            pl.BlockSpec((1, d), lambda i: (0, 0)),
            pl.BlockSpec((d, c), lambda i: (0, 0)),
            pl.BlockSpec((1, c), lambda i: (0, 0)),
        ],
        out_specs=out_specs,
        compiler_params=_cparams(("parallel",)),
        name=name,
    )(*x_parts, g, w, aux)


def _pool_prompt_kernel(u_ref, halo_ref, pw_ref, ps_ref, o_ref, e_ref, *, tp):
    i = pl.program_id(1)
    e_ref[0:POOL_HALO, :] = jnp.where(i > 0, halo_ref[0], 0.0)
    e_ref[POOL_HALO:, :] = u_ref[0]
    t = i * tp + lax.broadcasted_iota(jnp.int32, (tp, 1), 0)
    for gi, w in enumerate(POOL_WINDOWS):
        sl = slice(gi * POOL_GROUP, (gi + 1) * POOL_GROUP)
        tok = e_ref[POOL_HALO:POOL_HALO + tp, sl]
        s = tok
        for k in range(1, w):
            s = s + e_ref[POOL_HALO - k:POOL_HALO - k + tp, sl]
        cnt = jnp.minimum(t + 1, w).astype(F32)
        dg = s / cnt - tok
        y = _dot(dg.astype(BF16), pw_ref[gi]) * ps_ref[:, sl]
        o_ref[0, :, sl] = y.astype(o_ref.dtype)


def _pool_prompt(u, pool_w, pool_scale, tp=512):
    n, t, c = u.shape
    hb = tp // POOL_HALO
    return pl.pallas_call(
        functools.partial(_pool_prompt_kernel, tp=tp),
        out_shape=jax.ShapeDtypeStruct((n, t, c), BF16),
        grid=(n, t // tp),
        in_specs=[
            pl.BlockSpec((1, tp, c), lambda b, i: (b, i, 0)),
            pl.BlockSpec((1, POOL_HALO, c), lambda b, i: (b, jnp.maximum(i * hb - 1, 0), 0)),
            pl.BlockSpec((len(POOL_WINDOWS), POOL_GROUP, POOL_GROUP), lambda b, i: (0, 0, 0)),
            pl.BlockSpec((1, c), lambda b, i: (0, 0)),
        ],
        out_specs=pl.BlockSpec((1, tp, c), lambda b, i: (b, i, 0)),
        scratch_shapes=[pltpu.VMEM((tp + POOL_HALO, c), F32)],
        compiler_params=_cparams(("parallel", "parallel")),
        name="pool_prompt",
    )(u, u, pool_w, pool_scale)


def _pool_sample_kernel(x_ref, pw_ref, ps_ref, o_ref, *, bn, s_len):
    base = 1 + POOL_STATE
    for gi, w in enumerate(POOL_WINDOWS):
        sl = slice(gi * POOL_GROUP, (gi + 1) * POOL_GROUP)
        tok = x_ref[:, base:base + s_len, sl]
        s = tok
        for k in range(1, w):
            s = s + x_ref[:, base - k:base - k + s_len, sl]
        dg = (s / float(w) - tok).reshape(bn * s_len, POOL_GROUP)
        y = _dot(dg.astype(BF16), pw_ref[gi]) * ps_ref[:, sl]
        o_ref[:, :, sl] = y.reshape(bn, s_len, POOL_GROUP).astype(o_ref.dtype)


def _pool_sample(x_ext, pool_w, pool_scale, s_len, bn=16):
    n, l, c = x_ext.shape
    return pl.pallas_call(
        functools.partial(_pool_sample_kernel, bn=bn, s_len=s_len),
        out_shape=jax.ShapeDtypeStruct((n, s_len, c), F32),
        grid=(n // bn,),
        in_specs=[
            pl.BlockSpec((bn, l, c), lambda b: (b, 0, 0)),
            pl.BlockSpec((len(POOL_WINDOWS), POOL_GROUP, POOL_GROUP), lambda b: (0, 0, 0)),
            pl.BlockSpec((1, c), lambda b: (0, 0)),
        ],
        out_specs=pl.BlockSpec((bn, s_len, c), lambda b: (b, 0, 0)),
        compiler_params=_cparams(("parallel",)),
        name="pool_sample",
    )(x_ext, pool_w, pool_scale)


def _compress1_kernel(x_ref, w_ref, pa_ref, pb_ref):
    acc = None
    for j in range(CMP_STRIDE):
        xj = x_ref[:, j * KV_COLS:(j + 1) * KV_COLS].astype(BF16)
        d = _dot(xj, w_ref[j])
        acc = d if acc is None else acc + d
    half = 4 * CMP_HIDDEN
    pa_ref[...] = acc[:, :half]
    pb_ref[...] = acc[:, half:]


def _compress1(x, w_bd, ts):
    r, c = x.shape
    ts = min(ts, r)
    half = 4 * CMP_HIDDEN
    return pl.pallas_call(
        _compress1_kernel,
        out_shape=[jax.ShapeDtypeStruct((r, half), F32), jax.ShapeDtypeStruct((r, half), F32)],
        grid=(r // ts,),
        in_specs=[
            pl.BlockSpec((ts, c), lambda i: (i, 0)),
            pl.BlockSpec(w_bd.shape, lambda i: (0, 0, 0)),
        ],
        out_specs=[pl.BlockSpec((ts, half), lambda i: (i, 0)), pl.BlockSpec((ts, half), lambda i: (i, 0))],
        compiler_params=_cparams(("parallel",)),
        name="compress1",
    )(x, w_bd)


def _gelu_tanh(x):
    return 0.5 * x * (1.0 + jnp.tanh(0.7978845608028654 * (x + 0.044715 * (x * x * x))))


def _compress_tail(pa, pb, pb_new, pek_ref, pev_ref, w1k_ref, w1v_ref, w2_ref, kg_ref, o_ref):
    s = pa.shape[0]
    row = lax.broadcasted_iota(jnp.int32, (s, 1), 0)
    pb_next = jnp.where(row == s - 1, pb_new, pltpu.roll(pb, s - 1, 0))
    hk = _dot(pek_ref[...], w1k_ref[...])[0:1, :]
    hv = _dot(pev_ref[...], w1v_ref[...])[0:1, :]
    pe_h = jnp.concatenate([hk, hv, hk, hv], axis=1)
    a = _gelu_tanh(pa + pb_next + pe_h)
    o = _dot(a.astype(BF16), w2_ref[...])
    lane = lax.broadcasted_iota(jnp.int32, (1, LANES), 1)
    is_key = lane < HEAD_DIM
    for gg in range(KV_GROUPS):
        sl = slice(gg * LANES, (gg + 1) * LANES)
        ob = o[:, sl]
        msq = jnp.sum(jnp.where(is_key, ob * ob, 0.0), axis=-1, keepdims=True) * (1.0 / HEAD_DIM)
        y = jnp.where(is_key, ob * lax.rsqrt(msq + RMS_EPS) * kg_ref[...], ob)
        o_ref[0, :, sl] = y.astype(o_ref.dtype)


def _compress2_kernel(pa_ref, pb_ref, pbn_ref, pek_ref, pev_ref, w1k_ref, w1v_ref, w2_ref, kg_ref, o_ref):
    _compress_tail(pa_ref[0], pb_ref[0], pbn_ref[0][0:1, :], pek_ref, pev_ref, w1k_ref, w1v_ref, w2_ref,
                   kg_ref, o_ref)


def _pages_t(cache):
    return jnp.transpose(cache, (0, 2, 3, 4, 1)).reshape(cache.shape[0], KV_COLS, cache.shape[1])


def _paged_prefetch(pt_ref, cache_ref, dst, sem, n_pages):
    b = pl.program_id(0)
    slot = b % 2

    def fetch(bb, sl):
        def body(p, c):
            pltpu.make_async_copy(cache_ref.at[pt_ref[bb * n_pages + p]], dst(sl, p), sem.at[sl]).start()
            return c

        lax.fori_loop(0, n_pages, body, 0)

    @pl.when(b == 0)
    def _():
        fetch(0, 0)

    @pl.when(b + 1 < pl.num_programs(0))
    def _():
        fetch(b + 1, 1 - slot)

    def wait(p, c):
        pltpu.make_async_copy(cache_ref.at[0], dst(slot, p), sem.at[slot]).wait()
        return c

    lax.fori_loop(0, n_pages, wait, 0)
    return slot


def _paged_compress_kernel(pt_ref, cache_ref, pbn_ref, w1_ref, pek_ref, pev_ref, w1k_ref, w1v_ref, w2_ref, kg_ref,
                           o_ref, buf, rows_sc, sem, *, n_pages):
    slot = _paged_prefetch(
        pt_ref, cache_ref, lambda sl, p: buf.at[sl, pl.ds(pl.multiple_of(p * KV_COLS, KV_COLS), KV_COLS)],
        sem, n_pages)
    for p in range(n_pages):
        for gg in range(KV_GROUPS):
            f0 = p * KV_COLS + gg * LANES
            rows_sc[gg, p * PAGE_SIZE:(p + 1) * PAGE_SIZE, :] = buf[slot, f0:f0 + LANES, :].T
    n_seg = n_pages * (PAGE_SIZE // CMP_STRIDE)
    hid2 = 2 * CMP_HIDDEN
    pas, pbs = [], []
    for gg in range(KV_GROUPS):
        xg = jnp.concatenate(
            [rows_sc[gg, pl.ds(j, n_seg, stride=CMP_STRIDE), :].astype(BF16)
             for j in range(CMP_STRIDE)], axis=1)
        pg = _dot(xg, w1_ref[...])
        pas.append(pg[:, :hid2])
        pbs.append(pg[:, hid2:])
    _compress_tail(jnp.concatenate(pas, axis=1), jnp.concatenate(pbs, axis=1), pbn_ref[0][0:1, :],
                   pek_ref, pev_ref, w1k_ref, w1v_ref, w2_ref, kg_ref, o_ref)


def _paged_compress(page_table, cache_t, pb_new, w1_gs, pek, pev, w1k, w1v, w2_bd, kg):
    ns, n_pages = page_table.shape
    s = n_pages * (PAGE_SIZE // CMP_STRIDE)
    c = pb_new.shape[-1]
    full2 = lambda b, pt: (0, 0)
    grid_spec = pltpu.PrefetchScalarGridSpec(
        num_scalar_prefetch=1,
        grid=(ns,),
        in_specs=[
            pl.BlockSpec(memory_space=pl.ANY),
            pl.BlockSpec((1, 8, c), lambda b, pt: (b, 0, 0)),
            pl.BlockSpec(w1_gs.shape, full2),
            pl.BlockSpec(pek.shape, full2),
            pl.BlockSpec(pev.shape, full2),
            pl.BlockSpec(w1k.shape, full2),
            pl.BlockSpec(w1v.shape, full2),
            pl.BlockSpec(w2_bd.shape, full2),
            pl.BlockSpec(kg.shape, full2),
        ],
        out_specs=pl.BlockSpec((1, s, KV_COLS), lambda b, pt: (b, 0, 0)),
        scratch_shapes=[pltpu.VMEM((2, n_pages * KV_COLS, PAGE_SIZE), F32),
                        pltpu.VMEM((KV_GROUPS, n_pages * PAGE_SIZE, LANES), F32), pltpu.SemaphoreType.DMA((2,))],
    )
    return pl.pallas_call(
        functools.partial(_paged_compress_kernel, n_pages=n_pages),
        out_shape=jax.ShapeDtypeStruct((ns, s, KV_COLS), BF16),
        grid_spec=grid_spec,
        compiler_params=_cparams(("arbitrary",)),
        name="paged_compress",
    )(page_table.reshape(-1), cache_t, pb_new, w1_gs, pek, pev, w1k, w1v, w2_bd, kg)


def _compress2(pa, pb, pb_new, pek, pev, w1k, w1v, w2_bd, kg):
    n, s, c = pa.shape
    full2 = lambda b: (0, 0)
    return pl.pallas_call(
        _compress2_kernel,
        out_shape=jax.ShapeDtypeStruct((n, s, KV_COLS), BF16),
        grid=(n,),
        in_specs=[
            pl.BlockSpec((1, s, c), lambda b: (b, 0, 0)),
            pl.BlockSpec((1, s, c), lambda b: (b, 0, 0)),
            pl.BlockSpec((1, 8, c), lambda b: (b, 0, 0)),
            pl.BlockSpec(pek.shape, full2),
            pl.BlockSpec(pev.shape, full2),
            pl.BlockSpec(w1k.shape, full2),
            pl.BlockSpec(w1v.shape, full2),
            pl.BlockSpec(w2_bd.shape, full2),
            pl.BlockSpec(kg.shape, full2),
        ],
        out_specs=pl.BlockSpec((1, s, KV_COLS), lambda b: (b, 0, 0)),
        compiler_params=_cparams(("parallel",)),
        name="compress2",
    )(pa, pb, pb_new, pek, pev, w1k, w1v, w2_bd, kg)


def _stack_heads(q, hg):
    return jnp.concatenate([q[:, h * LANES:(h + 1) * LANES] for h in range(hg)], axis=0).astype(BF16)


def _row_qpos(hg, tq, q0):
    r = lax.broadcasted_iota(jnp.int32, (hg * tq, 1), 0)
    return q0 + (r & (tq - 1))


def _store_heads(o_ref, lead, o, hg, tq, head0=0):
    lane = lax.broadcasted_iota(jnp.int32, (1, LANES), 1)
    for h in range(hg):
        blk = jnp.where(lane >= HEAD_DIM, o[h * tq:(h + 1) * tq], 0.0)
        c0 = (head0 + h) * LANES
        o_ref[lead + (slice(None), slice(c0, c0 + LANES))] = blk.astype(o_ref.dtype)


def _cmp_attn_kernel(q_ref, kc_ref, cov_ref, o_ref, sel_ref, *, bn, hg, tq, q_off, n_blocks, row_off, as_bias):
    i = pl.program_id(2)
    s_len = kc_ref.shape[1]
    q0 = q_off + i * tq
    qpos = _row_qpos(hg, tq, q0)
    blk_end = lax.broadcasted_iota(jnp.int32, (1, s_len), 1) * CMP_STRIDE + (CMP_BLOCK - 1)
    cond = blk_end <= qpos
    psums = []
    for b in range(bn):
        qs = _stack_heads(q_ref[b], hg)
        kc = kc_ref[b]
        s = jnp.where(cond, _dot_nt(qs, kc), NEG_INF)
        m = jnp.max(s, axis=-1, keepdims=True)
        p = jnp.where(cond, jnp.exp(s - m), 0.0)
        den = jnp.sum(p, axis=-1, keepdims=True)
        p = p / jnp.maximum(den, 1e-30)
        o = _dot(p.astype(BF16), kc)
        _store_heads(o_ref, (b,), o, hg, tq)
        ps = p[0:tq]
        for h in range(1, hg):
            ps = ps + p[h * tq:(h + 1) * tq]
        psums.append(ps)
    ps = psums[0] if bn == 1 else jnp.concatenate(psums, axis=0)
    hi, lo = _split_bf16(ps)
    cov = cov_ref[...]
    score = _dot_nt(cov, hi) + _dot_nt(cov, lo)
    nb_pad, nl = score.shape
    j = lax.broadcasted_iota(jnp.int32, (nb_pad, nl), 0) - row_off
    t = q0 + (lax.broadcasted_iota(jnp.int32, (1, nl), 1) & (tq - 1))
    cur = t >> SLC_SHIFT
    forced = (j == 0) | (j == cur) | (j == cur - 1)
    valid = j * SLC_BLOCK <= t
    score = jnp.where(valid, jnp.where(forced, FORCE_SCORE, score), NEG_INF)
    is_block = lax.bitcast_convert_type(j, jnp.uint32) < jnp.uint32(n_blocks)
    score = jnp.where(is_block, score, PAD_SCORE)
    hi = min(nb_pad, row_off + -(-n_blocks // 8) * 8)
    sc = score[row_off:hi]
    jj = lax.broadcasted_iota(jnp.int32, sc.shape, 0)
    picked = jnp.zeros(sc.shape, F32)
    for _ in range(min(SLC_TOPN, n_blocks)):
        mx = jnp.max(sc, axis=0, keepdims=True)
        jm = jnp.min(jnp.where(sc == mx, jj, nb_pad), axis=0, keepdims=True)
        pick = jj == jm
        picked = jnp.where(pick, 1.0, picked)
        sc = jnp.where(pick, -jnp.inf, sc)
    sel = jnp.concatenate([jnp.zeros((n, nl), F32) for n in (row_off,) if n] + [picked] +
                          [jnp.zeros((n, nl), F32) for n in (nb_pad - hi,) if n], axis=0)
    if as_bias:
        sel = jnp.where(is_block, (sel - 1.0) * (-NEG_INF), 0.0)
    sel_t = sel.T
    for b in range(bn):
        sel_ref[b, 0] = sel_t[b * tq:(b + 1) * tq]


def _cmp_attn(q, kc, cov_t, *, hg, tq, bn, q_off, n_blocks, out_dtype, row_off, as_bias):
    n, t, hc = q.shape
    s_len = kc.shape[1]
    nb_pad = cov_t.shape[0]
    g = KV_GROUPS
    return pl.pallas_call(
        functools.partial(_cmp_attn_kernel, bn=bn, hg=hg, tq=tq, q_off=q_off, n_blocks=n_blocks,
                          row_off=row_off, as_bias=as_bias),
        out_shape=[jax.ShapeDtypeStruct((n, t, hc), out_dtype), jax.ShapeDtypeStruct((n, g, t, nb_pad), F32)],
        grid=(n // bn, g, t // tq),
        in_specs=[
            pl.BlockSpec((bn, tq, hg * LANES), lambda b, gg, i: (b, i, gg)),
            pl.BlockSpec((bn, s_len, LANES), lambda b, gg, i: (b, 0, gg)),
            pl.BlockSpec((nb_pad, s_len), lambda b, gg, i: (0, 0)),
        ],
        out_specs=[
            pl.BlockSpec((bn, tq, hg * LANES), lambda b, gg, i: (b, i, gg)),
            pl.BlockSpec((bn, 1, tq, nb_pad), lambda b, gg, i: (b, gg, i, 0)),
        ],
        compiler_params=_cparams(("parallel", "parallel", "parallel")),
        name="cmp_attn",
    )(q, kc, cov_t)


def _flash_kernel(*refs, hg, tq, tk, window, use_sel, use_sink):
    refs = list(refs)
    sink_ref = refs.pop(0) if use_sink else None
    q_ref = refs.pop(0)
    kv_ref = refs.pop(0)
    selb_ref = refs.pop(0) if use_sel else None
    o_ref, qs_sc, m_sc, acc_sc = refs
    g = pl.program_id(1)
    i = pl.program_id(2)
    rows = hg * tq
    q0 = i * tq
    for h in range(hg):
        qh = q_ref[:, h * LANES:(h + 1) * LANES]
        if use_sel:
            qh = qh + selb_ref[0, 0]
        qs_sc[h * tq:(h + 1) * tq, :] = qh.astype(BF16)
    m_sc[...] = jnp.full((rows, LANES), NEG_INF, F32)
    acc_sc[...] = jnp.zeros((rows, LANES), F32)

    def tile(k0, masked):
        kv = kv_ref[pl.ds(k0, tk), :]
        klane = lax.broadcasted_iota(jnp.int32, (tk, LANES), 1)
        if use_sel:
            krow = lax.broadcasted_iota(jnp.int32, (tk, LANES), 0)
            onehot = jnp.where(klane - HEAD_DIM == ((k0 + krow) >> SLC_SHIFT), 1.0, 0.0).astype(BF16)
            kaug = jnp.where(klane < HEAD_DIM, kv, onehot)
        else:
            kaug = kv
        vaug = jnp.where(klane < HEAD_DIM, jnp.ones_like(kv), kv)
        s = _dot_nt(qs_sc[...], kaug)
        if masked:
            qpos = q0 + (lax.broadcasted_iota(jnp.int32, (rows, 1), 0) & (tq - 1))
            d = qpos - (k0 + lax.broadcasted_iota(jnp.int32, (1, tk), 1))
            if window is None:
                s = jnp.where(d >= 0, s, NEG_INF)
            else:
                s = jnp.where(lax.bitcast_convert_type(d, jnp.uint32) <= jnp.uint32(window), s, NEG_INF)
        m_prev = m_sc[...]
        m_new = jnp.maximum(m_prev, jnp.max(s, axis=-1, keepdims=True))
        alpha = jnp.exp(m_prev - m_new)
        p = jnp.exp(s - jnp.concatenate([m_new] * (tk // LANES), axis=1))
        acc_sc[...] = alpha * acc_sc[...] + _dot(p.astype(BF16), vaug)
        m_sc[...] = m_new

    if window is None:
        def body(jt, c):
            tile(pl.multiple_of(jt * tk, tk), False)
            return c

        lax.fori_loop(0, i, body, 0)
        tile(pl.multiple_of(i * tk, tk), True)
    else:
        tile(pl.multiple_of(jnp.maximum(q0 - window, 0), LANES), True)

    acc = acc_sc[...]
    den = pltpu.roll(acc, HEAD_DIM, 1)
    m = m_sc[...]
    if use_sink:
        sink = jnp.concatenate([jnp.full((tq, LANES), sink_ref[g * hg + h], F32) for h in range(hg)], axis=0)
        m_fin = jnp.maximum(m, sink)
        w = jnp.exp(m - m_fin)
        o = acc * w / jnp.maximum(den * w + jnp.exp(sink - m_fin), 1e-30)
    else:
        o = jnp.where(m > 0.5 * NEG_INF, acc / jnp.maximum(den, 1e-30), 0.0)
    lane = lax.broadcasted_iota(jnp.int32, (1, LANES), 1)
    o = jnp.where(lane >= HEAD_DIM, o, 0.0)
    for h in range(hg):
        o_ref[:, h * LANES:(h + 1) * LANES] = o[h * tq:(h + 1) * tq].astype(o_ref.dtype)


def _flash(q, kv, *, n, t, hg, tq, window, selb=None, sinks=None):
    hc = q.shape[1]
    nq = t // tq
    g = KV_GROUPS
    use_sel = selb is not None
    use_sink = sinks is not None
    rows = hg * tq
    tk = tq if window is None else tq + window
    if use_sel:
        assert t // SLC_BLOCK <= LANES - HEAD_DIM, "selection bias needs one upper query lane per block"
    in_specs = []
    args = []
    if use_sink:
        in_specs.append(pl.BlockSpec(memory_space=pltpu.SMEM))
        args.append(sinks)
    in_specs += [
        pl.BlockSpec((tq, hg * LANES), lambda b, gg, i: (b * nq + i, gg)),
        pl.BlockSpec((t, LANES), lambda b, gg, i: (b, gg)),
    ]
    args += [q, kv]
    if use_sel:
        in_specs.append(pl.BlockSpec((1, 1, tq, LANES), lambda b, gg, i: (b, gg, i, 0)))
        args.append(selb)
    return pl.pallas_call(
        functools.partial(_flash_kernel, hg=hg, tq=tq, tk=tk, window=window, use_sel=use_sel,
                          use_sink=use_sink),
        out_shape=jax.ShapeDtypeStruct((n * t, hc), BF16),
        grid=(n, g, nq),
        in_specs=in_specs,
        out_specs=pl.BlockSpec((tq, hg * LANES), lambda b, gg, i: (b * nq + i, gg)),
        scratch_shapes=[pltpu.VMEM((rows, LANES), BF16), pltpu.VMEM((rows, LANES), F32),
                        pltpu.VMEM((rows, LANES), F32)],
        compiler_params=_cparams(("parallel", "parallel", "parallel")),
        name="flash",
    )(*args)


def _attn_kernel(*refs, hg, tq, bn, window, q_off, k_off, use_sink):
    refs = list(refs)
    sink_ref = refs.pop(0) if use_sink else None
    q_ref, buf_ref, new_ref, o_ref = refs
    qpos = _row_qpos(hg, tq, q_off)

    def valid(k0, width):
        d = qpos - (k0 + lax.broadcasted_iota(jnp.int32, (1, width), 1))
        return lax.bitcast_convert_type(d, jnp.uint32) <= jnp.uint32(window)

    valid_buf = valid(k_off, buf_ref.shape[2])
    valid_new = valid(q_off, new_ref.shape[2])
    for g in range(KV_GROUPS):
        if use_sink:
            sink = jnp.concatenate([jnp.full((tq, 1), sink_ref[g * hg + h], F32) for h in range(hg)], axis=0)
        for b in range(bn):
            qs = _stack_heads(q_ref[b, :, g * hg * LANES:(g + 1) * hg * LANES], hg)
            kb = buf_ref[b, g * LANES:(g + 1) * LANES, :].astype(BF16)
            kn = new_ref[b, g * LANES:(g + 1) * LANES, :].astype(BF16)
            sb = jnp.where(valid_buf, _dot(qs, kb), NEG_INF)
            sn = jnp.where(valid_new, _dot(qs, kn), NEG_INF)
            m = jnp.maximum(jnp.max(sb, axis=-1, keepdims=True), jnp.max(sn, axis=-1, keepdims=True))
            if use_sink:
                m = jnp.maximum(m, sink)
            pb = jnp.exp(sb - m)
            pn = jnp.exp(sn - m)
            den = jnp.sum(pb, axis=-1, keepdims=True) + jnp.sum(pn, axis=-1, keepdims=True)
            if use_sink:
                den = den + jnp.exp(sink - m)
            pv = _dot_nt(pb.astype(BF16), kb) + _dot_nt(pn.astype(BF16), kn)
            o = jnp.where(m > 0.5 * NEG_INF, pv / jnp.maximum(den, 1e-30), 0.0)
            _store_heads(o_ref, (b,), o, hg, tq, head0=g * hg)


def _attn(q, buf_t, new_t, *, hg, window, q_off, k_off, sinks=None, bn=8):
    n, tq, hc = q.shape
    use_sink = sinks is not None
    in_specs = []
    args = []
    if use_sink:
        in_specs.append(pl.BlockSpec(memory_space=pltpu.SMEM))
        args.append(sinks)
    in_specs += [
        pl.BlockSpec((bn, tq, hc), lambda b: (b, 0, 0)),
        pl.BlockSpec((bn,) + buf_t.shape[1:], lambda b: (b, 0, 0)),
        pl.BlockSpec((bn,) + new_t.shape[1:], lambda b: (b, 0, 0)),
    ]
    args += [q, buf_t, new_t]
    return pl.pallas_call(
        functools.partial(_attn_kernel, hg=hg, tq=tq, bn=bn, window=window, q_off=q_off, k_off=k_off,
                          use_sink=use_sink),
        out_shape=jax.ShapeDtypeStruct((n, tq, hc), F32),
        grid=(n // bn,),
        in_specs=in_specs,
        out_specs=pl.BlockSpec((bn, tq, hc), lambda b: (b, 0, 0)),
        compiler_params=_cparams(("parallel",)),
        name="attn",
    )(*args)


def _paged_slc_kernel(pt_ref, cache_ref, q_ref, new_ref, selb_ref, oh_ref, o_ref, buf, sem, *, n_pages, hg, tq,
                      q_off):
    past = n_pages * PAGE_SIZE
    tk = buf.shape[2]
    rng_keys = (LANES - HEAD_DIM) * SLC_BLOCK
    slot = _paged_prefetch(
        pt_ref, cache_ref, lambda sl, p: buf.at[sl, :, pl.ds(pl.multiple_of(p * PAGE_SIZE, PAGE_SIZE), PAGE_SIZE)],
        sem, n_pages)
    buf[slot, :, past:] = new_ref[0]
    kv_t = buf[slot].astype(BF16)
    rows = hg * tq
    zero = jnp.zeros((rows, LANES), F32)
    q = q_ref[0]
    stack = lambda g: jnp.concatenate([q[:, (g * hg + h) * LANES:(g * hg + h + 1) * LANES] for h in range(hg)],
                                      axis=0)
    oh = oh_ref[...]
    kaug_t = jnp.concatenate([kv_t[0:HEAD_DIM], oh, kv_t[LANES:LANES + HEAD_DIM], oh], axis=0)
    qg = [stack(g) for g in range(KV_GROUPS)]
    s_parts = []
    for rr in range(-(-tk // rng_keys)):
        qa = [qg[g] + jnp.concatenate([selb_ref[0, g, rr]] * hg, axis=0) for g in range(KV_GROUPS)]
        q2 = jnp.concatenate([jnp.concatenate([qa[0], zero], axis=1),
                              jnp.concatenate([zero, qa[1]], axis=1)], axis=0).astype(BF16)
        s_parts.append(_dot(q2, kaug_t[:, rr * rng_keys:min(tk, (rr + 1) * rng_keys)]))
    s = jnp.concatenate(s_parts, axis=1)
    r = lax.broadcasted_iota(jnp.int32, (2 * rows, 1), 0)
    qpos = q_off + (r & (tq - 1))
    s = jnp.where(qpos - lax.broadcasted_iota(jnp.int32, (1, tk), 1) >= 0, s, NEG_INF)
    m = jnp.max(s, axis=-1, keepdims=True)
    p = jnp.exp(s - m)
    den = jnp.sum(p, axis=-1, keepdims=True)
    o = jnp.where(m > 0.5 * NEG_INF, _dot_nt(p.astype(BF16), kv_t) / jnp.maximum(den, 1e-30), 0.0)
    lane = lax.broadcasted_iota(jnp.int32, (1, LANES), 1)
    for g in range(KV_GROUPS):
        for h in range(hg):
            blk = o[g * rows + h * tq:g * rows + (h + 1) * tq, g * LANES:(g + 1) * LANES]
            c0 = (g * hg + h) * LANES
            o_ref[0, :, c0:c0 + LANES] = jnp.where(lane >= HEAD_DIM, blk, 0.0)


def _paged_slc(page_table, cache_t, q, new_t, sel, *, hg, q_off):
    ns, n_pages = page_table.shape
    _, tq, hc = q.shape
    tk = (n_pages + 1) * PAGE_SIZE
    assert new_t.shape == (ns, KV_COLS, PAGE_SIZE)
    n_slot = LANES - HEAD_DIM
    n_rng = -(-tk // (n_slot * SLC_BLOCK))
    assert n_rng * n_slot <= sel.shape[-1]
    bias = (sel - 1.0) * (-NEG_INF)
    selb = jnp.stack([jnp.pad(bias[..., rr * n_slot:(rr + 1) * n_slot], ((0, 0), (0, 0), (0, 0), (HEAD_DIM, 0)))
                      for rr in range(n_rng)], axis=2)
    onehot = (((jnp.arange(tk)[None, :] >> SLC_SHIFT) & (n_slot - 1)) == jnp.arange(n_slot)[:, None]).astype(BF16)
    grid_spec = pltpu.PrefetchScalarGridSpec(
        num_scalar_prefetch=1,
        grid=(ns,),
        in_specs=[
            pl.BlockSpec(memory_space=pl.ANY),
            pl.BlockSpec((1, tq, hc), lambda b, pt: (b, 0, 0)),
            pl.BlockSpec((1, KV_COLS, PAGE_SIZE), lambda b, pt: (b, 0, 0)),
            pl.BlockSpec((1, KV_GROUPS, n_rng, tq, LANES), lambda b, pt: (b, 0, 0, 0, 0)),
            pl.BlockSpec(onehot.shape, lambda b, pt: (0, 0)),
        ],
        out_specs=pl.BlockSpec((1, tq, hc), lambda b, pt: (b, 0, 0)),
        scratch_shapes=[pltpu.VMEM((2, KV_COLS, tk), F32), pltpu.SemaphoreType.DMA((2,))],
    )
    return pl.pallas_call(
        functools.partial(_paged_slc_kernel, n_pages=n_pages, hg=hg, tq=tq, q_off=q_off),
        out_shape=jax.ShapeDtypeStruct((ns, tq, hc), F32),
        grid_spec=grid_spec,
        compiler_params=_cparams(("arbitrary",)),
        name="paged_slc",
    )(page_table.reshape(-1), cache_t, q, new_t, selb, onehot)


def _out0_kernel(*refs, bounds):
    np_ = len(bounds)
    x, pool, oc, osl, ow = (_row_part(refs[k * np_:(k + 1) * np_], bounds) for k in range(5))
    gt_ref, eg_ref, wa_ref, wb_ref, o_ref = refs[5 * np_:]
    ghi, glo = _split_bf16(gt_ref[...])
    o = None
    for b, br in enumerate((oc, osl, ow)):
        gate = _dot(ghi, eg_ref[b]) + _dot(glo, eg_ref[b])
        term = gate * br.astype(F32)
        o = term if o is None else o + term
    mix = _dot(pool, wa_ref[...]) + _dot(o.astype(BF16), wb_ref[...])
    o_ref[...] = x + mix


def _out0(x_parts, pool_parts, cmp_parts, slc_parts, win_parts, gates, eg, wa, wb, tm=512):
    m = sum(a.shape[0] for a in x_parts)
    d = x_parts[0].shape[1]
    row = lambda w: pl.BlockSpec((tm, w), lambda i: (i, 0))
    specs, args = [], []
    for parts in (x_parts, pool_parts, cmp_parts, slc_parts, win_parts):
        sp, bounds = _row_part_specs(parts, tm)
        specs += sp
        args += list(parts)
    return pl.pallas_call(
        functools.partial(_out0_kernel, bounds=bounds),
        out_shape=jax.ShapeDtypeStruct((m, d), F32),
        grid=(m // tm,),
        in_specs=specs + [row(LANES),
                          pl.BlockSpec(eg.shape, lambda i: (0, 0, 0)),
                          pl.BlockSpec(wa.shape, lambda i: (0, 0)),
                          pl.BlockSpec(wb.shape, lambda i: (0, 0))],
        out_specs=row(d),
        compiler_params=_cparams(("parallel",)),
        name="out0",
    )(*args, gates, eg, wa, wb)


def _ffn_kernel(x_ref, g_ref, wg_ref, wu_ref, wd_ref, o_ref, h_sc, acc_sc):
    f = pl.program_id(1)

    @pl.when(f == 0)
    def _():
        x = x_ref[...]
        ms = jnp.mean(x * x, axis=-1, keepdims=True)
        h_sc[...] = (x * lax.rsqrt(ms + RMS_EPS) * g_ref[...]).astype(BF16)
        acc_sc[...] = x

    h = h_sc[...]
    a = _dot(h, wg_ref[...])
    u = _dot(h, wu_ref[...])
    act = (a * jax.nn.sigmoid(a) * u).astype(BF16)
    acc_sc[...] += _dot(act, wd_ref[...])

    @pl.when(f == pl.num_programs(1) - 1)
    def _():
        o_ref[...] = acc_sc[...]


def _ffn(x, g, wg, wu, wd, tm=512, nf=2):
    m, d = x.shape
    fdim = wg.shape[1]
    tf = fdim // nf
    return pl.pallas_call(
        _ffn_kernel,
        out_shape=jax.ShapeDtypeStruct((m, d), F32),
        grid=(m // tm, nf),
        in_specs=[
            pl.BlockSpec((tm, d), lambda i, f: (i, 0)),
            pl.BlockSpec((1, d), lambda i, f: (0, 0)),
            pl.BlockSpec((d, tf), lambda i, f: (0, f)),
            pl.BlockSpec((d, tf), lambda i, f: (0, f)),
            pl.BlockSpec((tf, d), lambda i, f: (f, 0)),
        ],
        out_specs=pl.BlockSpec((tm, d), lambda i, f: (i, 0)),
        scratch_shapes=[pltpu.VMEM((tm, d), BF16), pltpu.VMEM((tm, d), F32)],
        compiler_params=_cparams(("parallel", "arbitrary")),
        name="ffn",
    )(x, g, wg, wu, wd)


def _out1_kernel(*refs, bounds):
    np_ = len(bounds)
    x_ref = refs[0]
    attn = _row_part(refs[1:1 + np_], bounds)
    w_ref, g_ref, rwh_ref, rwl_ref, rb_ref, xo_ref, h_ref, r_ref = refs[1 + np_:]
    x = x_ref[...] + _dot(attn, w_ref[...])
    xo_ref[...] = x
    ms = jnp.mean(x * x, axis=-1, keepdims=True)
    h = x * lax.rsqrt(ms + RMS_EPS) * g_ref[...]
    h_ref[...] = h
    hi, lo = _split_bf16(h)
    logits = _dot(hi, rwh_ref[...]) + _dot(lo, rwh_ref[...]) + _dot(hi, rwl_ref[...]) + rb_ref[...]
    lane = lax.broadcasted_iota(jnp.int32, logits.shape, 1)
    logits = jnp.where(lane < N_EXPERTS, logits, -jnp.inf)
    m1 = jnp.max(logits, axis=-1, keepdims=True)
    i1 = jnp.min(jnp.where(logits == m1, lane, LANES), axis=-1, keepdims=True)
    rest = jnp.where(lane == i1, -jnp.inf, logits)
    m2 = jnp.max(rest, axis=-1, keepdims=True)
    i2 = jnp.min(jnp.where(rest == m2, lane, LANES), axis=-1, keepdims=True)
    e2 = jnp.exp(m2 - m1)
    g1 = 1.0 / (1.0 + e2)
    g2 = e2 / (1.0 + e2)
    r = jnp.where(lane == 0, i1.astype(F32), 0.0)
    r = jnp.where(lane == 1, i2.astype(F32), r)
    r = jnp.where(lane == 2, g1, r)
    r = jnp.where(lane == 3, g2, r)
    r_ref[...] = r


def _out1(x, o_parts, w, g, rwh, rwl, rb, tm=512):
    m, d = x.shape
    row = lambda wd: pl.BlockSpec((tm, wd), lambda i: (i, 0))
    full = lambda a: pl.BlockSpec(a.shape, lambda i: (0, 0))
    o_specs, bounds = _row_part_specs(o_parts, tm)
    return pl.pallas_call(
        functools.partial(_out1_kernel, bounds=bounds),
        out_shape=[jax.ShapeDtypeStruct((m, d), F32), jax.ShapeDtypeStruct((m, d), F32),
                   jax.ShapeDtypeStruct((m, LANES), F32)],
        grid=(m // tm,),
        in_specs=[row(d)] + o_specs + [full(w), full(g), full(rwh), full(rwl), full(rb)],
        out_specs=[row(d), row(d), row(LANES)],
        compiler_params=_cparams(("parallel",)),
        name="out1",
    )(x, *o_parts, w, g, rwh, rwl, rb)


def _moe_kernel(be_ref, nb_ref, x_ref, wg_ref, wu_ref, wd_ref, o_ref, acc_sc):
    b = pl.program_id(0)
    f = pl.program_id(1)
    used = b < nb_ref[0]

    @pl.when(f == 0)
    def _():
        acc_sc[...] = jnp.zeros_like(acc_sc)

    @pl.when(used)
    def _():
        x = x_ref[...]
        a = _dot(x, wg_ref[0])
        u = _dot(x, wu_ref[0])
        act = (a * jax.nn.sigmoid(a) * u).astype(BF16)
        acc_sc[...] += _dot(act, wd_ref[0])

    @pl.when(f == pl.num_programs(1) - 1)
    def _():
        o_ref[...] = acc_sc[...]


def _moe(xs, blk_e, n_used, wg, wu, wd, tm, tf):
    cap, d = xs.shape
    fdim = wg.shape[2]
    grid_spec = pltpu.PrefetchScalarGridSpec(
        num_scalar_prefetch=2,
        grid=(cap // tm, fdim // tf),
        in_specs=[
            pl.BlockSpec((tm, d), lambda b, f, be, nb: (b, 0)),
            pl.BlockSpec((1, d, tf), lambda b, f, be, nb: (be[b], 0, f)),
            pl.BlockSpec((1, d, tf), lambda b, f, be, nb: (be[b], 0, f)),
            pl.BlockSpec((1, tf, d), lambda b, f, be, nb: (be[b], f, 0)),
        ],
        out_specs=pl.BlockSpec((tm, d), lambda b, f, be, nb: (b, 0)),
        scratch_shapes=[pltpu.VMEM((tm, d), F32)],
    )
    return pl.pallas_call(
        _moe_kernel,
        out_shape=jax.ShapeDtypeStruct((cap, d), F32),
        grid_spec=grid_spec,
        compiler_params=_cparams(("parallel", "arbitrary")),
        name="moe",
    )(blk_e, n_used, xs, wg, wu, wd)


def _start_rows(idx_ref, src_ref, dst, sem, base, n_rows):
    group = 8
    assert n_rows % group == 0

    def start(rg, c):
        for u in range(group):
            r = rg * group + u
            pltpu.make_async_copy(src_ref.at[pl.ds(idx_ref[base + r], 1)], dst.at[pl.ds(r, 1)],
                                  sem).start(priority=u % 2)
        return c

    lax.fori_loop(0, n_rows // group, start, 0)


def _wait_rows(src_ref, dst, sem, n_rows):
    def wait(r, c):
        pltpu.make_async_copy(src_ref.at[pl.ds(0, 1)], dst.at[pl.ds(r, 1)], sem).wait()
        return c

    lax.fori_loop(0, n_rows, wait, 0, unroll=8)


def _dispatch_kernel(idx_ref, h_ref, o_ref, rows_sc, sem, *, tm):
    _start_rows(idx_ref, h_ref, rows_sc, sem.at[0], pl.program_id(0) * tm, tm)
    _wait_rows(h_ref, rows_sc, sem.at[0], tm)
    o_ref[...] = rows_sc[...].astype(o_ref.dtype)


def _dispatch(row_tok, h, tm):
    cap = row_tok.shape[0]
    d = h.shape[1]
    grid_spec = pltpu.PrefetchScalarGridSpec(
        num_scalar_prefetch=1,
        grid=(cap // tm,),
        in_specs=[pl.BlockSpec(memory_space=pl.ANY)],
        out_specs=pl.BlockSpec((tm, d), lambda b, idx: (b, 0)),
        scratch_shapes=[pltpu.VMEM((tm, d), h.dtype), pltpu.SemaphoreType.DMA((1,))],
    )
    return pl.pallas_call(
        functools.partial(_dispatch_kernel, tm=tm),
        out_shape=jax.ShapeDtypeStruct((cap, d), BF16),
        grid_spec=grid_spec,
        compiler_params=_cparams(("arbitrary",)),
        name="moe_dispatch",
    )(row_tok, h)


def _combine_kernel(dest_ref, x_ref, r_ref, ys_ref, o_ref, rows_sc, sem, *, tm, n_tok):
    t0 = pl.program_id(0) * tm
    for k in range(2):
        _start_rows(dest_ref, ys_ref, rows_sc.at[k], sem.at[0], k * n_tok + t0, tm)
    for k in range(2):
        _wait_rows(ys_ref, rows_sc.at[k], sem.at[0], tm)
    r = r_ref[...]
    o_ref[...] = x_ref[...] + (r[:, 2:3] * rows_sc[0] + r[:, 3:4] * rows_sc[1])


def _combine(dest, x, r, ys, tm=512):
    m, d = x.shape
    grid_spec = pltpu.PrefetchScalarGridSpec(
        num_scalar_prefetch=1,
        grid=(m // tm,),
        in_specs=[
            pl.BlockSpec((tm, d), lambda b, idx: (b, 0)),
            pl.BlockSpec((tm, LANES), lambda b, idx: (b, 0)),
            pl.BlockSpec(memory_space=pl.ANY),
        ],
        out_specs=pl.BlockSpec((tm, d), lambda b, idx: (b, 0)),
        scratch_shapes=[pltpu.VMEM((2, tm, d), F32), pltpu.SemaphoreType.DMA((1,))],
    )
    return pl.pallas_call(
        functools.partial(_combine_kernel, tm=tm, n_tok=m),
        out_shape=jax.ShapeDtypeStruct((m, d), F32),
        grid_spec=grid_spec,
        compiler_params=_cparams(("arbitrary",)),
        name="moe_combine",
    )(dest, x, r, ys)


def _pad_heads_cols(w, heads):
    d = w.shape[0]
    w = w.reshape(d, heads, HEAD_DIM)
    return jnp.pad(w, ((0, 0), (0, 0), (0, LANES - HEAD_DIM))).reshape(d, heads * LANES)


def _pad_heads_rows(w, heads):
    d = w.shape[1]
    w = w.reshape(heads, HEAD_DIM, d)
    return jnp.pad(w, ((0, 0), (LANES - HEAD_DIM, 0), (0, 0))).reshape(heads * LANES, d)


def _head_gain(gain, heads, scale):
    g = jnp.pad(gain.astype(F32) * scale, (0, LANES - HEAD_DIM))
    return jnp.tile(g, heads)


def _key_gain(gain):
    return jnp.concatenate([gain.astype(F32), jnp.ones((HEAD_DIM,), F32)])


def _cover_t(n_cmp_pad, n_cmp, n_blocks, nb_pad, row_off):
    m = jnp.arange(n_cmp_pad)
    c_start = m * CMP_STRIDE
    c_end = c_start + CMP_BLOCK - 1
    jb = jnp.arange(nb_pad) - row_off
    b_start = jb * SLC_BLOCK
    b_end = b_start + SLC_BLOCK - 1
    cov = (c_start[None, :] <= b_end[:, None]) & (c_end[None, :] >= b_start[:, None])
    cov = cov & (m[None, :] < n_cmp) & (jb[:, None] >= 0) & (jb[:, None] < n_blocks)
    return cov.astype(BF16)


def _route(r, n_tok, tm):
    top_e = r[:, 0:2].astype(jnp.int32)
    n_asg = n_tok * 2
    flat_e = top_e.reshape(n_asg)
    onehot = (flat_e[:, None] == jnp.arange(N_EXPERTS)[None, :]).astype(jnp.int32)
    csum = jnp.cumsum(onehot, axis=0)
    counts = csum[-1]
    padded = (counts + tm - 1) // tm * tm
    start = jnp.cumsum(counts) - counts
    pend = jnp.cumsum(padded)
    pstart = pend - padded
    dest = jnp.sum(onehot * (pstart[None, :] + csum - onehot), axis=1)
    n_blk = n_asg // tm + N_EXPERTS
    cap = n_blk * tm
    blk_e = jnp.minimum(jnp.sum(jnp.arange(n_blk)[:, None] * tm >= pend[None, :], axis=1), N_EXPERTS - 1)
    order = jnp.argsort(flat_e, stable=True)
    k = jnp.arange(cap, dtype=jnp.int32) - jnp.repeat(pstart[blk_e], tm)
    src = jnp.clip(jnp.repeat(start[blk_e], tm) + k, 0, n_asg - 1)
    row_tok = jnp.where(k < jnp.repeat(counts[blk_e], tm), order[src] // 2, 0).astype(jnp.int32)
    n_used = (pend[-1] // tm).astype(jnp.int32).reshape(1)
    dest_t = dest.astype(jnp.int32).reshape(n_tok, 2).T.reshape(n_asg)
    return dest_t, row_tok, blk_e.astype(jnp.int32), n_used


def kernel(x_prompt, x_sample, state_pool, cache_nsa_cmp, cache_nsa_slc, cache_nsa_win, cache_swa, page_table, norm0_mix, w_in0, pool_w, pool_scale, nsa_q_gain, nsa_k_gain, cmp_k_w1, cmp_k_w2, cmp_k_pe, cmp_v_w1, cmp_v_w2, cmp_v_pe, w_out0, norm0_ffn, ffn_w_gate, ffn_w_up, ffn_w_down, norm1_mix, w_in1, swa_q_gain, swa_k_gain, swa_sinks, w_out1, norm1_ffn, router_w, router_b, moe_w_gate, moe_w_up, moe_w_down):
    nb, t, d = x_prompt.shape
    ns, s_len, _ = x_sample.shape
    mp = nb * t
    msz = ns * s_len
    past = page_table.shape[1] * PAGE_SIZE
    total = past + s_len
    g = KV_GROUPS

    x_parts = (x_prompt.reshape(mp, d), x_sample.reshape(msz, d))

    c_q = POOL_CH
    c_kv = c_q + NSA_HEADS * HEAD_DIM
    c_gate = c_kv + 3 * KV_COLS
    n_gate = 3 * NSA_HEADS
    w0 = jnp.concatenate([
        w_in0[:, :c_q],
        _pad_heads_cols(w_in0[:, c_q:c_kv], NSA_HEADS),
        w_in0[:, c_kv:c_gate],
        jnp.pad(w_in0[:, c_gate:], ((0, 0), (0, LANES - n_gate))),
    ], axis=1).astype(BF16)
    qw = NSA_HEADS * LANES
    aux0 = jnp.concatenate([
        jnp.ones((POOL_CH,), F32),
        _head_gain(nsa_q_gain, NSA_HEADS, ATTN_SCALE),
        jnp.ones((KV_COLS,), F32),
        jnp.tile(_key_gain(nsa_k_gain[1]), g),
        jnp.tile(_key_gain(nsa_k_gain[2]), g),
        jnp.ones((LANES,), F32),
    ]).reshape(1, -1)
    o0 = POOL_CH
    o1 = o0 + qw
    o2 = o1 + KV_COLS
    o3 = o2 + KV_COLS
    o4 = o3 + KV_COLS
    segs0 = (("raw", 0, POOL_CH), ("qnorm", o0, qw), ("raw", o1, KV_COLS), ("kvnorm", o2, KV_COLS),
             ("kvnorm", o3, KV_COLS), ("sigmoid", o4, LANES))
    outs0 = ((POOL_CH, F32), (qw, F32), (KV_COLS, F32), (KV_COLS, F32), (KV_COLS, BF16),
             (KV_COLS, F32), (KV_COLS, BF16), (LANES, F32))
    u_all, q_all, cmp_all, slc_all, slc_bf, win_all, win_bf, gates_all = _proj(
        x_parts, norm0_mix.reshape(1, d), w0, aux0, segs0, outs0, 512, "proj0")

    pool_w_bf = pool_w.astype(BF16)
    pool_scale2 = pool_scale.reshape(1, POOL_CH).astype(F32)

    u_p = u_all[:mp].reshape(nb, t, POOL_CH)
    u_s = u_all[mp:].reshape(ns, s_len, POOL_CH)
    pool_o_p = _pool_prompt(u_p, pool_w_bf, pool_scale2)
    u_ext = jnp.concatenate([state_pool.astype(F32), u_s], axis=1)
    x_ext = jnp.pad(u_ext, ((0, 0), (1, 0), (0, 0)))
    pool_o_s = _pool_sample(x_ext, pool_w_bf, pool_scale2, s_len)
    pool_parts = (pool_o_p.reshape(mp, POOL_CH), pool_o_s.reshape(msz, POOL_CH).astype(BF16))

    def w1_parts(w1):
        return w1.reshape(2, CMP_STRIDE, HEAD_DIM, CMP_HIDDEN)

    w_sel = jnp.stack([w1_parts(cmp_k_w1), w1_parts(cmp_v_w1)] * g, axis=0)
    w1_bd = jnp.einsum("cpjdh,ce->jcdpeh", w_sel, jnp.eye(2 * g, dtype=F32))
    w1_bd = w1_bd.reshape(CMP_STRIDE, KV_COLS, 2 * 2 * g * CMP_HIDDEN).astype(BF16)
    w2_sel = jnp.stack([cmp_k_w2, cmp_v_w2] * g, axis=0)
    w2_bd = jnp.einsum("chd,ce->ched", w2_sel, jnp.eye(2 * g, dtype=F32))
    w2_bd = w2_bd.reshape(2 * g * CMP_HIDDEN, KV_COLS).astype(BF16)
    pek = jnp.pad(cmp_k_pe.reshape(1, -1), ((0, 7), (0, 0))).astype(BF16)
    pev = jnp.pad(cmp_v_pe.reshape(1, -1), ((0, 7), (0, 0))).astype(BF16)
    w1k = cmp_k_w1.astype(BF16)
    w1v = cmp_v_w1.astype(BF16)
    kg0 = _key_gain(nsa_k_gain[0]).reshape(1, LANES)
    seg_cols = CMP_STRIDE * KV_COLS
    half = 2 * g * CMP_HIDDEN

    n_seg_p = t // CMP_STRIDE
    pa_p, pb_p = _compress1(cmp_all[:mp].reshape(nb * n_seg_p, seg_cols), w1_bd, 256)
    kcv_p = _compress2(pa_p.reshape(nb, n_seg_p, half), pb_p.reshape(nb, n_seg_p, half),
                       jnp.zeros((nb, 8, half), F32), pek, pev, w1k, w1v, w2_bd, kg0)
    n_seg_s = past // CMP_STRIDE
    pad_new = -(-total // CMP_STRIDE) * CMP_STRIDE - past
    assert pad_new == CMP_STRIDE, "the new rows must fit one segment"
    new_seg = jnp.pad(cmp_all[mp:].reshape(ns, s_len * KV_COLS), ((0, 0), (0, (pad_new - s_len) * KV_COLS)))
    _, pb_new = _compress1(new_seg, w1_bd, ns)
    pb_new = jnp.pad(pb_new.reshape(ns, 1, half), ((0, 0), (0, 7), (0, 0)))
    w_kv = jnp.stack([w1_parts(cmp_k_w1), w1_parts(cmp_v_w1)], axis=0)
    w1_gs = jnp.einsum("kpjdh,ke->jkdpeh", w_kv, jnp.eye(2, dtype=F32))
    w1_gs = w1_gs.reshape(CMP_STRIDE * 2 * HEAD_DIM, 2 * 2 * CMP_HIDDEN).astype(BF16)
    kcv_s = _paged_compress(page_table, _pages_t(cache_nsa_cmp), pb_new, w1_gs, pek, pev, w1k, w1v, w2_bd, kg0)

    hg0 = NSA_HEADS // g
    q_p = q_all[:mp].reshape(nb, t, qw)
    q_s = q_all[mp:].reshape(ns, s_len, qw)
    nblk_p = t // SLC_BLOCK
    cov_p = _cover_t(n_seg_p, n_seg_p - 1, nblk_p, LANES, HEAD_DIM)
    o_cmp_p, selb_p = _cmp_attn(q_p, kcv_p, cov_p, hg=hg0, tq=256, bn=1, q_off=0, n_blocks=nblk_p,
                                out_dtype=BF16, row_off=HEAD_DIM, as_bias=True)
    nblk_s = -(-total // SLC_BLOCK)
    nbp_s = -(-nblk_s // LANES) * LANES
    n_cmp_s = -(-total // CMP_STRIDE) - 1
    cov_s = _cover_t(n_seg_s, n_cmp_s, nblk_s, nbp_s, 0)
    o_cmp_s, sel_s = _cmp_attn(q_s, kcv_s, cov_s, hg=hg0, tq=s_len, bn=LANES // s_len, q_off=past,
                               n_blocks=nblk_s, out_dtype=F32, row_off=0, as_bias=False)

    o_slc_p = _flash(q_all, slc_bf, n=nb, t=t, hg=hg0, tq=512, window=None, selb=selb_p)
    new_t = lambda rows: jnp.pad(jnp.transpose(rows.reshape(ns, s_len, KV_COLS), (0, 2, 1)),
                                 ((0, 0), (0, 0), (0, PAGE_SIZE - s_len)))
    o_slc_s = _paged_slc(page_table, _pages_t(cache_nsa_slc), q_s, new_t(slc_all[mp:]), sel_s, hg=hg0, q_off=past)

    o_win_p = _flash(q_all, win_bf, n=nb, t=t, hg=hg0, tq=256, window=NSA_WINDOW)
    wb_len = cache_nsa_win.shape[1]
    win_buf_t = _pages_t(cache_nsa_win)
    win_new_t = new_t(win_all[mp:])
    o_win_s = _attn(q_s, win_buf_t, win_new_t, hg=hg0, window=NSA_WINDOW, q_off=past, k_off=past - wb_len)

    parts = lambda a, b: (a.reshape(mp, -1), b.reshape(msz, -1).astype(BF16))
    eg = (jnp.arange(LANES)[None, :, None] ==
          (jnp.arange(3)[:, None, None] * NSA_HEADS + jnp.arange(qw)[None, None, :] // LANES)).astype(BF16)
    wa0 = w_out0[:POOL_CH].astype(BF16)
    wb0 = _pad_heads_rows(w_out0[POOL_CH:], NSA_HEADS).astype(BF16)
    x1 = _out0(x_parts, pool_parts, parts(o_cmp_p, o_cmp_s), parts(o_slc_p, o_slc_s), parts(o_win_p, o_win_s),
               gates_all, eg, wa0, wb0)
    x2 = _ffn(x1, norm0_ffn.reshape(1, d), ffn_w_gate.astype(BF16), ffn_w_up.astype(BF16),
              ffn_w_down.astype(BF16))

    c_q1 = SWA_HEADS * HEAD_DIM
    qw1 = SWA_HEADS * LANES
    w1p = jnp.concatenate([_pad_heads_cols(w_in1[:, :c_q1], SWA_HEADS), w_in1[:, c_q1:]], axis=1).astype(BF16)
    aux1 = jnp.concatenate([_head_gain(swa_q_gain, SWA_HEADS, ATTN_SCALE),
                            jnp.tile(_key_gain(swa_k_gain), g)]).reshape(1, -1)
    segs1 = (("qnorm", 0, qw1), ("kvnorm", qw1, KV_COLS))
    outs1 = ((qw1, F32), (KV_COLS, F32), (KV_COLS, BF16))
    q1_all, swa_all, swa_bf = _proj((x2,), norm1_mix.reshape(1, d), w1p, aux1, segs1, outs1, 512, "proj1")
    hg1 = SWA_HEADS // g
    sinks = swa_sinks.astype(F32)
    q1_s = q1_all[mp:].reshape(ns, s_len, qw1)
    o1_p = _flash(q1_all, swa_bf, n=nb, t=t, hg=hg1, tq=128, window=SWA_WINDOW, sinks=sinks)
    sb_len = cache_swa.shape[1]
    swa_buf_t = _pages_t(cache_swa)
    swa_new_t = new_t(swa_all[mp:])
    o1_s = _attn(q1_s, swa_buf_t, swa_new_t, hg=hg1, window=SWA_WINDOW, q_off=past, k_off=past - sb_len,
                 sinks=sinks)

    rw = jnp.pad(router_w.astype(F32), ((0, 0), (0, LANES - N_EXPERTS)))
    rwh, rwl = _split_bf16(rw)
    rb = jnp.pad(router_b.astype(F32), (0, LANES - N_EXPERTS)).reshape(1, LANES)
    x3, h3, r = _out1(x2, parts(o1_p, o1_s), _pad_heads_rows(w_out1, SWA_HEADS).astype(BF16),
                      norm1_ffn.reshape(1, d), rwh, rwl, rb)
    m_all = mp + msz
    tm_moe = 512
    dest_t, row_tok, blk_e, n_used = _route(r, m_all, tm_moe)
    n_blk = row_tok.shape[0] // tm_moe
    xs = _dispatch(row_tok, h3, tm_moe * max(k for k in (4, 2, 1) if n_blk % k == 0))
    ys = _moe(xs, blk_e, n_used, moe_w_gate.astype(BF16), moe_w_up.astype(BF16), moe_w_down.astype(BF16),
              tm_moe, moe_w_gate.shape[2] // 2)
    x4 = _combine(dest_t, x3, r, ys)

    y_p = x4[:mp].reshape(nb, t, d)
    y_s = x4[mp:].reshape(ns, s_len, d)
    rows5 = lambda a, n, l: a.reshape(n, l, g, 2, HEAD_DIM)
    keep_w = min(NSA_WINDOW, t)
    keep_s = min(SWA_WINDOW, t)
    pool_p = u_p[:, t - POOL_STATE:]
    cmp_p = rows5(cmp_all[:mp], nb, t)
    slc_p_out = rows5(slc_all[:mp], nb, t)
    win_p_out = rows5(win_all[:mp], nb, t)[:, t - keep_w:]
    swa_p_out = rows5(swa_all[:mp], nb, t)[:, t - keep_s:]
    pool_s = u_ext[:, -POOL_STATE:]
    cmp_s = rows5(cmp_all[mp:], ns, s_len)
    slc_s_out = rows5(slc_all[mp:], ns, s_len)
    def shifted(buf_t, add_t, length):
        out_t = jnp.concatenate([buf_t[:, :, s_len:], add_t[:, :, :s_len]], axis=2)
        return jnp.transpose(out_t.reshape(ns, g, 2, HEAD_DIM, length), (0, 4, 1, 2, 3))

    win_s_out = shifted(win_buf_t, win_new_t, wb_len)
    swa_s_out = shifted(swa_buf_t, swa_new_t, sb_len)
    return (y_p, y_s, pool_p, cmp_p, slc_p_out, win_p_out, swa_p_out, pool_s, cmp_s, slc_s_out, win_s_out,
            swa_s_out)
```
